```python
import math
import jax, jax.numpy as jnp
from jax import lax
import numpy as np

D_MODEL = 1024
BATCH = 2
SEQ = 8192
DEPTH = 2

D_MIX = D_MODEL
D_FF = int(math.ceil(8 * D_MODEL / 3 / 256)) * 256
EPS = 1e-6
S5_W = D_MIX // 4
S5_GROUP = 16
S5_G = S5_W // S5_GROUP
S5_P = 64
HG_W = D_MIX // 4
HG_HEADS = 4
HG_DK = HG_W // HG_HEADS
HG_DV = HG_W // HG_HEADS
HG_CHUNK = 64
NSA_W = D_MIX - S5_W - HG_W
NSA_DH = 64
NSA_H = NSA_W // NSA_DH
NSA_G = 2
NSA_R = NSA_H // NSA_G
CMP_LEN = 32
CMP_STRIDE = 16
CMP_RATIO = CMP_LEN // CMP_STRIDE
SLC_LEN = 64
SLC_RATIO = SLC_LEN // CMP_STRIDE
N_SEL = 16
WIN = 512
Q_BLOCK = 128
FORCE_BONUS = 1e4
NEG_INF = -1e30
TINY = 1e-30
REL_BUCKETS = 32
REL_MAX_DIST = 1024
KV_W = NSA_G * NSA_DH
D_IN = S5_W + 4 * HG_W + NSA_W + 6 * KV_W + 3 * NSA_H

kernel_name = 'hymba_s5_hgrn2_nsa_macaron'


def rmsnorm(x, g):
    xf = x.astype(jnp.float32)
    y = xf * lax.rsqrt(jnp.mean(xf * xf, axis=-1, keepdims=True) + EPS)
    return y.astype(x.dtype) * g


def swiglu(h, wg, wu, wd):
    return (jax.nn.silu(h @ wg) * (h @ wu)) @ wd


def masked_softmax(s, mask):
    s = jnp.where(mask, s.astype(jnp.float32), NEG_INF)
    m = jnp.max(s, axis=-1, keepdims=True)
    e = jnp.where(mask, jnp.exp(s - m), 0.0)
    return e / jnp.maximum(jnp.sum(e, axis=-1, keepdims=True), TINY)


def t5_bucket(dist):
    n = jnp.maximum(dist, 0)
    max_exact = REL_BUCKETS // 2
    nf = jnp.maximum(n, max_exact).astype(jnp.float32)
    large = max_exact + (jnp.log(nf / max_exact) / math.log(REL_MAX_DIST / max_exact)
                         * (REL_BUCKETS - max_exact)).astype(jnp.int32)
    large = jnp.minimum(large, REL_BUCKETS - 1)
    return jnp.where(n < max_exact, n, large)


def s5_mixer(u, lam_re, lam_im, log_dt, b_re, b_im, c_re, c_im, d, w_glu):
    B_, L, _ = u.shape
    ug = u.astype(jnp.float32).reshape(B_, L, S5_G, S5_GROUP)
    lr, li = lam_re.astype(jnp.float32), lam_im.astype(jnp.float32)
    dt = jnp.exp(log_dt.astype(jnp.float32))[:, None]
    mag = jnp.exp(lr * dt)
    ab_re, ab_im = mag * jnp.cos(li * dt), mag * jnp.sin(li * dt)
    den = lr * lr + li * li
    nr, ni = ab_re - 1.0, ab_im
    g_re = (nr * lr + ni * li) / den
    g_im = (ni * lr - nr * li) / den
    br, bi = b_re.astype(jnp.float32), b_im.astype(jnp.float32)
    bb_re = g_re[..., None] * br - g_im[..., None] * bi
    bb_im = g_re[..., None] * bi + g_im[..., None] * br
    bu_re = jnp.einsum('blgh,gph->blgp', ug, bb_re)
    bu_im = jnp.einsum('blgh,gph->blgp', ug, bb_im)
    a_re = jnp.broadcast_to(ab_re, bu_re.shape)
    a_im = jnp.broadcast_to(ab_im, bu_im.shape)

    def combine(e1, e2):
        a1r, a1i, b1r, b1i = e1
        a2r, a2i, b2r, b2i = e2
        return (a2r * a1r - a2i * a1i, a2r * a1i + a2i * a1r,
                a2r * b1r - a2i * b1i + b2r, a2r * b1i + a2i * b1r + b2i)

    _, _, xr, xi = lax.associative_scan(combine, (a_re, a_im, bu_re, bu_im), axis=1)
    y = (jnp.einsum('blgp,ghp->blgh', xr, c_re.astype(jnp.float32))
         - jnp.einsum('blgp,ghp->blgh', xi, c_im.astype(jnp.float32))
         + d.astype(jnp.float32) * ug)
    y = jax.nn.gelu(y.reshape(B_, L, S5_W))
    y = y * jax.nn.sigmoid(y @ w_glu.astype(jnp.float32))
    return y.astype(u.dtype)


def hgrn2_mixer(q, f_logit, i_in, g, lb, norm_gain):
    B_, L, _ = q.shape
    nc = L // HG_CHUNK
    qf = jax.nn.silu(q.astype(jnp.float32))
    lbf = lb.astype(jnp.float32)
    fl = f_logit.astype(jnp.float32)
    f = lbf + (1.0 - lbf) * jax.nn.sigmoid(fl)
    lf = jnp.log(jnp.maximum(f, TINY))
    kf = (1.0 - lbf) * jax.nn.sigmoid(-fl)
    vf = i_in.astype(jnp.float32)

    def to_chunks(a, dim):
        return a.reshape(B_, nc, HG_CHUNK, HG_HEADS, dim).transpose(1, 0, 3, 2, 4)

    causal = jnp.tril(jnp.ones((HG_CHUNK, HG_CHUNK), bool))[:, :, None]

    def step(S, inp):
        qc, kc, vc, lfc = inp
        b = jnp.cumsum(lfc, axis=2)
        diff = b[:, :, :, None, :] - b[:, :, None, :, :]
        decay = jnp.where(causal, jnp.exp(jnp.minimum(diff, 0.0)), 0.0)
        att = jnp.einsum('bhtd,bhsd,bhtsd->bhts', qc, kc, decay)
        o = att @ vc + jnp.einsum('bhtd,bhde->bhte', qc * jnp.exp(b), S)
        b_last = b[:, :, -1:, :]
        S = (jnp.exp(b_last[:, :, 0, :, None]) * S
             + jnp.einsum('bhsd,bhse->bhde', kc * jnp.exp(b_last - b), vc))
        return S, o

    S0 = jnp.zeros((B_, HG_HEADS, HG_DK, HG_DV), jnp.float32)
    _, o = lax.scan(step, S0, (to_chunks(qf, HG_DK), to_chunks(kf, HG_DK),
                               to_chunks(vf, HG_DV), to_chunks(lf, HG_DK)))
    o = o.transpose(1, 0, 3, 2, 4).reshape(B_, L, HG_HEADS, HG_DV)
    o = o * lax.rsqrt(jnp.mean(o * o, axis=-1, keepdims=True) + EPS) * norm_gain.astype(jnp.float32)
    o = o.reshape(B_, L, HG_W) * jax.nn.silu(g.astype(jnp.float32))
    return o.astype(q.dtype)


def compress_blocks(k, pos, w1, w2):
    B_, L = k.shape[0], k.shape[1]
    ch = k.reshape(B_, L // CMP_STRIDE, CMP_STRIDE, NSA_G, NSA_DH)
    n = L // CMP_STRIDE - CMP_RATIO + 1
    blocks = jnp.concatenate([ch[:, r:r + n] for r in range(CMP_RATIO)], axis=2)
    blocks = blocks + pos[None, None, :, None, :]
    flat = blocks.transpose(0, 1, 3, 2, 4).reshape(B_, n, NSA_G, CMP_LEN * NSA_DH)
    return jax.nn.gelu(flat @ w1) @ w2


def nsa_mixer(q, kv, gate_logits, pos_k, w1_k, w2_k, pos_v, w1_v, w2_v, rel_bias):
    B_, L, _ = q.shape
    qh = q.reshape(B_, L, NSA_G, NSA_R, NSA_DH)
    kvh = kv.reshape(B_, L, 6, NSA_G, NSA_DH)
    k_cr, v_cr, k_sl, v_sl, k_wn, v_wn = (kvh[:, :, j] for j in range(6))
    gates = jax.nn.sigmoid(gate_logits.astype(jnp.float32)).reshape(B_, L, NSA_G, NSA_R, 3)
    k_cmp = compress_blocks(k_cr, pos_k, w1_k, w2_k)
    v_cmp = compress_blocks(v_cr, pos_v, w1_v, w2_v)
    ncmp = k_cmp.shape[1]
    nsb = L // SLC_LEN
    n_sel = min(N_SEL, nsb)

    def sel_blocks(a):
        return a.reshape(B_, nsb, SLC_LEN, NSA_G, NSA_DH).transpose(0, 3, 1, 2, 4).reshape(
            B_, NSA_G, nsb, SLC_LEN * NSA_DH)

    k_blk, v_blk = sel_blocks(k_sl), sel_blocks(v_sl)
    k_wp = jnp.pad(k_wn, ((0, 0), (WIN, 0), (0, 0), (0, 0)))
    v_wp = jnp.pad(v_wn, ((0, 0), (WIN, 0), (0, 0), (0, 0)))
    tab = rel_bias.astype(jnp.float32)
    tab_g = tab.reshape(REL_BUCKETS, NSA_G, NSA_R).transpose(1, 0, 2)
    cmp_end = jnp.arange(ncmp) * CMP_STRIDE + CMP_LEN - 1
    blk = jnp.arange(nsb)
    tok = jnp.arange(SLC_LEN)
    scale = NSA_DH ** -0.5
    span = SLC_RATIO * (nsb - 1) + 1
    gidx = jnp.arange(NSA_G)[None, :, None, None]

    def dense_bias(dist):
        return jnp.moveaxis(tab[t5_bucket(dist)], -1, 0).reshape(NSA_G, NSA_R, *dist.shape)

    def one_block(qi):
        q0 = qi * Q_BLOCK
        qb = lax.dynamic_slice_in_dim(qh, q0, Q_BLOCK, axis=1)
        gb = lax.dynamic_slice_in_dim(gates, q0, Q_BLOCK, axis=1)
        t = q0 + jnp.arange(Q_BLOCK)
        d_c = t[:, None] - cmp_end[None, :]
        s = jnp.einsum('bqgrd,bcgd->bgrqc', qb, k_cmp).astype(jnp.float32) * scale + dense_bias(d_c)
        p_cmp = masked_softmax(s, d_c >= 0)
        o_cmp = jnp.einsum('bgrqc,bcgd->bqgrd', p_cmp.astype(v_cmp.dtype), v_cmp)
        imp = jnp.sum(p_cmp, axis=2)
        imp = jnp.pad(imp, ((0, 0), (0, 0), (0, 0), (CMP_RATIO - 1, SLC_RATIO * nsb - ncmp)))
        p_slc = sum(imp[..., m + n:m + n + span:SLC_RATIO]
                    for m in range(SLC_RATIO) for n in range(CMP_RATIO))
        blk_ok = (blk[None, :] * SLC_LEN) <= t[:, None]
        cur = (t // SLC_LEN)[:, None]
        forced = (blk[None, :] == 0) | (blk[None, :] == cur) | (blk[None, :] == cur - 1)
        score = jnp.where(blk_ok, p_slc + jnp.where(forced, FORCE_BONUS, 0.0), NEG_INF)
        _, idx = lax.top_k(score, n_sel)
        idx_f = idx.reshape(B_, NSA_G, Q_BLOCK * n_sel, 1)
        k_sel = jnp.take_along_axis(k_blk, idx_f, axis=2).reshape(B_, NSA_G, Q_BLOCK, n_sel * SLC_LEN, NSA_DH)
        v_sel = jnp.take_along_axis(v_blk, idx_f, axis=2).reshape(B_, NSA_G, Q_BLOCK, n_sel * SLC_LEN, NSA_DH)
        pos = (idx[..., None] * SLC_LEN + tok).reshape(B_, NSA_G, Q_BLOCK, n_sel * SLC_LEN)
        d_s = t[None, None, :, None] - pos
        bias_s = tab_g[gidx, t5_bucket(d_s)].transpose(0, 1, 4, 2, 3)
        s = jnp.einsum('bqgrd,bgqkd->bgrqk', qb, k_sel).astype(jnp.float32) * scale + bias_s
        p = masked_softmax(s, (d_s >= 0)[:, :, None])
        o_slc = jnp.einsum('bgrqk,bgqkd->bqgrd', p.astype(v_sel.dtype), v_sel)
        kw = lax.dynamic_slice_in_dim(k_wp, q0, WIN + Q_BLOCK, axis=1)
        vw = lax.dynamic_slice_in_dim(v_wp, q0, WIN + Q_BLOCK, axis=1)
        pos_w = q0 - WIN + jnp.arange(WIN + Q_BLOCK)
        d_w = t[:, None] - pos_w[None, :]
        ok_w = (d_w >= 0) & (d_w < WIN) & (pos_w[None, :] >= 0)
        s = jnp.einsum('bqgrd,bkgd->bgrqk', qb, kw).astype(jnp.float32) * scale + dense_bias(d_w)
        p = masked_softmax(s, ok_w)
        o_win = jnp.einsum('bgrqk,bkgd->bqgrd', p.astype(vw.dtype), vw)
        o = gb[..., 0:1] * o_cmp + gb[..., 1:2] * o_slc + gb[..., 2:3] * o_win
        return o.astype(q.dtype)

    out = lax.map(one_block, jnp.arange(L // Q_BLOCK))
    return out.transpose(1, 0, 2, 3, 4, 5).reshape(B_, L, NSA_W)


def setup_inputs(seed: int = 0) -> dict:
    key = jax.random.key(seed)
    ks = iter(jax.random.split(key, 40))

    def nrm(shape, scale):
        return jax.random.normal(next(ks), shape, jnp.float32) * scale

    def gain(shape):
        return 1.0 + nrm(shape, 0.02)

    n_idx = jnp.arange(S5_P, dtype=jnp.float32)
    return {
        'x': nrm((BATCH, SEQ, D_MODEL), 1.0),
        'ffn1_norm': gain((DEPTH, D_MODEL)),
        'ffn1_w_gate': nrm((DEPTH, D_MODEL, D_FF), D_MODEL ** -0.5),
        'ffn1_w_up': nrm((DEPTH, D_MODEL, D_FF), D_MODEL ** -0.5),
        'ffn1_w_down': nrm((DEPTH, D_FF, D_MODEL), D_FF ** -0.5),
        'mix_norm': gain((DEPTH, D_MODEL)),
        'w_in': nrm((DEPTH, D_MODEL, D_IN), D_MODEL ** -0.5),
        'w_out': nrm((DEPTH, D_MIX, D_MODEL), D_MIX ** -0.5),
        's5_lambda_re': -0.5 + nrm((DEPTH, S5_G, S5_P), 0.01),
        's5_lambda_im': math.pi * n_idx + nrm((DEPTH, S5_G, S5_P), 0.01),
        's5_log_dt': jax.random.uniform(next(ks), (DEPTH, S5_G), jnp.float32,
                                        minval=math.log(0.001), maxval=math.log(0.1)),
        's5_b_re': nrm((DEPTH, S5_G, S5_P, S5_GROUP), (2 * S5_GROUP) ** -0.5),
        's5_b_im': nrm((DEPTH, S5_G, S5_P, S5_GROUP), (2 * S5_GROUP) ** -0.5),
        's5_c_re': nrm((DEPTH, S5_G, S5_GROUP, S5_P), (2 * S5_P) ** -0.5),
        's5_c_im': nrm((DEPTH, S5_G, S5_GROUP, S5_P), (2 * S5_P) ** -0.5),
        's5_d': nrm((DEPTH, S5_G, S5_GROUP), 1.0),
        's5_w_glu': nrm((DEPTH, S5_W, S5_W), S5_W ** -0.5),
        'hgrn_lb_logits': nrm((DEPTH, HG_W), 1.0),
        'hgrn_norm': gain((DEPTH, HG_DV)),
        'nsa_cmp_pos_k': nrm((DEPTH, CMP_LEN, NSA_DH), 0.02),
        'nsa_cmp_w1_k': nrm((DEPTH, CMP_LEN * NSA_DH, NSA_DH), (CMP_LEN * NSA_DH) ** -0.5),
        'nsa_cmp_w2_k': nrm((DEPTH, NSA_DH, NSA_DH), NSA_DH ** -0.5),
        'nsa_cmp_pos_v': nrm((DEPTH, CMP_LEN, NSA_DH), 0.02),
        'nsa_cmp_w1_v': nrm((DEPTH, CMP_LEN * NSA_DH, NSA_DH), (CMP_LEN * NSA_DH) ** -0.5),
        'nsa_cmp_w2_v': nrm((DEPTH, NSA_DH, NSA_DH), NSA_DH ** -0.5),
        'rel_bias': nrm((REL_BUCKETS, NSA_H), 0.5),
        'ffn2_norm': gain((DEPTH, D_MODEL)),
        'ffn2_w_gate': nrm((DEPTH, D_MODEL, D_FF), D_MODEL ** -0.5),
        'ffn2_w_up': nrm((DEPTH, D_MODEL, D_FF), D_MODEL ** -0.5),
        'ffn2_w_down': nrm((DEPTH, D_FF, D_MODEL), D_FF ** -0.5),
        'final_norm': gain((D_MODEL,)),
    }


def reference(x, ffn1_norm, ffn1_w_gate, ffn1_w_up, ffn1_w_down, mix_norm, w_in, w_out,
              s5_lambda_re, s5_lambda_im, s5_log_dt, s5_b_re, s5_b_im, s5_c_re, s5_c_im, s5_d,
              s5_w_glu, hgrn_lb_logits, hgrn_norm, nsa_cmp_pos_k, nsa_cmp_w1_k, nsa_cmp_w2_k,
              nsa_cmp_pos_v, nsa_cmp_w1_v, nsa_cmp_w2_v, rel_bias, ffn2_norm, ffn2_w_gate,
              ffn2_w_up, ffn2_w_down, final_norm):
    gam = jax.nn.softmax(hgrn_lb_logits.astype(jnp.float32), axis=0)
    lower_bounds = jnp.cumsum(gam, axis=0) - gam[0:1]
    offs = np.cumsum([0, S5_W, HG_W, HG_W, HG_W, HG_W, NSA_W, 6 * KV_W, 3 * NSA_H])
    for l in range(DEPTH):
        x = x + 0.5 * swiglu(rmsnorm(x, ffn1_norm[l]), ffn1_w_gate[l], ffn1_w_up[l], ffn1_w_down[l])
        z = rmsnorm(x, mix_norm[l]) @ w_in[l]
        u_s5, q_hg, f_hg, i_hg, g_hg, q_nsa, kv_nsa, gate_nsa = (
            z[..., int(offs[j]):int(offs[j + 1])] for j in range(8))
        y_s5 = s5_mixer(u_s5, s5_lambda_re[l], s5_lambda_im[l], s5_log_dt[l], s5_b_re[l], s5_b_im[l],
                        s5_c_re[l], s5_c_im[l], s5_d[l], s5_w_glu[l])
        y_hg = hgrn2_mixer(q_hg, f_hg, i_hg, g_hg, lower_bounds[l], hgrn_norm[l])
        y_nsa = nsa_mixer(q_nsa, kv_nsa, gate_nsa, nsa_cmp_pos_k[l], nsa_cmp_w1_k[l], nsa_cmp_w2_k[l],
                          nsa_cmp_pos_v[l], nsa_cmp_w1_v[l], nsa_cmp_w2_v[l], rel_bias)
        y = jnp.concatenate([y_s5, y_hg, y_nsa], axis=-1).astype(x.dtype)
        x = x + y @ w_out[l]
        x = x + 0.5 * swiglu(rmsnorm(x, ffn2_norm[l]), ffn2_w_gate[l], ffn2_w_up[l], ffn2_w_down[l])
    return rmsnorm(x, final_norm)
```

```python
import functools
import math

import numpy as np
import jax
import jax.numpy as jnp
from jax import lax
from jax.experimental import pallas as pl
from jax.experimental.pallas import tpu as pltpu

BF = jnp.bfloat16
F32 = jnp.float32

EPS = 1e-6
NEG_INF = -1e30
TINY = 1e-30
FORCE_BONUS = 1e4

S5_W = 256
S5_G = 16
S5_GROUP = 16
S5_P = 64
S5_N = S5_G * S5_P
HG_W = 256
HG_HEADS = 4
HG_D = 64
NSA_W = 512
NSA_DH = 64
NSA_H = 8
NSA_G = 2
NSA_R = 4
KV_W = NSA_G * NSA_DH
CMP_LEN = 32
CMP_STRIDE = 16
SLC_LEN = 64
N_SEL = 16
WIN = 512
REL_BUCKETS = 32
REL_MAX_DIST = 1024

Q_TILE = 128
K_TILE = 128
GR_LANES = NSA_R * Q_TILE
N_TOEPLITZ = REL_MAX_DIST // K_TILE + 2
N_WIN_TILES = WIN // K_TILE + 1
HG_CHUNK = 64
HG_LEVELS = 6
SUBLANES = 8
VMEM_LIMIT_BYTES = 56 * 1024 * 1024


def _cparams(*sem):
    return pltpu.CompilerParams(dimension_semantics=sem, vmem_limit_bytes=VMEM_LIMIT_BYTES)


def _const_spec(shape):
    nd = len(shape)
    return pl.BlockSpec(shape, lambda *_: (0,) * nd)


def _rms(x, g_row):
    ms = jnp.mean(x * x, axis=-1, keepdims=True)
    return x * lax.rsqrt(ms + EPS) * g_row


def _silu(x):
    return x * jax.nn.sigmoid(x)


def _gelu_tanh(x):
    return 0.5 * x * (1.0 + jnp.tanh(math.sqrt(2.0 / math.pi) * (x + 0.044715 * (x * x * x))))


def _dot(a, b):
    return jnp.dot(a, b, preferred_element_type=F32)


def _dot_nt(a, b):
    return lax.dot_general(a, b, (((1,), (1,)), ((), ())), preferred_element_type=F32)


def _dot_tn(a, b):
    return lax.dot_general(a, b, (((0,), (0,)), ((), ())), preferred_element_type=F32)


def _dot_exact_lhs(c_bf, x):
    hi = x.astype(BF)
    r1 = x - hi.astype(F32)
    mid = r1.astype(BF)
    lo = (r1 - mid.astype(F32)).astype(BF)
    return _dot(c_bf, hi) + _dot(c_bf, mid) + _dot(c_bf, lo)


def _dot_exact_rhs(x, c_bf):
    hi = x.astype(BF)
    r1 = x - hi.astype(F32)
    mid = r1.astype(BF)
    lo = (r1 - mid.astype(F32)).astype(BF)
    return _dot(hi, c_bf) + _dot(mid, c_bf) + _dot(lo, c_bf)


def _ffn_kernel(*refs, n_chunks, tf, with_proj, with_final):
    it = iter(refs)
    x_ref = next(it)
    if with_proj:
        ys5_ref, yhg_ref, ynsa_ref, wo_ref = next(it), next(it), next(it), next(it)
    g_ref, wg_ref, wu_ref, wd_ref = next(it), next(it), next(it), next(it)
    if with_final:
        fg_ref = next(it)
    o_ref, h_scr, a_scr = next(it), next(it), next(it)

    x = x_ref[...]
    if with_proj:
        x = (x + _dot(ys5_ref[...].astype(BF), wo_ref[0:S5_W, :])
             + _dot(yhg_ref[...].astype(BF), wo_ref[S5_W:S5_W + HG_W, :])
             + _dot(ynsa_ref[...].astype(BF), wo_ref[S5_W + HG_W:, :]))
    h_scr[...] = _rms(x, g_ref[...]).astype(BF)
    for c in range(n_chunks):
        sl = slice(c * tf, (c + 1) * tf)
        h = h_scr[...]
        gate = _dot(h, wg_ref[:, sl])
        up = _dot(h, wu_ref[:, sl])
        a_scr[:, sl] = (_silu(gate) * up).astype(BF)
    x = x + 0.5 * _dot(a_scr[...], wd_ref[...])
    if with_final:
        x = _rms(x, fg_ref[...])
    o_ref[...] = x


def _ffn(x2, g, wg, wu, wd, proj=None, final_g=None, tm=512, tf=256):
    n, d = x2.shape
    dff = wg.shape[1]
    assert n % tm == 0 and dff % tf == 0
    row = lambda i: (i, 0)
    in_specs = [pl.BlockSpec((tm, d), row)]
    args = [x2]
    if proj is not None:
        ys5, yhg, ynsa, wo = proj
        in_specs += [pl.BlockSpec((tm, S5_W), row), pl.BlockSpec((tm, HG_W), row),
                     pl.BlockSpec((tm, NSA_W), row), _const_spec(wo.shape)]
        args += [ys5, yhg, ynsa, wo]
    in_specs += [_const_spec((1, d)), _const_spec(wg.shape), _const_spec(wu.shape), _const_spec(wd.shape)]
    args += [g.reshape(1, d), wg, wu, wd]
    if final_g is not None:
        in_specs.append(_const_spec((1, d)))
        args.append(final_g.reshape(1, d))
    kern = functools.partial(_ffn_kernel, n_chunks=dff // tf, tf=tf,
                             with_proj=proj is not None, with_final=final_g is not None)
    return pl.pallas_call(
        kern,
        grid=(n // tm,),
        in_specs=in_specs,
        out_specs=pl.BlockSpec((tm, d), row),
        out_shape=jax.ShapeDtypeStruct((n, d), F32),
        scratch_shapes=[pltpu.VMEM((tm, d), BF), pltpu.VMEM((tm, dff), BF)],
        compiler_params=_cparams("arbitrary"),
        name="ffn",
    )(*args)


N_TOK_A = S5_W + 4 * HG_W
N_T_ROWS = NSA_W + 2 * KV_W + 32


def _inproj_kernel(x_ref, g_ref, wtok_ref, wt_ref, za_ref, kcr_ref, vcr_ref, ksw_ref, qt_ref, vt_ref, gt_ref):
    h = _rms(x_ref[0], g_ref[...]).astype(BF)
    for c in range(N_TOK_A // 256):
        sl = slice(c * 256, (c + 1) * 256)
        za_ref[0, :, sl] = _dot(h, wtok_ref[:, sl])
    kcr_ref[0] = _dot(h, wtok_ref[:, N_TOK_A:N_TOK_A + KV_W])
    vcr_ref[0] = _dot(h, wtok_ref[:, N_TOK_A + KV_W:N_TOK_A + 2 * KV_W])
    ksw_ref[0] = _dot(h, wtok_ref[:, N_TOK_A + 2 * KV_W:N_TOK_A + 4 * KV_W]).astype(BF)
    qt_ref[0] = (_dot_nt(wt_ref[0:NSA_W, :], h) * (NSA_DH ** -0.5)).astype(BF)
    vt_ref[0] = _dot_nt(wt_ref[NSA_W:NSA_W + 2 * KV_W, :], h).astype(BF)
    gt_ref[0] = jax.nn.sigmoid(_dot_nt(wt_ref[NSA_W + 2 * KV_W:, :], h))


def _inproj(x3, g, w_tok, w_t, tm=512):
    b, l, d = x3.shape
    assert l % tm == 0
    tok = lambda w: pl.BlockSpec((1, tm, w), lambda bi, i: (bi, i, 0))
    tr = lambda r: pl.BlockSpec((1, r, tm), lambda bi, i: (bi, 0, i))
    return pl.pallas_call(
        _inproj_kernel,
        grid=(b, l // tm),
        in_specs=[tok(d), _const_spec((1, d)), _const_spec(w_tok.shape), _const_spec(w_t.shape)],
        out_specs=[tok(N_TOK_A), tok(KV_W), tok(KV_W), tok(2 * KV_W), tr(NSA_W), tr(2 * KV_W), tr(32)],
        out_shape=[jax.ShapeDtypeStruct((b, l, N_TOK_A), F32),
                   jax.ShapeDtypeStruct((b, l, KV_W), F32),
                   jax.ShapeDtypeStruct((b, l, KV_W), F32),
                   jax.ShapeDtypeStruct((b, l, 2 * KV_W), BF),
                   jax.ShapeDtypeStruct((b, NSA_W, l), BF),
                   jax.ShapeDtypeStruct((b, 2 * KV_W, l), BF),
                   jax.ShapeDtypeStruct((b, 32, l), F32)],
        compiler_params=_cparams("arbitrary", "arbitrary"),
        name="inproj",
    )(x3, g.reshape(1, d), w_tok, w_t)


def _s5_kernel(u_ref, bblk_ref, cblk_ref, coef_ref, d_ref, wglu_ref, y_ref, xs_scr, carry_scr, *, tc):
    @pl.when(pl.program_id(1) == 0)
    def _():
        carry_scr[...] = jnp.zeros_like(carry_scr)

    u = u_ref[0]
    xs_scr[...] = _dot(u.astype(BF), bblk_ref[...])

    def body(r, carry):
        cre, cim = carry
        row = pl.multiple_of(r * SUBLANES, SUBLANES)
        xre = xs_scr[pl.ds(row, SUBLANES), 0:S5_N]
        xim = xs_scr[pl.ds(row, SUBLANES), S5_N:2 * S5_N]
        for idx, k in enumerate((1, 2, 4)):
            are, aim = coef_ref[idx, 0], coef_ref[idx, 1]
            sre, sim = pltpu.roll(xre, k, 0), pltpu.roll(xim, k, 0)
            xre, xim = xre + (are * sre - aim * sim), xim + (are * sim + aim * sre)
        pre, pim = coef_ref[3, 0], coef_ref[3, 1]
        xre, xim = xre + (pre * cre - pim * cim), xim + (pre * cim + pim * cre)
        xs_scr[pl.ds(row, SUBLANES), 0:S5_N] = xre
        xs_scr[pl.ds(row, SUBLANES), S5_N:2 * S5_N] = xim
        return xre[SUBLANES - 1:SUBLANES], xim[SUBLANES - 1:SUBLANES]

    cre, cim = lax.fori_loop(0, tc // SUBLANES, body, (carry_scr[0:1], carry_scr[1:2]))
    carry_scr[0:1] = cre
    carry_scr[1:2] = cim

    y = _dot(xs_scr[...].astype(BF), cblk_ref[...]) + d_ref[...] * u
    y = _gelu_tanh(y)
    y_ref[0] = y * jax.nn.sigmoid(_dot(y.astype(BF), wglu_ref[...]))


def _s5_params(lam_re, lam_im, log_dt, b_re, b_im, c_re, c_im):
    lr, li = lam_re.astype(F32), lam_im.astype(F32)
    dt = jnp.exp(log_dt.astype(F32))[:, None]
    mag = jnp.exp(lr * dt)
    ab_re, ab_im = mag * jnp.cos(li * dt), mag * jnp.sin(li * dt)
    den = lr * lr + li * li
    nr, ni = ab_re - 1.0, ab_im
    g_re = (nr * lr + ni * li) / den
    g_im = (ni * lr - nr * li) / den
    br, bi = b_re.astype(F32), b_im.astype(F32)
    bb_re = g_re[..., None] * br - g_im[..., None] * bi
    bb_im = g_re[..., None] * bi + g_im[..., None] * br
    eye = jnp.eye(S5_G, dtype=F32)
    blk = lambda w: jnp.einsum('gph,gk->ghkp', w, eye).reshape(S5_W, S5_N)
    bblk = jnp.concatenate([blk(bb_re), blk(bb_im)], axis=1).astype(BF)
    cblk_f = lambda w: jnp.einsum('ghp,gk->gpkh', w, eye).reshape(S5_N, S5_W)
    cblk = jnp.concatenate([cblk_f(c_re.astype(F32)), -cblk_f(c_im.astype(F32))], axis=0).astype(BF)
    are, aim = ab_re.reshape(1, S5_N), ab_im.reshape(1, S5_N)
    pw = [(are, aim)]
    for _ in range(SUBLANES - 1):
        pr, pi = pw[-1]
        pw.append((pr * are - pi * aim, pr * aim + pi * are))
    rows = np.arange(SUBLANES)[:, None]
    coef = []
    for k in (1, 2, 4):
        m = jnp.asarray((rows >= k).astype(np.float32))
        coef.append(jnp.stack([m * pw[k - 1][0], m * pw[k - 1][1]]))
    coef.append(jnp.stack([jnp.concatenate([p[0] for p in pw], axis=0), jnp.concatenate([p[1] for p in pw], axis=0)]))
    return bblk, cblk, jnp.stack(coef)


def _s5(za, bblk, cblk, coef, d, w_glu, tc=512):
    b, l, _ = za.shape
    assert l % tc == 0
    return pl.pallas_call(
        functools.partial(_s5_kernel, tc=tc),
        grid=(b, l // tc),
        in_specs=[pl.BlockSpec((1, tc, S5_W), lambda bi, i: (bi, i, 0)),
                  _const_spec(bblk.shape), _const_spec(cblk.shape), _const_spec(coef.shape),
                  _const_spec((1, S5_W)), _const_spec(w_glu.shape)],
        out_specs=pl.BlockSpec((1, tc, S5_W), lambda bi, i: (bi, i, 0)),
        out_shape=jax.ShapeDtypeStruct((b, l, S5_W), F32),
        scratch_shapes=[pltpu.VMEM((tc, 2 * S5_N), F32), pltpu.VMEM((2, S5_N), F32)],
        compiler_params=_cparams("arbitrary", "arbitrary"),
        name="s5",
    )(za, bblk, cblk, coef, d.reshape(1, S5_W).astype(F32), w_glu.astype(BF))


def _hgrn_constants():
    c = HG_CHUNK
    t = np.arange(c)[:, None]
    u = np.arange(c)[None, :]
    mats = [(u <= t), (u > t)]
    masks = []
    for lv in range(HG_LEVELS):
        n = c >> lv
        half = n // 2
        ref = (t // n) * n + half - 1
        lower = (t % n) >= half
        mats.append(np.where(lower, (u > ref) & (u <= t), (u > t) & (u <= ref)))
        same = (t // n) == (u // n)
        masks.append(same & lower & ((u % n) < half))
    masks.append(t == u)
    gall = np.concatenate(mats, axis=0).astype(np.float32)
    mstk = np.stack([np.tile(m, (HG_HEADS, 1)) for m in masks]).astype(np.float32)
    lane_head = np.arange(HG_W)[None, :] // HG_D
    hmask = (np.repeat(np.arange(HG_HEADS), c)[:, None] == lane_head).astype(np.float32)
    bd = (np.arange(HG_W)[:, None] // HG_D == lane_head).astype(np.float32)
    return gall, mstk, hmask, bd


def _hgrn_kernel(q_ref, f_ref, i_ref, g_ref, lb_ref, gain_ref, gall_ref, mstk_ref, hmask_ref, bd_ref, bdn_ref,
                 o_ref, st_scr, *, tt):
    c = HG_CHUNK

    @pl.when(pl.program_id(1) == 0)
    def _():
        st_scr[...] = jnp.zeros_like(st_scr)

    lb = lb_ref[...]
    hmask = hmask_ref[...]

    def chunk(ci, carry):
        row = pl.multiple_of(ci * c, c)
        fl = f_ref[0, pl.ds(row, c), :]
        qf = _silu(q_ref[0, pl.ds(row, c), :])
        f = lb + (1.0 - lb) * jax.nn.sigmoid(fl)
        lf = jnp.log(jnp.maximum(f, TINY))
        kf = (1.0 - lb) * jax.nn.sigmoid(-fl)
        v = i_ref[0, pl.ds(row, c), :]
        v_bf = v.astype(BF)
        e_all = jnp.exp(_dot_exact_lhs(gall_ref[...], lf))
        e_b = e_all[0:c]
        e_suf = e_all[c:2 * c]

        att = jnp.zeros((HG_HEADS * c, c), F32)
        for lv in range(HG_LEVELS + 1):
            if lv < HG_LEVELS:
                e = e_all[(2 + lv) * c:(3 + lv) * c]
                z, w = qf * e, kf * e
            else:
                z, w = qf, kf
            zs = (jnp.concatenate([z] * HG_HEADS, axis=0) * hmask).astype(BF)
            att = att + mstk_ref[lv] * _dot_nt(zs, w.astype(BF))
        o4 = _dot(att.astype(BF), v_bf) * hmask
        o = o4[0:c]
        for h in range(1, HG_HEADS):
            o = o + o4[h * c:(h + 1) * c]

        st = st_scr[...]
        o = o + _dot_nt((qf * e_b).astype(BF), st.astype(BF))
        st_scr[...] = e_b[c - 1:c] * st + bd_ref[...] * _dot_tn(v_bf, (kf * e_suf).astype(BF))

        ms = _dot_exact_rhs(o * o, bdn_ref[...])
        o = o * lax.rsqrt(ms + EPS) * gain_ref[...]
        o_ref[0, pl.ds(row, c), :] = o * _silu(g_ref[0, pl.ds(row, c), :])
        return carry

    lax.fori_loop(0, tt // c, chunk, 0)


def _hgrn(za, lb, gain, tt=512):
    b, l, _ = za.shape
    assert l % tt == 0
    gall, mstk, hmask, bd = _hgrn_constants()
    col = lambda j: pl.BlockSpec((1, tt, HG_W), lambda bi, i: (bi, i, j))
    consts = [jnp.asarray(gall, BF), jnp.asarray(mstk, F32), jnp.asarray(hmask, F32), jnp.asarray(bd, F32),
              jnp.asarray(bd / HG_D, BF)]
    return pl.pallas_call(
        functools.partial(_hgrn_kernel, tt=tt),
        grid=(b, l // tt),
        in_specs=[col(1), col(2), col(3), col(4), _const_spec((1, HG_W)), _const_spec((1, HG_W))]
                 + [_const_spec(x.shape) for x in consts],
        out_specs=pl.BlockSpec((1, tt, HG_W), lambda bi, i: (bi, i, 0)),
        out_shape=jax.ShapeDtypeStruct((b, l, HG_W), F32),
        scratch_shapes=[pltpu.VMEM((HG_W, HG_W), F32)],
        compiler_params=_cparams("arbitrary", "arbitrary"),
        name="hgrn2",
    )(za, za, za, za, lb.reshape(1, HG_W).astype(F32), jnp.tile(gain.astype(F32), HG_HEADS).reshape(1, HG_W),
      *consts)


def _compress_kernel(k16_ref, v16_ref, kpa_ref, kpb_ref, kw1a_ref, kw1b_ref, kw2_ref,
                     vpa_ref, vpb_ref, vw1a_ref, vw1b_ref, vw2_ref, kc_ref, vct_ref, *, nc):
    def hidden(x16, pa, pb, w1a, w1b):
        first = _dot((x16 + pa).astype(BF), w1a)
        second = _dot((x16 + pb).astype(BF), w1b)
        pre = first + pltpu.roll(second, nc - 1, 0)
        rows = lax.broadcasted_iota(jnp.int32, pre.shape, 0)
        pre = jnp.where(rows < nc - 1, pre, 0.0)
        return _gelu_tanh(pre).astype(BF)

    hk = hidden(k16_ref[0], kpa_ref[...], kpb_ref[...], kw1a_ref[...], kw1b_ref[...])
    kc_ref[0] = _dot(hk, kw2_ref[...]).astype(BF)
    hv = hidden(v16_ref[0], vpa_ref[...], vpb_ref[...], vw1a_ref[...], vw1b_ref[...])
    vct_ref[0] = _dot_nt(vw2_ref[...], hv).astype(BF)


def _compress_params(pos, w1, w2, transpose_out=False):
    eye = jnp.eye(NSA_G, dtype=F32)
    w1r = w1.astype(F32).reshape(CMP_LEN, NSA_DH, NSA_DH)
    wexp = jnp.einsum('jde,gh->jgdhe', w1r, eye).reshape(CMP_LEN, KV_W, KV_W)
    half = CMP_LEN // 2
    w1a = wexp[:half].reshape(half * KV_W, KV_W).astype(BF)
    w1b = wexp[half:].reshape(half * KV_W, KV_W).astype(BF)
    pt = jnp.broadcast_to(pos.astype(F32)[:, None, :], (CMP_LEN, NSA_G, NSA_DH))
    pa = pt[:half].reshape(1, half * KV_W)
    pb = pt[half:].reshape(1, half * KV_W)
    w2bd = jnp.einsum('de,gh->gdhe', w2.astype(F32), eye).reshape(KV_W, KV_W).astype(BF)
    return pa, pb, w1a, w1b, (w2bd.T if transpose_out else w2bd)


def _compress(kcr, vcr, kparams, vparams):
    b, l, _ = kcr.shape
    nc = l // CMP_STRIDE
    w = CMP_STRIDE * KV_W
    k16 = kcr.reshape(b, nc, w)
    v16 = vcr.reshape(b, nc, w)
    per_b = lambda r, cc: pl.BlockSpec((1, r, cc), lambda bi: (bi, 0, 0))
    params = list(kparams) + list(vparams)
    return pl.pallas_call(
        functools.partial(_compress_kernel, nc=nc),
        grid=(b,),
        in_specs=[per_b(nc, w), per_b(nc, w)] + [_const_spec(p.shape) for p in params],
        out_specs=[per_b(nc, KV_W), per_b(KV_W, nc)],
        out_shape=[jax.ShapeDtypeStruct((b, nc, KV_W), BF), jax.ShapeDtypeStruct((b, KV_W, nc), BF)],
        compiler_params=_cparams("arbitrary"),
        name="compress",
    )(k16, v16, *params)


def _t5_bucket(dist):
    n = jnp.maximum(dist, 0)
    max_exact = REL_BUCKETS // 2
    nf = jnp.maximum(n, max_exact).astype(jnp.float32)
    large = max_exact + (jnp.log(nf / max_exact) / math.log(REL_MAX_DIST / max_exact)
                         * (REL_BUCKETS - max_exact)).astype(jnp.int32)
    large = jnp.minimum(large, REL_BUCKETS - 1)
    return jnp.where(n < max_exact, n, large)


def _bias_tables(rel_bias, l):
    tab = rel_bias.astype(F32)
    by_dist = tab[_t5_bucket(jnp.arange(l))]
    nc = l // CMP_STRIDE

    def tiles(dist, valid):
        vals = jnp.where(valid[..., None], by_dist[jnp.clip(dist, 0, l - 1)], NEG_INF)
        nt, nr, nq, _ = vals.shape
        vals = vals.transpose(3, 0, 1, 2).reshape(NSA_G, NSA_R, nt, nr, nq)
        return vals.transpose(0, 2, 3, 1, 4).reshape(NSA_G, nt, nr, NSA_R * nq)

    tq = np.arange(Q_TILE)[None, None, :]
    key = np.arange(K_TILE)[None, :, None]
    d_slc = np.arange(N_TOEPLITZ)[:, None, None] * K_TILE + tq - key
    tz = tiles(d_slc, d_slc >= 0)
    d_win = (N_WIN_TILES - 1 - np.arange(N_WIN_TILES))[:, None, None] * K_TILE + tq - key
    wz = tiles(d_win, (d_win >= 0) & (d_win < WIN))
    u = np.arange(2 * nc - SUBLANES)[None, :, None]
    d_cmp = tq - CMP_STRIDE * (u - (nc - SUBLANES)) - (CMP_LEN - 1)
    mc = tiles(d_cmp, d_cmp >= 0)[:, 0]
    return tz, wz, mc


def _nsa_kernel(qt_ref, gt_ref, kc_ref, vct_ref, ksl_ref, kwn_ref, vslt_ref, vwnt_ref, mc_ref, tz_ref, wz_ref,
                o_ref, imp_scr, selb_scr, m_scr, l_scr, acc_scr, y_scr, *, nc, nsb):
    qi = pl.program_id(1)
    q0 = qi * Q_TILE
    n_sel = min(N_SEL, nsb)
    cmp_per_q = Q_TILE // CMP_STRIDE
    cmp_per_slc = SLC_LEN // CMP_STRIDE

    def lanes4(rows):
        return jnp.concatenate(rows, axis=1)

    def attn_tile(k_tile, vt_tile, qt, bias):
        s = _dot(k_tile, qt) + bias
        m_prev = m_scr[...]
        m_new = jnp.maximum(m_prev, jnp.max(s, axis=0, keepdims=True))
        alpha = jnp.exp(m_prev - m_new)
        p = jnp.exp(s - m_new)
        l_scr[...] = alpha * l_scr[...] + jnp.sum(p, axis=0, keepdims=True)
        acc_scr[...] = alpha * acc_scr[...] + _dot(vt_tile, p.astype(BF))
        m_scr[...] = m_new

    def reset():
        m_scr[...] = jnp.full_like(m_scr, NEG_INF)
        l_scr[...] = jnp.zeros_like(l_scr)
        acc_scr[...] = jnp.zeros_like(acc_scr)

    def result():
        return acc_scr[...] * (1.0 / l_scr[...])

    for g in range(NSA_G):
        zeros = jnp.zeros((NSA_DH, Q_TILE), BF)
        parts = []
        for r in range(NSA_R):
            h = g * NSA_R + r
            qh = qt_ref[0, h * NSA_DH:(h + 1) * NSA_DH, :]
            parts.append(jnp.concatenate([qh, zeros] if g == 0 else [zeros, qh], axis=0))
        qt = lanes4(parts)

        bias_c = mc_ref[g, pl.ds(pl.multiple_of((nc - SUBLANES) - cmp_per_q * qi, SUBLANES), nc), :]
        s = _dot(kc_ref[0], qt) + bias_c
        m = jnp.max(s, axis=0, keepdims=True)
        e = jnp.exp(s - m)
        lsum = jnp.sum(e, axis=0, keepdims=True)
        tq4 = q0 + lax.broadcasted_iota(jnp.int32, (1, GR_LANES), 1) % Q_TILE
        inv = jnp.where(tq4 >= CMP_LEN - 1, 1.0 / jnp.maximum(lsum, TINY), 0.0)
        p = e * inv
        o_cmp = _dot(vct_ref[0], p.astype(BF))
        imp = p[:, 0:Q_TILE]
        for r in range(1, NSA_R):
            imp = imp + p[:, r * Q_TILE:(r + 1) * Q_TILE]

        imp_scr[0:SUBLANES, :] = jnp.zeros((SUBLANES, Q_TILE), F32)
        imp_scr[SUBLANES:SUBLANES + nc, :] = imp
        a = [imp_scr[pl.ds(SUBLANES - 1 + k, nsb, stride=cmp_per_slc), :] for k in range(cmp_per_slc + 1)]
        p_slc = a[0] + a[1]
        for k in range(1, cmp_per_slc):
            p_slc = p_slc + a[k] + a[k + 1]

        jidx = lax.broadcasted_iota(jnp.int32, (nsb, Q_TILE), 0)
        tq = q0 + lax.broadcasted_iota(jnp.int32, (nsb, Q_TILE), 1)
        cur = tq // SLC_LEN
        forced = (jidx == 0) | (jidx == cur) | (jidx == cur - 1)
        score = jnp.where(jidx * SLC_LEN <= tq, p_slc + jnp.where(forced, FORCE_BONUS, 0.0), NEG_INF)
        selb = jnp.full((nsb, Q_TILE), NEG_INF, F32)
        for _ in range(n_sel):
            mx = jnp.max(score, axis=0, keepdims=True)
            first = jnp.min(jnp.where(score == mx, jidx, nsb), axis=0, keepdims=True)
            hit = jidx == first
            selb = jnp.where(hit, 0.0, selb)
            score = jnp.where(hit, -jnp.inf, score)
        selb_scr[...] = selb

        reset()

        def slc_body(kt, carry):
            k0 = pl.multiple_of(kt * K_TILE, K_TILE)
            blk = kt * (K_TILE // SLC_LEN)
            sb = []
            for j in range(K_TILE // SLC_LEN):
                row = selb_scr[pl.ds(blk + j, 1), :]
                sb.append(jnp.broadcast_to(lanes4([row] * NSA_R), (SLC_LEN, GR_LANES)))
            bias = tz_ref[g, jnp.minimum(qi - kt, N_TOEPLITZ - 1)] + jnp.concatenate(sb, axis=0)
            attn_tile(ksl_ref[0, pl.ds(k0, K_TILE), :], vslt_ref[0, :, pl.ds(k0, K_TILE)], qt, bias)
            return carry

        lax.fori_loop(0, qi + 1, slc_body, 0)
        o_slc = result()

        reset()

        def win_body(kt, carry):
            k0 = pl.multiple_of(kt * K_TILE, K_TILE)
            bias = wz_ref[g, kt - (qi - (N_WIN_TILES - 1))]
            attn_tile(kwn_ref[0, pl.ds(k0, K_TILE), :], vwnt_ref[0, :, pl.ds(k0, K_TILE)], qt, bias)
            return carry

        lax.fori_loop(jnp.maximum(qi - (N_WIN_TILES - 1), 0), qi + 1, win_body, 0)
        o_win = result()

        def gate(branch):
            return lanes4([gt_ref[0, pl.ds((g * NSA_R + r) * 3 + branch, 1), :] for r in range(NSA_R)])

        yt = gate(0) * o_cmp + gate(1) * o_slc + gate(2) * o_win
        for r in range(NSA_R):
            h = g * NSA_R + r
            y_scr[h * NSA_DH:(h + 1) * NSA_DH, :] = yt[g * NSA_DH:(g + 1) * NSA_DH, r * Q_TILE:(r + 1) * Q_TILE]

    o_ref[0] = y_scr[...].T


def _nsa(qt, gt, kc, vct, ksw, vt, tz, wz, mc):
    b, _, l = qt.shape
    nc = l // CMP_STRIDE
    nsb = l // SLC_LEN
    per_b = lambda shape, idx: pl.BlockSpec((1,) + shape, lambda bi, i: (bi,) + idx)
    return pl.pallas_call(
        functools.partial(_nsa_kernel, nc=nc, nsb=nsb),
        grid=(b, l // Q_TILE),
        in_specs=[pl.BlockSpec((1, NSA_W, Q_TILE), lambda bi, i: (bi, 0, i)),
                  pl.BlockSpec((1, 32, Q_TILE), lambda bi, i: (bi, 0, i)),
                  per_b((nc, KV_W), (0, 0)), per_b((KV_W, nc), (0, 0)),
                  per_b((l, KV_W), (0, 0)), per_b((l, KV_W), (0, 1)),
                  per_b((KV_W, l), (0, 0)), per_b((KV_W, l), (1, 0)),
                  _const_spec(mc.shape), _const_spec(tz.shape), _const_spec(wz.shape)],
        out_specs=pl.BlockSpec((1, Q_TILE, NSA_W), lambda bi, i: (bi, i, 0)),
        out_shape=jax.ShapeDtypeStruct((b, l, NSA_W), F32),
        scratch_shapes=[pltpu.VMEM((nc + SUBLANES, Q_TILE), F32), pltpu.VMEM((nsb, Q_TILE), F32),
                        pltpu.VMEM((1, GR_LANES), F32), pltpu.VMEM((1, GR_LANES), F32),
                        pltpu.VMEM((KV_W, GR_LANES), F32), pltpu.VMEM((NSA_W, Q_TILE), F32)],
        compiler_params=_cparams("arbitrary", "arbitrary"),
        name="nsa",
    )(qt, gt, kc, vct, ksw, ksw, vt, vt, mc, tz, wz)


def _split_w_in(w_in):
    o_q = N_TOK_A
    o_kv = o_q + NSA_W
    o_gate = o_kv + 6 * KV_W
    kv = lambda j: w_in[:, o_kv + j * KV_W:o_kv + (j + 1) * KV_W]
    w_tok = jnp.concatenate([w_in[:, :N_TOK_A], kv(0), kv(1), kv(2), kv(4)], axis=1).astype(BF)
    gates = w_in[:, o_gate:]
    pad = jnp.zeros((w_in.shape[0], 32 - gates.shape[1]), w_in.dtype)
    w_t = jnp.concatenate([w_in[:, o_q:o_kv], kv(3), kv(5), gates, pad], axis=1).T.astype(BF)
    return w_tok, w_t


def kernel(x, ffn1_norm, ffn1_w_gate, ffn1_w_up, ffn1_w_down, mix_norm, w_in, w_out, s5_lambda_re, s5_lambda_im, s5_log_dt, s5_b_re, s5_b_im, s5_c_re, s5_c_im, s5_d, s5_w_glu, hgrn_lb_logits, hgrn_norm, nsa_cmp_pos_k, nsa_cmp_w1_k, nsa_cmp_w2_k, nsa_cmp_pos_v, nsa_cmp_w1_v, nsa_cmp_w2_v, rel_bias, ffn2_norm, ffn2_w_gate, ffn2_w_up, ffn2_w_down, final_norm):
    b, l, d = x.shape
    depth = w_in.shape[0]
    gam = jax.nn.softmax(hgrn_lb_logits.astype(F32), axis=0)
    lower_bounds = jnp.cumsum(gam, axis=0) - gam[0:1]
    tz, wz, mc = _bias_tables(rel_bias, l)
    bf = lambda w: w.astype(BF)

    x2 = x.reshape(b * l, d)
    for i in range(depth):
        x2 = _ffn(x2, ffn1_norm[i], bf(ffn1_w_gate[i]), bf(ffn1_w_up[i]), bf(ffn1_w_down[i]))
        w_tok, w_t = _split_w_in(w_in[i])
        za, kcr, vcr, ksw, qt, vt, gt = _inproj(x2.reshape(b, l, d), mix_norm[i], w_tok, w_t)
        bblk, cblk, coef = _s5_params(s5_lambda_re[i], s5_lambda_im[i], s5_log_dt[i], s5_b_re[i], s5_b_im[i],
                                      s5_c_re[i], s5_c_im[i])
        y_s5 = _s5(za, bblk, cblk, coef, s5_d[i], s5_w_glu[i])
        y_hg = _hgrn(za, lower_bounds[i], hgrn_norm[i])
        kc, vct = _compress(kcr, vcr,
                            _compress_params(nsa_cmp_pos_k[i], nsa_cmp_w1_k[i], nsa_cmp_w2_k[i]),
                            _compress_params(nsa_cmp_pos_v[i], nsa_cmp_w1_v[i], nsa_cmp_w2_v[i], transpose_out=True))
        y_nsa = _nsa(qt, gt, kc, vct, ksw, vt, tz, wz, mc)
        proj = (y_s5.reshape(b * l, S5_W), y_hg.reshape(b * l, HG_W), y_nsa.reshape(b * l, NSA_W), bf(w_out[i]))
        x2 = _ffn(x2, ffn2_norm[i], bf(ffn2_w_gate[i]), bf(ffn2_w_up[i]), bf(ffn2_w_down[i]), proj=proj,
                  final_g=final_norm if i + 1 == depth else None)
    return x2.reshape(b, l, d)
```

```python
import functools
import math

import numpy as np
import jax
import jax.numpy as jnp
from jax import lax
from jax.experimental import pallas as pl
from jax.experimental.pallas import tpu as pltpu

BF = jnp.bfloat16
F32 = jnp.float32

EPS = 1e-6
NEG_INF = -1e30
TINY = 1e-30
FORCE_BONUS = 1e4

S5_W = 256
S5_G = 16
S5_GROUP = 16
S5_P = 64
S5_N = S5_G * S5_P
HG_W = 256
HG_HEADS = 4
HG_D = 64
NSA_W = 512
NSA_DH = 64
NSA_H = 8
NSA_G = 2
NSA_R = 4
KV_W = NSA_G * NSA_DH
CMP_LEN = 32
CMP_STRIDE = 16
SLC_LEN = 64
N_SEL = 16
WIN = 512
REL_BUCKETS = 32
REL_MAX_DIST = 1024

Q_TILE = 128
K_TILE = 128
GR_LANES = NSA_R * Q_TILE
N_TOEPLITZ = REL_MAX_DIST // K_TILE + 2
N_WIN_TILES = WIN // K_TILE + 1
HG_CHUNK = 64
HG_LEVELS = 6
SUBLANES = 8
VMEM_LIMIT_BYTES = 56 * 1024 * 1024


def _cparams(*sem):
    return pltpu.CompilerParams(dimension_semantics=sem, vmem_limit_bytes=VMEM_LIMIT_BYTES)


def _const_spec(shape):
    nd = len(shape)
    return pl.BlockSpec(shape, lambda *_: (0,) * nd)


def _rms(x, g_row):
    ms = jnp.mean(x * x, axis=-1, keepdims=True)
    return x * lax.rsqrt(ms + EPS) * g_row


def _silu(x):
    return x * jax.nn.sigmoid(x)


def _gelu_tanh(x):
    return 0.5 * x * (1.0 + jnp.tanh(math.sqrt(2.0 / math.pi) * (x + 0.044715 * (x * x * x))))


def _dot(a, b):
    return jnp.dot(a, b, preferred_element_type=F32)


def _dot_nt(a, b):
    return lax.dot_general(a, b, (((1,), (1,)), ((), ())), preferred_element_type=F32)


def _dot_tn(a, b):
    return lax.dot_general(a, b, (((0,), (0,)), ((), ())), preferred_element_type=F32)


def _dot_exact_lhs(c_bf, x):
    hi = x.astype(BF)
    r1 = x - hi.astype(F32)
    mid = r1.astype(BF)
    lo = (r1 - mid.astype(F32)).astype(BF)
    return _dot(c_bf, hi) + _dot(c_bf, mid) + _dot(c_bf, lo)


def _dot_exact_rhs(x, c_bf):
    hi = x.astype(BF)
    r1 = x - hi.astype(F32)
    mid = r1.astype(BF)
    lo = (r1 - mid.astype(F32)).astype(BF)
    return _dot(hi, c_bf) + _dot(mid, c_bf) + _dot(lo, c_bf)


def _ffn_kernel(*refs, n_chunks, tf, with_proj, with_final):
    it = iter(refs)
    x_ref = next(it)
    if with_proj:
        ys5_ref, yhg_ref, ynsa_ref, wo_ref = next(it), next(it), next(it), next(it)
    g_ref, wg_ref, wu_ref, wd_ref = next(it), next(it), next(it), next(it)
    if with_final:
        fg_ref = next(it)
    o_ref, h_scr, a_scr = next(it), next(it), next(it)

    x = x_ref[...]
    if with_proj:
        x = (x + _dot(ys5_ref[...].astype(BF), wo_ref[0:S5_W, :])
             + _dot(yhg_ref[...].astype(BF), wo_ref[S5_W:S5_W + HG_W, :])
             + _dot(ynsa_ref[...].astype(BF), wo_ref[S5_W + HG_W:, :]))
    h_scr[...] = _rms(x, g_ref[...]).astype(BF)
    for c in range(n_chunks):
        sl = slice(c * tf, (c + 1) * tf)
        h = h_scr[...]
        gate = _dot(h, wg_ref[:, sl])
        up = _dot(h, wu_ref[:, sl])
        a_scr[:, sl] = (_silu(gate) * up).astype(BF)
    x = x + 0.5 * _dot(a_scr[...], wd_ref[...])
    if with_final:
        x = _rms(x, fg_ref[...])
    o_ref[...] = x


def _ffn(x2, g, wg, wu, wd, proj=None, final_g=None, tm=512, tf=256):
    n, d = x2.shape
    dff = wg.shape[1]
    assert n % tm == 0 and dff % tf == 0
    row = lambda i: (i, 0)
    in_specs = [pl.BlockSpec((tm, d), row)]
    args = [x2]
    if proj is not None:
        ys5, yhg, ynsa, wo = proj
        in_specs += [pl.BlockSpec((tm, S5_W), row), pl.BlockSpec((tm, HG_W), row),
                     pl.BlockSpec((tm, NSA_W), row), _const_spec(wo.shape)]
        args += [ys5, yhg, ynsa, wo]
    in_specs += [_const_spec((1, d)), _const_spec(wg.shape), _const_spec(wu.shape), _const_spec(wd.shape)]
    args += [g.reshape(1, d), wg, wu, wd]
    if final_g is not None:
        in_specs.append(_const_spec((1, d)))
        args.append(final_g.reshape(1, d))
    kern = functools.partial(_ffn_kernel, n_chunks=dff // tf, tf=tf,
                             with_proj=proj is not None, with_final=final_g is not None)
    return pl.pallas_call(
        kern,
        grid=(n // tm,),
        in_specs=in_specs,
        out_specs=pl.BlockSpec((tm, d), row),
        out_shape=jax.ShapeDtypeStruct((n, d), F32),
        scratch_shapes=[pltpu.VMEM((tm, d), BF), pltpu.VMEM((tm, dff), BF)],
        compiler_params=_cparams("arbitrary"),
        name="ffn",
    )(*args)


N_TOK_A = S5_W + 4 * HG_W
N_T_ROWS = NSA_W + 2 * KV_W + 32


def _inproj_kernel(x_ref, g_ref, wtok_ref, wt_ref, za_ref, kcr_ref, vcr_ref, ksw_ref, qt_ref, vt_ref, gt_ref):
    h = _rms(x_ref[0], g_ref[...]).astype(BF)
    for c in range(N_TOK_A // 256):
        sl = slice(c * 256, (c + 1) * 256)
        za_ref[0, :, sl] = _dot(h, wtok_ref[:, sl])
    kcr_ref[0] = _dot(h, wtok_ref[:, N_TOK_A:N_TOK_A + KV_W])
    vcr_ref[0] = _dot(h, wtok_ref[:, N_TOK_A + KV_W:N_TOK_A + 2 * KV_W])
    ksw_ref[0] = _dot(h, wtok_ref[:, N_TOK_A + 2 * KV_W:N_TOK_A + 4 * KV_W]).astype(BF)
    qt_ref[0] = (_dot_nt(wt_ref[0:NSA_W, :], h) * (NSA_DH ** -0.5)).astype(BF)
    vt_ref[0] = _dot_nt(wt_ref[NSA_W:NSA_W + 2 * KV_W, :], h).astype(BF)
    gt_ref[0] = jax.nn.sigmoid(_dot_nt(wt_ref[NSA_W + 2 * KV_W:, :], h))


def _inproj(x3, g, w_tok, w_t, tm=512):
    b, l, d = x3.shape
    assert l % tm == 0
    tok = lambda w: pl.BlockSpec((1, tm, w), lambda bi, i: (bi, i, 0))
    tr = lambda r: pl.BlockSpec((1, r, tm), lambda bi, i: (bi, 0, i))
    return pl.pallas_call(
        _inproj_kernel,
        grid=(b, l // tm),
        in_specs=[tok(d), _const_spec((1, d)), _const_spec(w_tok.shape), _const_spec(w_t.shape)],
        out_specs=[tok(N_TOK_A), tok(KV_W), tok(KV_W), tok(2 * KV_W), tr(NSA_W), tr(2 * KV_W), tr(32)],
        out_shape=[jax.ShapeDtypeStruct((b, l, N_TOK_A), F32),
                   jax.ShapeDtypeStruct((b, l, KV_W), F32),
                   jax.ShapeDtypeStruct((b, l, KV_W), F32),
                   jax.ShapeDtypeStruct((b, l, 2 * KV_W), BF),
                   jax.ShapeDtypeStruct((b, NSA_W, l), BF),
                   jax.ShapeDtypeStruct((b, 2 * KV_W, l), BF),
                   jax.ShapeDtypeStruct((b, 32, l), F32)],
        compiler_params=_cparams("arbitrary", "arbitrary"),
        name="inproj",
    )(x3, g.reshape(1, d), w_tok, w_t)


def _s5_kernel(u_ref, bblk_ref, cblk_ref, coef_ref, d_ref, wglu_ref, y_ref, xs_scr, carry_scr, *, tc):
    @pl.when(pl.program_id(1) == 0)
    def _():
        carry_scr[...] = jnp.zeros_like(carry_scr)

    u = u_ref[0]
    xs_scr[...] = _dot(u.astype(BF), bblk_ref[...])

    def body(r, carry):
        cre, cim = carry
        row = pl.multiple_of(r * SUBLANES, SUBLANES)
        xre = xs_scr[pl.ds(row, SUBLANES), 0:S5_N]
        xim = xs_scr[pl.ds(row, SUBLANES), S5_N:2 * S5_N]
        for idx, k in enumerate((1, 2, 4)):
            are, aim = coef_ref[idx, 0], coef_ref[idx, 1]
            sre, sim = pltpu.roll(xre, k, 0), pltpu.roll(xim, k, 0)
            xre, xim = xre + (are * sre - aim * sim), xim + (are * sim + aim * sre)
        pre, pim = coef_ref[3, 0], coef_ref[3, 1]
        xre, xim = xre + (pre * cre - pim * cim), xim + (pre * cim + pim * cre)
        xs_scr[pl.ds(row, SUBLANES), 0:S5_N] = xre
        xs_scr[pl.ds(row, SUBLANES), S5_N:2 * S5_N] = xim
        return xre[SUBLANES - 1:SUBLANES], xim[SUBLANES - 1:SUBLANES]

    cre, cim = lax.fori_loop(0, tc // SUBLANES, body, (carry_scr[0:1], carry_scr[1:2]))
    carry_scr[0:1] = cre
    carry_scr[1:2] = cim

    y = _dot(xs_scr[...].astype(BF), cblk_ref[...]) + d_ref[...] * u
    y = _gelu_tanh(y)
    y_ref[0] = y * jax.nn.sigmoid(_dot(y.astype(BF), wglu_ref[...]))


def _s5_params(lam_re, lam_im, log_dt, b_re, b_im, c_re, c_im):
    lr, li = lam_re.astype(F32), lam_im.astype(F32)
    dt = jnp.exp(log_dt.astype(F32))[:, None]
    mag = jnp.exp(lr * dt)
    ab_re, ab_im = mag * jnp.cos(li * dt), mag * jnp.sin(li * dt)
    den = lr * lr + li * li
    nr, ni = ab_re - 1.0, ab_im
    g_re = (nr * lr + ni * li) / den
    g_im = (ni * lr - nr * li) / den
    br, bi = b_re.astype(F32), b_im.astype(F32)
    bb_re = g_re[..., None] * br - g_im[..., None] * bi
    bb_im = g_re[..., None] * bi + g_im[..., None] * br
    eye = jnp.eye(S5_G, dtype=F32)
    blk = lambda w: jnp.einsum('gph,gk->ghkp', w, eye).reshape(S5_W, S5_N)
    bblk = jnp.concatenate([blk(bb_re), blk(bb_im)], axis=1).astype(BF)
    cblk_f = lambda w: jnp.einsum('ghp,gk->gpkh', w, eye).reshape(S5_N, S5_W)
    cblk = jnp.concatenate([cblk_f(c_re.astype(F32)), -cblk_f(c_im.astype(F32))], axis=0).astype(BF)
    are, aim = ab_re.reshape(1, S5_N), ab_im.reshape(1, S5_N)
    pw = [(are, aim)]
    for _ in range(SUBLANES - 1):
        pr, pi = pw[-1]
        pw.append((pr * are - pi * aim, pr * aim + pi * are))
    rows = np.arange(SUBLANES)[:, None]
    coef = []
    for k in (1, 2, 4):
        m = jnp.asarray((rows >= k).astype(np.float32))
        coef.append(jnp.stack([m * pw[k - 1][0], m * pw[k - 1][1]]))
    coef.append(jnp.stack([jnp.concatenate([p[0] for p in pw], axis=0), jnp.concatenate([p[1] for p in pw], axis=0)]))
    return bblk, cblk, jnp.stack(coef)


def _s5(za, bblk, cblk, coef, d, w_glu, tc=512):
    b, l, _ = za.shape
    assert l % tc == 0
    return pl.pallas_call(
        functools.partial(_s5_kernel, tc=tc),
        grid=(b, l // tc),
        in_specs=[pl.BlockSpec((1, tc, S5_W), lambda bi, i: (bi, i, 0)),
                  _const_spec(bblk.shape), _const_spec(cblk.shape), _const_spec(coef.shape),
                  _const_spec((1, S5_W)), _const_spec(w_glu.shape)],
        out_specs=pl.BlockSpec((1, tc, S5_W), lambda bi, i: (bi, i, 0)),
        out_shape=jax.ShapeDtypeStruct((b, l, S5_W), F32),
        scratch_shapes=[pltpu.VMEM((tc, 2 * S5_N), F32), pltpu.VMEM((2, S5_N), F32)],
        compiler_params=_cparams("arbitrary", "arbitrary"),
        name="s5",
    )(za, bblk, cblk, coef, d.reshape(1, S5_W).astype(F32), w_glu.astype(BF))


def _hgrn_constants():
    c = HG_CHUNK
    t = np.arange(c)[:, None]
    u = np.arange(c)[None, :]
    mats = [(u <= t), (u > t)]
    masks = []
    for lv in range(HG_LEVELS):
        n = c >> lv
        half = n // 2
        ref = (t // n) * n + half - 1
        lower = (t % n) >= half
        mats.append(np.where(lower, (u > ref) & (u <= t), (u > t) & (u <= ref)))
        same = (t // n) == (u // n)
        masks.append(same & lower & ((u % n) < half))
    masks.append(t == u)
    gall = np.concatenate(mats, axis=0).astype(np.float32)
    mstk = np.stack([np.tile(m, (HG_HEADS, 1)) for m in masks]).astype(np.float32)
    lane_head = np.arange(HG_W)[None, :] // HG_D
    hmask = (np.repeat(np.arange(HG_HEADS), c)[:, None] == lane_head).astype(np.float32)
    bd = (np.arange(HG_W)[:, None] // HG_D == lane_head).astype(np.float32)
    return gall, mstk, hmask, bd


def _hgrn_kernel(q_ref, f_ref, i_ref, g_ref, lb_ref, gain_ref, gall_ref, mstk_ref, hmask_ref, bd_ref, bdn_ref,
                 o_ref, st_scr, *, tt):
    c = HG_CHUNK

    @pl.when(pl.program_id(1) == 0)
    def _():
        st_scr[...] = jnp.zeros_like(st_scr)

    lb = lb_ref[...]
    hmask = hmask_ref[...]

    def chunk(ci, carry):
        row = pl.multiple_of(ci * c, c)
        fl = f_ref[0, pl.ds(row, c), :]
        qf = _silu(q_ref[0, pl.ds(row, c), :])
        f = lb + (1.0 - lb) * jax.nn.sigmoid(fl)
        lf = jnp.log(jnp.maximum(f, TINY))
        kf = (1.0 - lb) * jax.nn.sigmoid(-fl)
        v = i_ref[0, pl.ds(row, c), :]
        v_bf = v.astype(BF)
        e_all = jnp.exp(_dot_exact_lhs(gall_ref[...], lf))
        e_b = e_all[0:c]
        e_suf = e_all[c:2 * c]

        att = jnp.zeros((HG_HEADS * c, c), F32)
        for lv in range(HG_LEVELS + 1):
            if lv < HG_LEVELS:
                e = e_all[(2 + lv) * c:(3 + lv) * c]
                z, w = qf * e, kf * e
            else:
                z, w = qf, kf
            zs = (jnp.concatenate([z] * HG_HEADS, axis=0) * hmask).astype(BF)
            att = att + mstk_ref[lv] * _dot_nt(zs, w.astype(BF))
        o4 = _dot(att.astype(BF), v_bf) * hmask
        o = o4[0:c]
        for h in range(1, HG_HEADS):
            o = o + o4[h * c:(h + 1) * c]

        st = st_scr[...]
        o = o + _dot_nt((qf * e_b).astype(BF), st.astype(BF))
        st_scr[...] = e_b[c - 1:c] * st + bd_ref[...] * _dot_tn(v_bf, (kf * e_suf).astype(BF))

        ms = _dot_exact_rhs(o * o, bdn_ref[...])
        o = o * lax.rsqrt(ms + EPS) * gain_ref[...]
        o_ref[0, pl.ds(row, c), :] = o * _silu(g_ref[0, pl.ds(row, c), :])
        return carry

    lax.fori_loop(0, tt // c, chunk, 0)


def _hgrn(za, lb, gain, tt=512):
    b, l, _ = za.shape
    assert l % tt == 0
    gall, mstk, hmask, bd = _hgrn_constants()
    col = lambda j: pl.BlockSpec((1, tt, HG_W), lambda bi, i: (bi, i, j))
    consts = [jnp.asarray(gall, BF), jnp.asarray(mstk, F32), jnp.asarray(hmask, F32), jnp.asarray(bd, F32),
              jnp.asarray(bd / HG_D, BF)]
    return pl.pallas_call(
        functools.partial(_hgrn_kernel, tt=tt),
        grid=(b, l // tt),
        in_specs=[col(1), col(2), col(3), col(4), _const_spec((1, HG_W)), _const_spec((1, HG_W))]
                 + [_const_spec(x.shape) for x in consts],
        out_specs=pl.BlockSpec((1, tt, HG_W), lambda bi, i: (bi, i, 0)),
        out_shape=jax.ShapeDtypeStruct((b, l, HG_W), F32),
        scratch_shapes=[pltpu.VMEM((HG_W, HG_W), F32)],
        compiler_params=_cparams("arbitrary", "arbitrary"),
        name="hgrn2",
    )(za, za, za, za, lb.reshape(1, HG_W).astype(F32), jnp.tile(gain.astype(F32), HG_HEADS).reshape(1, HG_W),
      *consts)


def _compress_kernel(k16_ref, v16_ref, kpa_ref, kpb_ref, kw1a_ref, kw1b_ref, kw2_ref,
                     vpa_ref, vpb_ref, vw1a_ref, vw1b_ref, vw2_ref, kc_ref, vct_ref, *, nc):
    def hidden(x16, pa, pb, w1a, w1b):
        first = _dot((x16 + pa).astype(BF), w1a)
        second = _dot((x16 + pb).astype(BF), w1b)
        pre = first + pltpu.roll(second, nc - 1, 0)
        rows = lax.broadcasted_iota(jnp.int32, pre.shape, 0)
        pre = jnp.where(rows < nc - 1, pre, 0.0)
        return _gelu_tanh(pre).astype(BF)

    hk = hidden(k16_ref[0], kpa_ref[...], kpb_ref[...], kw1a_ref[...], kw1b_ref[...])
    kc_ref[0] = _dot(hk, kw2_ref[...]).astype(BF)
    hv = hidden(v16_ref[0], vpa_ref[...], vpb_ref[...], vw1a_ref[...], vw1b_ref[...])
    vct_ref[0] = _dot_nt(vw2_ref[...], hv).astype(BF)


def _compress_params(pos, w1, w2, transpose_out=False):
    eye = jnp.eye(NSA_G, dtype=F32)
    w1r = w1.astype(F32).reshape(CMP_LEN, NSA_DH, NSA_DH)
    wexp = jnp.einsum('jde,gh->jgdhe', w1r, eye).reshape(CMP_LEN, KV_W, KV_W)
    half = CMP_LEN // 2
    w1a = wexp[:half].reshape(half * KV_W, KV_W).astype(BF)
    w1b = wexp[half:].reshape(half * KV_W, KV_W).astype(BF)
    pt = jnp.broadcast_to(pos.astype(F32)[:, None, :], (CMP_LEN, NSA_G, NSA_DH))
    pa = pt[:half].reshape(1, half * KV_W)
    pb = pt[half:].reshape(1, half * KV_W)
    w2bd = jnp.einsum('de,gh->gdhe', w2.astype(F32), eye).reshape(KV_W, KV_W).astype(BF)
    return pa, pb, w1a, w1b, (w2bd.T if transpose_out else w2bd)


def _compress(kcr, vcr, kparams, vparams):
    b, l, _ = kcr.shape
    nc = l // CMP_STRIDE
    w = CMP_STRIDE * KV_W
    k16 = kcr.reshape(b, nc, w)
    v16 = vcr.reshape(b, nc, w)
    per_b = lambda r, cc: pl.BlockSpec((1, r, cc), lambda bi: (bi, 0, 0))
    params = list(kparams) + list(vparams)
    return pl.pallas_call(
        functools.partial(_compress_kernel, nc=nc),
        grid=(b,),
        in_specs=[per_b(nc, w), per_b(nc, w)] + [_const_spec(p.shape) for p in params],
        out_specs=[per_b(nc, KV_W), per_b(KV_W, nc)],
        out_shape=[jax.ShapeDtypeStruct((b, nc, KV_W), BF), jax.ShapeDtypeStruct((b, KV_W, nc), BF)],
        compiler_params=_cparams("arbitrary"),
        name="compress",
    )(k16, v16, *params)


def _t5_bucket(dist):
    n = jnp.maximum(dist, 0)
    max_exact = REL_BUCKETS // 2
    nf = jnp.maximum(n, max_exact).astype(jnp.float32)
    large = max_exact + (jnp.log(nf / max_exact) / math.log(REL_MAX_DIST / max_exact)
                         * (REL_BUCKETS - max_exact)).astype(jnp.int32)
    large = jnp.minimum(large, REL_BUCKETS - 1)
    return jnp.where(n < max_exact, n, large)


def _bias_tables(rel_bias, l):
    tab = rel_bias.astype(F32)
    by_dist = tab[_t5_bucket(jnp.arange(l))]
    nc = l // CMP_STRIDE
    front = K_TILE
    fpad = jnp.pad(by_dist.T, ((0, 0), (front, 2 * K_TILE)), mode='edge')

    def rows(dist0, valid):
        starts = np.clip(dist0 + front, 0, fpad.shape[1] - Q_TILE)
        w = jax.vmap(lambda s: lax.dynamic_slice(fpad, (0, s), (NSA_H, Q_TILE)))(jnp.asarray(starts, jnp.int32))
        n = w.shape[0]
        w = w.reshape(n, NSA_G, NSA_R, Q_TILE).transpose(1, 0, 2, 3).reshape(NSA_G, n, GR_LANES)
        return jnp.where(jnp.asarray(np.tile(valid, (1, NSA_R)))[None], w, NEG_INF)

    tq = np.arange(Q_TILE)[None, :]
    masked_tile = jnp.full((NSA_G, 1, K_TILE, GR_LANES), NEG_INF, F32)
    d0 = (np.arange(N_TOEPLITZ)[:, None] * K_TILE - np.arange(K_TILE)[None, :]).reshape(-1)
    tz = rows(d0, d0[:, None] + tq >= 0).reshape(NSA_G, N_TOEPLITZ, K_TILE, GR_LANES)
    tz = jnp.concatenate([masked_tile, tz], axis=1)
    d0 = ((N_WIN_TILES - 1 - np.arange(N_WIN_TILES))[:, None] * K_TILE - np.arange(K_TILE)[None, :]).reshape(-1)
    d = d0[:, None] + tq
    wz = rows(d0, (d >= 0) & (d < WIN)).reshape(NSA_G, N_WIN_TILES, K_TILE, GR_LANES)
    wz = jnp.concatenate([masked_tile, wz, masked_tile], axis=1)
    d0 = -CMP_STRIDE * (np.arange(2 * nc - SUBLANES) - (nc - SUBLANES)) - (CMP_LEN - 1)
    mc = rows(d0, d0[:, None] + tq >= 0)
    return tz, wz, mc


def _nsa_kernel(qt_ref, gt_ref, kc_ref, vct_ref, ksl_ref, kwn_ref, vslt_ref, vwnt_ref, mc_ref, tz_ref, wz_ref,
                o_ref, imp_scr, y_scr, *group_scr, nc, nsb):
    per_group = len(group_scr) // NSA_G
    qi = pl.program_id(1)
    q0 = qi * Q_TILE
    n_sel = min(N_SEL, nsb)
    cmp_per_q = Q_TILE // CMP_STRIDE
    cmp_per_slc = SLC_LEN // CMP_STRIDE
    pair = 2 * K_TILE
    blocks_per_tile = K_TILE // SLC_LEN

    def lanes4(rows):
        return jnp.concatenate(rows, axis=1)

    def group_rows(g):
        return slice(g * NSA_DH, (g + 1) * NSA_DH)

    def attn_pair(g, k_ref, vt_ref, k0, bias):
        qt_scr, _, m_scr, l_scr, acc_scr = group_scr[g * per_group:g * per_group + 5]
        s = _dot(k_ref[0, pl.ds(k0, pair), :], qt_scr[...]) + bias
        m_prev = m_scr[...]
        m_new = jnp.maximum(m_prev, jnp.max(s, axis=0, keepdims=True))
        alpha = jnp.exp(m_prev - m_new)
        p = jnp.exp(s - m_new)
        l_scr[...] = alpha * l_scr[...] + jnp.sum(p, axis=0, keepdims=True)
        acc_scr[...] = alpha * acc_scr[...] + _dot(vt_ref[0, group_rows(g), pl.ds(k0, pair)], p.astype(BF))
        m_scr[...] = m_new

    def reset(g):
        _, _, m_scr, l_scr, acc_scr = group_scr[g * per_group:g * per_group + 5]
        m_scr[...] = jnp.full_like(m_scr, NEG_INF)
        l_scr[...] = jnp.zeros_like(l_scr)
        acc_scr[...] = jnp.zeros_like(acc_scr)

    def result(g):
        _, _, _, l_scr, acc_scr = group_scr[g * per_group:g * per_group + 5]
        return acc_scr[...] * (1.0 / l_scr[...])

    for g in range(NSA_G):
        qt_scr, selb_scr = group_scr[g * per_group], group_scr[g * per_group + 1]
        ocmp_scr = group_scr[g * per_group + 5]
        zeros = jnp.zeros((NSA_DH, Q_TILE), BF)
        parts = []
        for r in range(NSA_R):
            h = g * NSA_R + r
            qh = qt_ref[0, h * NSA_DH:(h + 1) * NSA_DH, :]
            parts.append(jnp.concatenate([qh, zeros] if g == 0 else [zeros, qh], axis=0))
        qt = lanes4(parts)
        qt_scr[...] = qt

        bias_c = mc_ref[g, pl.ds(pl.multiple_of((nc - SUBLANES) - cmp_per_q * qi, SUBLANES), nc), :]
        s = _dot(kc_ref[0], qt) + bias_c
        m = jnp.max(s, axis=0, keepdims=True)
        e = jnp.exp(s - m)
        lsum = jnp.sum(e, axis=0, keepdims=True)
        tq4 = q0 + lax.broadcasted_iota(jnp.int32, (1, GR_LANES), 1) % Q_TILE
        inv = jnp.where(tq4 >= CMP_LEN - 1, 1.0 / jnp.maximum(lsum, TINY), 0.0)
        p = e * inv
        ocmp_scr[...] = _dot(vct_ref[0, group_rows(g), :], p.astype(BF))
        imp = p[:, 0:Q_TILE]
        for r in range(1, NSA_R):
            imp = imp + p[:, r * Q_TILE:(r + 1) * Q_TILE]

        imp_scr[0:SUBLANES, :] = jnp.zeros((SUBLANES, Q_TILE), F32)
        imp_scr[SUBLANES:SUBLANES + nc, :] = imp
        a = [imp_scr[pl.ds(SUBLANES - 1 + k, nsb, stride=cmp_per_slc), :] for k in range(cmp_per_slc + 1)]
        p_slc = a[0] + a[1]
        for k in range(1, cmp_per_slc):
            p_slc = p_slc + a[k] + a[k + 1]

        jidx = lax.broadcasted_iota(jnp.int32, (nsb, Q_TILE), 0)
        tq = q0 + lax.broadcasted_iota(jnp.int32, (nsb, Q_TILE), 1)
        cur = tq // SLC_LEN
        forced = (jidx == 0) | (jidx == cur) | (jidx == cur - 1)
        score = jnp.where(jidx * SLC_LEN <= tq, p_slc + jnp.where(forced, FORCE_BONUS, 0.0), NEG_INF)
        selb = jnp.full((nsb, Q_TILE), NEG_INF, F32)
        for _ in range(n_sel):
            mx = jnp.max(score, axis=0, keepdims=True)
            first = jnp.min(jnp.where(score == mx, jidx, nsb), axis=0, keepdims=True)
            hit = jidx == first
            selb = jnp.where(hit, 0.0, selb)
            score = jnp.where(hit, -jnp.inf, score)
        selb_scr[...] = selb
        reset(g)

    def slc_body(kp, carry):
        k0 = pl.multiple_of(kp * pair, pair)
        for g in range(NSA_G):
            selb_scr = group_scr[g * per_group + 1]
            halves = []
            for j in range(2):
                tile = 2 * kp + j
                sb = []
                for jj in range(blocks_per_tile):
                    row = selb_scr[pl.ds(tile * blocks_per_tile + jj, 1), :]
                    sb.append(jnp.broadcast_to(lanes4([row] * NSA_R), (SLC_LEN, GR_LANES)))
                halves.append(tz_ref[g, jnp.clip(qi - tile, -1, N_TOEPLITZ - 1) + 1] + jnp.concatenate(sb, axis=0))
            attn_pair(g, ksl_ref, vslt_ref, k0, jnp.concatenate(halves, axis=0))
        return carry

    lax.fori_loop(0, qi // 2 + 1, slc_body, 0)
    for g in range(NSA_G):
        group_scr[g * per_group + 6][...] = result(g)
        reset(g)

    first_tile = qi - (N_WIN_TILES - 1)

    def win_body(kp, carry):
        k0 = pl.multiple_of(kp * pair, pair)
        for g in range(NSA_G):
            bias = jnp.concatenate([wz_ref[g, 2 * kp + j - first_tile + 1] for j in range(2)], axis=0)
            attn_pair(g, kwn_ref, vwnt_ref, k0, bias)
        return carry

    lax.fori_loop(jnp.maximum(first_tile, 0) // 2, qi // 2 + 1, win_body, 0)

    for g in range(NSA_G):
        def gate(branch):
            return lanes4([gt_ref[0, pl.ds((g * NSA_R + r) * 3 + branch, 1), :] for r in range(NSA_R)])

        yt = (gate(0) * group_scr[g * per_group + 5][...] + gate(1) * group_scr[g * per_group + 6][...]
              + gate(2) * result(g))
        for r in range(NSA_R):
            h = g * NSA_R + r
            y_scr[h * NSA_DH:(h + 1) * NSA_DH, :] = yt[:, r * Q_TILE:(r + 1) * Q_TILE]

    o_ref[0] = y_scr[...].T


def _nsa(qt, gt, kc, vct, ksw, vt, tz, wz, mc):
    b, _, l = qt.shape
    nc = l // CMP_STRIDE
    nsb = l // SLC_LEN
    per_b = lambda shape, idx: pl.BlockSpec((1,) + shape, lambda bi, i: (bi,) + idx)
    return pl.pallas_call(
        functools.partial(_nsa_kernel, nc=nc, nsb=nsb),
        grid=(b, l // Q_TILE),
        in_specs=[pl.BlockSpec((1, NSA_W, Q_TILE), lambda bi, i: (bi, 0, i)),
                  pl.BlockSpec((1, 32, Q_TILE), lambda bi, i: (bi, 0, i)),
                  per_b((nc, KV_W), (0, 0)), per_b((KV_W, nc), (0, 0)),
                  per_b((l, KV_W), (0, 0)), per_b((l, KV_W), (0, 1)),
                  per_b((KV_W, l), (0, 0)), per_b((KV_W, l), (1, 0)),
                  _const_spec(mc.shape), _const_spec(tz.shape), _const_spec(wz.shape)],
        out_specs=pl.BlockSpec((1, Q_TILE, NSA_W), lambda bi, i: (bi, i, 0)),
        out_shape=jax.ShapeDtypeStruct((b, l, NSA_W), F32),
        scratch_shapes=[pltpu.VMEM((nc + SUBLANES, Q_TILE), F32), pltpu.VMEM((NSA_W, Q_TILE), F32)]
                       + NSA_G * [pltpu.VMEM((KV_W, GR_LANES), BF), pltpu.VMEM((nsb, Q_TILE), F32),
                                  pltpu.VMEM((1, GR_LANES), F32), pltpu.VMEM((1, GR_LANES), F32),
                                  pltpu.VMEM((NSA_DH, GR_LANES), F32), pltpu.VMEM((NSA_DH, GR_LANES), F32),
                                  pltpu.VMEM((NSA_DH, GR_LANES), F32)],
        compiler_params=_cparams("arbitrary", "arbitrary"),
        name="nsa",
    )(qt, gt, kc, vct, ksw, ksw, vt, vt, mc, tz, wz)


def _split_w_in(w_in):
    o_q = N_TOK_A
    o_kv = o_q + NSA_W
    o_gate = o_kv + 6 * KV_W
    kv = lambda j: w_in[:, o_kv + j * KV_W:o_kv + (j + 1) * KV_W]
    w_tok = jnp.concatenate([w_in[:, :N_TOK_A], kv(0), kv(1), kv(2), kv(4)], axis=1).astype(BF)
    gates = w_in[:, o_gate:]
    pad = jnp.zeros((w_in.shape[0], 32 - gates.shape[1]), w_in.dtype)
    w_t = jnp.concatenate([w_in[:, o_q:o_kv], kv(3), kv(5), gates, pad], axis=1).T.astype(BF)
    return w_tok, w_t


def kernel(x, ffn1_norm, ffn1_w_gate, ffn1_w_up, ffn1_w_down, mix_norm, w_in, w_out, s5_lambda_re, s5_lambda_im, s5_log_dt, s5_b_re, s5_b_im, s5_c_re, s5_c_im, s5_d, s5_w_glu, hgrn_lb_logits, hgrn_norm, nsa_cmp_pos_k, nsa_cmp_w1_k, nsa_cmp_w2_k, nsa_cmp_pos_v, nsa_cmp_w1_v, nsa_cmp_w2_v, rel_bias, ffn2_norm, ffn2_w_gate, ffn2_w_up, ffn2_w_down, final_norm):
    b, l, d = x.shape
    depth = w_in.shape[0]
    gam = jax.nn.softmax(hgrn_lb_logits.astype(F32), axis=0)
    lower_bounds = jnp.cumsum(gam, axis=0) - gam[0:1]
    tz, wz, mc = _bias_tables(rel_bias, l)
    bf = lambda w: w.astype(BF)

    x2 = x.reshape(b * l, d)
    for i in range(depth):
        x2 = _ffn(x2, ffn1_norm[i], bf(ffn1_w_gate[i]), bf(ffn1_w_up[i]), bf(ffn1_w_down[i]))
        w_tok, w_t = _split_w_in(w_in[i])
        za, kcr, vcr, ksw, qt, vt, gt = _inproj(x2.reshape(b, l, d), mix_norm[i], w_tok, w_t)
        bblk, cblk, coef = _s5_params(s5_lambda_re[i], s5_lambda_im[i], s5_log_dt[i], s5_b_re[i], s5_b_im[i],
                                      s5_c_re[i], s5_c_im[i])
        y_s5 = _s5(za, bblk, cblk, coef, s5_d[i], s5_w_glu[i])
        y_hg = _hgrn(za, lower_bounds[i], hgrn_norm[i])
        kc, vct = _compress(kcr, vcr,
                            _compress_params(nsa_cmp_pos_k[i], nsa_cmp_w1_k[i], nsa_cmp_w2_k[i]),
                            _compress_params(nsa_cmp_pos_v[i], nsa_cmp_w1_v[i], nsa_cmp_w2_v[i], transpose_out=True))
        y_nsa = _nsa(qt, gt, kc, vct, ksw, vt, tz, wz, mc)
        proj = (y_s5.reshape(b * l, S5_W), y_hg.reshape(b * l, HG_W), y_nsa.reshape(b * l, NSA_W), bf(w_out[i]))
        x2 = _ffn(x2, ffn2_norm[i], bf(ffn2_w_gate[i]), bf(ffn2_w_up[i]), bf(ffn2_w_down[i]), proj=proj,
                  final_g=final_norm if i + 1 == depth else None)
    return x2.reshape(b, l, d)
```

```python
import functools
import math

import numpy as np
import jax
import jax.numpy as jnp
from jax import lax
from jax.experimental import pallas as pl
from jax.experimental.pallas import tpu as pltpu

BF = jnp.bfloat16
F32 = jnp.float32

EPS = 1e-6
NEG_INF = -1e30
TINY = 1e-30
FORCE_BONUS = 1e4
LOG2E = math.log2(math.e)

S5_W = 256
S5_G = 16
S5_GROUP = 16
S5_P = 64
S5_N = S5_G * S5_P
HG_W = 256
HG_HEADS = 4
HG_D = 64
NSA_W = 512
NSA_DH = 64
NSA_H = 8
NSA_G = 2
NSA_R = 4
KV_W = NSA_G * NSA_DH
CMP_LEN = 32
CMP_STRIDE = 16
SLC_LEN = 64
N_SEL = 16
WIN = 512
REL_BUCKETS = 32
REL_MAX_DIST = 1024

Q_TILE = 128
K_TILE = 128
GR_LANES = NSA_R * Q_TILE
N_TOEPLITZ = REL_MAX_DIST // K_TILE + 2
N_WIN_TILES = WIN // K_TILE + 1
SLAB = 16
HG_CHUNK = 64
HG_LEVELS = 6
SUBLANES = 8
VMEM_LIMIT_BYTES = 56 * 1024 * 1024


def _cparams(*sem):
    return pltpu.CompilerParams(dimension_semantics=sem, vmem_limit_bytes=VMEM_LIMIT_BYTES)


def _const_spec(shape):
    nd = len(shape)
    return pl.BlockSpec(shape, lambda *_: (0,) * nd)


def _rms(x, g_row):
    ms = jnp.mean(x * x, axis=-1, keepdims=True)
    return x * lax.rsqrt(ms + EPS) * g_row


def _silu(x):
    return x * jax.nn.sigmoid(x)


def _gelu_tanh(x):
    return 0.5 * x * (1.0 + jnp.tanh(math.sqrt(2.0 / math.pi) * (x + 0.044715 * (x * x * x))))


def _dot(a, b):
    return jnp.dot(a, b, preferred_element_type=F32)


def _dot_nt(a, b):
    return lax.dot_general(a, b, (((1,), (1,)), ((), ())), preferred_element_type=F32)


def _dot_tn(a, b):
    return lax.dot_general(a, b, (((0,), (0,)), ((), ())), preferred_element_type=F32)


def _dot_exact_lhs(c_bf, x):
    hi = x.astype(BF)
    r1 = x - hi.astype(F32)
    mid = r1.astype(BF)
    lo = (r1 - mid.astype(F32)).astype(BF)
    return _dot(c_bf, hi) + _dot(c_bf, mid) + _dot(c_bf, lo)


def _dot_exact_rhs(x, c_bf):
    hi = x.astype(BF)
    r1 = x - hi.astype(F32)
    mid = r1.astype(BF)
    lo = (r1 - mid.astype(F32)).astype(BF)
    return _dot(hi, c_bf) + _dot(mid, c_bf) + _dot(lo, c_bf)


def _ffn_kernel(*refs, n_chunks, tf, with_proj, with_final):
    it = iter(refs)
    x_ref = next(it)
    if with_proj:
        ys5_ref, yhg_ref, ynsa_ref, wo_ref = next(it), next(it), next(it), next(it)
    g_ref, wg_ref, wu_ref, wd_ref = next(it), next(it), next(it), next(it)
    if with_final:
        fg_ref = next(it)
    o_ref, h_scr, a_scr = next(it), next(it), next(it)

    x = x_ref[...]
    if with_proj:
        x = (x + _dot(ys5_ref[...].astype(BF), wo_ref[0:S5_W, :])
             + _dot(yhg_ref[...].astype(BF), wo_ref[S5_W:S5_W + HG_W, :])
             + _dot(ynsa_ref[...].astype(BF), wo_ref[S5_W + HG_W:, :]))
    h_scr[...] = _rms(x, g_ref[...]).astype(BF)
    for c in range(n_chunks):
        sl = slice(c * tf, (c + 1) * tf)
        h = h_scr[...]
        gate = _dot(h, wg_ref[:, sl])
        up = _dot(h, wu_ref[:, sl])
        a_scr[:, sl] = (_silu(gate) * up).astype(BF)
    x = x + 0.5 * _dot(a_scr[...], wd_ref[...])
    if with_final:
        x = _rms(x, fg_ref[...])
    o_ref[...] = x


def _ffn(x2, g, wg, wu, wd, proj=None, final_g=None, tm=512, tf=256):
    n, d = x2.shape
    dff = wg.shape[1]
    assert n % tm == 0 and dff % tf == 0
    row = lambda i: (i, 0)
    in_specs = [pl.BlockSpec((tm, d), row)]
    args = [x2]
    if proj is not None:
        ys5, yhg, ynsa, wo = proj
        in_specs += [pl.BlockSpec((tm, S5_W), row), pl.BlockSpec((tm, HG_W), row),
                     pl.BlockSpec((tm, NSA_W), row), _const_spec(wo.shape)]
        args += [ys5, yhg, ynsa, wo]
    in_specs += [_const_spec((1, d)), _const_spec(wg.shape), _const_spec(wu.shape), _const_spec(wd.shape)]
    args += [g.reshape(1, d), wg, wu, wd]
    if final_g is not None:
        in_specs.append(_const_spec((1, d)))
        args.append(final_g.reshape(1, d))
    kern = functools.partial(_ffn_kernel, n_chunks=dff // tf, tf=tf,
                             with_proj=proj is not None, with_final=final_g is not None)
    return pl.pallas_call(
        kern,
        grid=(n // tm,),
        in_specs=in_specs,
        out_specs=pl.BlockSpec((tm, d), row),
        out_shape=jax.ShapeDtypeStruct((n, d), F32),
        scratch_shapes=[pltpu.VMEM((tm, d), BF), pltpu.VMEM((tm, dff), BF)],
        compiler_params=_cparams("arbitrary"),
        name="ffn",
    )(*args)


N_TOK_A = S5_W + 4 * HG_W
N_T_ROWS = NSA_W + 2 * KV_W + 32


def _inproj_kernel(x_ref, g_ref, wtok_ref, wt_ref, za_ref, kcr_ref, vcr_ref, ksw_ref, qt_ref, vt_ref, gt_ref):
    h = _rms(x_ref[0], g_ref[...]).astype(BF)
    for c in range(N_TOK_A // 256):
        sl = slice(c * 256, (c + 1) * 256)
        za_ref[0, :, sl] = _dot(h, wtok_ref[:, sl])
    kcr_ref[0] = _dot(h, wtok_ref[:, N_TOK_A:N_TOK_A + KV_W])
    vcr_ref[0] = _dot(h, wtok_ref[:, N_TOK_A + KV_W:N_TOK_A + 2 * KV_W])
    ksw_ref[0] = _dot(h, wtok_ref[:, N_TOK_A + 2 * KV_W:N_TOK_A + 4 * KV_W]).astype(BF)
    qt_ref[0] = (_dot_nt(wt_ref[0:NSA_W, :], h) * (NSA_DH ** -0.5 * LOG2E)).astype(BF)
    vt_ref[0] = _dot_nt(wt_ref[NSA_W:NSA_W + 2 * KV_W, :], h).astype(BF)
    gt_ref[0] = jax.nn.sigmoid(_dot_nt(wt_ref[NSA_W + 2 * KV_W:, :], h))


def _inproj(x3, g, w_tok, w_t, tm=512):
    b, l, d = x3.shape
    assert l % tm == 0
    tok = lambda w: pl.BlockSpec((1, tm, w), lambda bi, i: (bi, i, 0))
    tr = lambda r: pl.BlockSpec((1, r, tm), lambda bi, i: (bi, 0, i))
    return pl.pallas_call(
        _inproj_kernel,
        grid=(b, l // tm),
        in_specs=[tok(d), _const_spec((1, d)), _const_spec(w_tok.shape), _const_spec(w_t.shape)],
        out_specs=[tok(N_TOK_A), tok(KV_W), tok(KV_W), tok(2 * KV_W), tr(NSA_W), tr(2 * KV_W), tr(32)],
        out_shape=[jax.ShapeDtypeStruct((b, l, N_TOK_A), F32),
                   jax.ShapeDtypeStruct((b, l, KV_W), F32),
                   jax.ShapeDtypeStruct((b, l, KV_W), F32),
                   jax.ShapeDtypeStruct((b, l, 2 * KV_W), BF),
                   jax.ShapeDtypeStruct((b, NSA_W, l), BF),
                   jax.ShapeDtypeStruct((b, 2 * KV_W, l), BF),
                   jax.ShapeDtypeStruct((b, 32, l), F32)],
        compiler_params=_cparams("arbitrary", "arbitrary"),
        name="inproj",
    )(x3, g.reshape(1, d), w_tok, w_t)


def _s5_kernel(u_ref, bblk_ref, cblk_ref, coef_ref, d_ref, wglu_ref, y_ref, xs_scr, carry_scr, *, tc):
    @pl.when(pl.program_id(1) == 0)
    def _():
        carry_scr[...] = jnp.zeros_like(carry_scr)

    u = u_ref[0]
    xs_scr[...] = _dot(u.astype(BF), bblk_ref[...])

    def body(r, carry):
        cre, cim = carry
        row = pl.multiple_of(r * SUBLANES, SUBLANES)
        xre = xs_scr[pl.ds(row, SUBLANES), 0:S5_N]
        xim = xs_scr[pl.ds(row, SUBLANES), S5_N:2 * S5_N]
        for idx, k in enumerate((1, 2, 4)):
            are, aim = coef_ref[idx, 0], coef_ref[idx, 1]
            sre, sim = pltpu.roll(xre, k, 0), pltpu.roll(xim, k, 0)
            xre, xim = xre + (are * sre - aim * sim), xim + (are * sim + aim * sre)
        pre, pim = coef_ref[3, 0], coef_ref[3, 1]
        xre, xim = xre + (pre * cre - pim * cim), xim + (pre * cim + pim * cre)
        xs_scr[pl.ds(row, SUBLANES), 0:S5_N] = xre
        xs_scr[pl.ds(row, SUBLANES), S5_N:2 * S5_N] = xim
        return xre[SUBLANES - 1:SUBLANES], xim[SUBLANES - 1:SUBLANES]

    cre, cim = lax.fori_loop(0, tc // SUBLANES, body, (carry_scr[0:1], carry_scr[1:2]))
    carry_scr[0:1] = cre
    carry_scr[1:2] = cim

    y = _dot(xs_scr[...].astype(BF), cblk_ref[...]) + d_ref[...] * u
    y = _gelu_tanh(y)
    y_ref[0] = y * jax.nn.sigmoid(_dot(y.astype(BF), wglu_ref[...]))


def _s5_params(lam_re, lam_im, log_dt, b_re, b_im, c_re, c_im):
    lr, li = lam_re.astype(F32), lam_im.astype(F32)
    dt = jnp.exp(log_dt.astype(F32))[:, None]
    mag = jnp.exp(lr * dt)
    ab_re, ab_im = mag * jnp.cos(li * dt), mag * jnp.sin(li * dt)
    den = lr * lr + li * li
    nr, ni = ab_re - 1.0, ab_im
    g_re = (nr * lr + ni * li) / den
    g_im = (ni * lr - nr * li) / den
    br, bi = b_re.astype(F32), b_im.astype(F32)
    bb_re = g_re[..., None] * br - g_im[..., None] * bi
    bb_im = g_re[..., None] * bi + g_im[..., None] * br
    eye = jnp.eye(S5_G, dtype=F32)
    blk = lambda w: jnp.einsum('gph,gk->ghkp', w, eye).reshape(S5_W, S5_N)
    bblk = jnp.concatenate([blk(bb_re), blk(bb_im)], axis=1).astype(BF)
    cblk_f = lambda w: jnp.einsum('ghp,gk->gpkh', w, eye).reshape(S5_N, S5_W)
    cblk = jnp.concatenate([cblk_f(c_re.astype(F32)), -cblk_f(c_im.astype(F32))], axis=0).astype(BF)
    are, aim = ab_re.reshape(1, S5_N), ab_im.reshape(1, S5_N)
    pw = [(are, aim)]
    for _ in range(SUBLANES - 1):
        pr, pi = pw[-1]
        pw.append((pr * are - pi * aim, pr * aim + pi * are))
    rows = np.arange(SUBLANES)[:, None]
    coef = []
    for k in (1, 2, 4):
        m = jnp.asarray((rows >= k).astype(np.float32))
        coef.append(jnp.stack([m * pw[k - 1][0], m * pw[k - 1][1]]))
    coef.append(jnp.stack([jnp.concatenate([p[0] for p in pw], axis=0), jnp.concatenate([p[1] for p in pw], axis=0)]))
    return bblk, cblk, jnp.stack(coef)


def _s5(za, bblk, cblk, coef, d, w_glu, tc=512):
    b, l, _ = za.shape
    assert l % tc == 0
    return pl.pallas_call(
        functools.partial(_s5_kernel, tc=tc),
        grid=(b, l // tc),
        in_specs=[pl.BlockSpec((1, tc, S5_W), lambda bi, i: (bi, i, 0)),
                  _const_spec(bblk.shape), _const_spec(cblk.shape), _const_spec(coef.shape),
                  _const_spec((1, S5_W)), _const_spec(w_glu.shape)],
        out_specs=pl.BlockSpec((1, tc, S5_W), lambda bi, i: (bi, i, 0)),
        out_shape=jax.ShapeDtypeStruct((b, l, S5_W), F32),
        scratch_shapes=[pltpu.VMEM((tc, 2 * S5_N), F32), pltpu.VMEM((2, S5_N), F32)],
        compiler_params=_cparams("arbitrary", "arbitrary"),
        name="s5",
    )(za, bblk, cblk, coef, d.reshape(1, S5_W).astype(F32), w_glu.astype(BF))


def _hgrn_constants():
    c = HG_CHUNK
    t = np.arange(c)[:, None]
    u = np.arange(c)[None, :]
    mats = [(u <= t), (u > t)]
    masks = []
    for lv in range(HG_LEVELS):
        n = c >> lv
        half = n // 2
        ref = (t // n) * n + half - 1
        lower = (t % n) >= half
        mats.append(np.where(lower, (u > ref) & (u <= t), (u > t) & (u <= ref)))
        same = (t // n) == (u // n)
        masks.append(same & lower & ((u % n) < half))
    masks.append(t == u)
    gall = np.concatenate(mats, axis=0).astype(np.float32)
    mstk = np.stack([np.tile(m, (HG_HEADS, 1)) for m in masks]).astype(np.float32)
    lane_head = np.arange(HG_W)[None, :] // HG_D
    hmask = (np.repeat(np.arange(HG_HEADS), c)[:, None] == lane_head).astype(np.float32)
    bd = (np.arange(HG_W)[:, None] // HG_D == lane_head).astype(np.float32)
    return gall, mstk, hmask, bd


def _hgrn_kernel(q_ref, f_ref, i_ref, g_ref, lb_ref, gain_ref, gall_ref, mstk_ref, hmask_ref, bd_ref, bdn_ref,
                 o_ref, st_scr, *, tt):
    c = HG_CHUNK

    @pl.when(pl.program_id(1) == 0)
    def _():
        st_scr[...] = jnp.zeros_like(st_scr)

    lb = lb_ref[...]
    hmask = hmask_ref[...]

    def chunk(ci, carry):
        row = pl.multiple_of(ci * c, c)
        fl = f_ref[0, pl.ds(row, c), :]
        qf = _silu(q_ref[0, pl.ds(row, c), :])
        f = lb + (1.0 - lb) * jax.nn.sigmoid(fl)
        lf = jnp.log(jnp.maximum(f, TINY))
        kf = (1.0 - lb) * jax.nn.sigmoid(-fl)
        v = i_ref[0, pl.ds(row, c), :]
        v_bf = v.astype(BF)
        e_all = jnp.exp(_dot_exact_lhs(gall_ref[...], lf))
        e_b = e_all[0:c]
        e_suf = e_all[c:2 * c]

        att = jnp.zeros((HG_HEADS * c, c), F32)
        for lv in range(HG_LEVELS + 1):
            if lv < HG_LEVELS:
                e = e_all[(2 + lv) * c:(3 + lv) * c]
                z, w = qf * e, kf * e
            else:
                z, w = qf, kf
            zs = (jnp.concatenate([z] * HG_HEADS, axis=0) * hmask).astype(BF)
            att = att + mstk_ref[lv] * _dot_nt(zs, w.astype(BF))
        o4 = _dot(att.astype(BF), v_bf) * hmask
        o = o4[0:c]
        for h in range(1, HG_HEADS):
            o = o + o4[h * c:(h + 1) * c]

        st = st_scr[...]
        o = o + _dot_nt((qf * e_b).astype(BF), st.astype(BF))
        st_scr[...] = e_b[c - 1:c] * st + bd_ref[...] * _dot_tn(v_bf, (kf * e_suf).astype(BF))

        ms = _dot_exact_rhs(o * o, bdn_ref[...])
        o = o * lax.rsqrt(ms + EPS) * gain_ref[...]
        o_ref[0, pl.ds(row, c), :] = o * _silu(g_ref[0, pl.ds(row, c), :])
        return carry

    lax.fori_loop(0, tt // c, chunk, 0)


def _hgrn(za, lb, gain, tt=512):
    b, l, _ = za.shape
    assert l % tt == 0
    gall, mstk, hmask, bd = _hgrn_constants()
    col = lambda j: pl.BlockSpec((1, tt, HG_W), lambda bi, i: (bi, i, j))
    consts = [jnp.asarray(gall, BF), jnp.asarray(mstk, F32), jnp.asarray(hmask, F32), jnp.asarray(bd, F32),
              jnp.asarray(bd / HG_D, BF)]
    return pl.pallas_call(
        functools.partial(_hgrn_kernel, tt=tt),
        grid=(b, l // tt),
        in_specs=[col(1), col(2), col(3), col(4), _const_spec((1, HG_W)), _const_spec((1, HG_W))]
                 + [_const_spec(x.shape) for x in consts],
        out_specs=pl.BlockSpec((1, tt, HG_W), lambda bi, i: (bi, i, 0)),
        out_shape=jax.ShapeDtypeStruct((b, l, HG_W), F32),
        scratch_shapes=[pltpu.VMEM((HG_W, HG_W), F32)],
        compiler_params=_cparams("arbitrary", "arbitrary"),
        name="hgrn2",
    )(za, za, za, za, lb.reshape(1, HG_W).astype(F32), jnp.tile(gain.astype(F32), HG_HEADS).reshape(1, HG_W),
      *consts)


def _compress_kernel(k16_ref, v16_ref, kpa_ref, kpb_ref, kw1a_ref, kw1b_ref, kw2_ref,
                     vpa_ref, vpb_ref, vw1a_ref, vw1b_ref, vw2_ref, kc_ref, vct_ref, *, nc):
    def hidden(x16, pa, pb, w1a, w1b):
        first = _dot((x16 + pa).astype(BF), w1a)
        second = _dot((x16 + pb).astype(BF), w1b)
        pre = first + pltpu.roll(second, nc - 1, 0)
        rows = lax.broadcasted_iota(jnp.int32, pre.shape, 0)
        pre = jnp.where(rows < nc - 1, pre, 0.0)
        return _gelu_tanh(pre).astype(BF)

    hk = hidden(k16_ref[0], kpa_ref[...], kpb_ref[...], kw1a_ref[...], kw1b_ref[...])
    kc_ref[0] = _dot(hk, kw2_ref[...]).astype(BF)
    hv = hidden(v16_ref[0], vpa_ref[...], vpb_ref[...], vw1a_ref[...], vw1b_ref[...])
    vct_ref[0] = _dot_nt(vw2_ref[...], hv).astype(BF)


def _compress_params(pos, w1, w2, transpose_out=False):
    eye = jnp.eye(NSA_G, dtype=F32)
    w1r = w1.astype(F32).reshape(CMP_LEN, NSA_DH, NSA_DH)
    wexp = jnp.einsum('jde,gh->jgdhe', w1r, eye).reshape(CMP_LEN, KV_W, KV_W)
    half = CMP_LEN // 2
    w1a = wexp[:half].reshape(half * KV_W, KV_W).astype(BF)
    w1b = wexp[half:].reshape(half * KV_W, KV_W).astype(BF)
    pt = jnp.broadcast_to(pos.astype(F32)[:, None, :], (CMP_LEN, NSA_G, NSA_DH))
    pa = pt[:half].reshape(1, half * KV_W)
    pb = pt[half:].reshape(1, half * KV_W)
    w2bd = jnp.einsum('de,gh->gdhe', w2.astype(F32), eye).reshape(KV_W, KV_W).astype(BF)
    return pa, pb, w1a, w1b, (w2bd.T if transpose_out else w2bd)


def _compress(kcr, vcr, kparams, vparams):
    b, l, _ = kcr.shape
    nc = l // CMP_STRIDE
    w = CMP_STRIDE * KV_W
    k16 = kcr.reshape(b, nc, w)
    v16 = vcr.reshape(b, nc, w)
    per_b = lambda r, cc: pl.BlockSpec((1, r, cc), lambda bi: (bi, 0, 0))
    params = list(kparams) + list(vparams)
    return pl.pallas_call(
        functools.partial(_compress_kernel, nc=nc),
        grid=(b,),
        in_specs=[per_b(nc, w), per_b(nc, w)] + [_const_spec(p.shape) for p in params],
        out_specs=[per_b(nc, KV_W), per_b(KV_W, nc)],
        out_shape=[jax.ShapeDtypeStruct((b, nc, KV_W), BF), jax.ShapeDtypeStruct((b, KV_W, nc), BF)],
        compiler_params=_cparams("arbitrary"),
        name="compress",
    )(k16, v16, *params)


def _t5_bucket(dist):
    n = jnp.maximum(dist, 0)
    max_exact = REL_BUCKETS // 2
    nf = jnp.maximum(n, max_exact).astype(jnp.float32)
    large = max_exact + (jnp.log(nf / max_exact) / math.log(REL_MAX_DIST / max_exact)
                         * (REL_BUCKETS - max_exact)).astype(jnp.int32)
    large = jnp.minimum(large, REL_BUCKETS - 1)
    return jnp.where(n < max_exact, n, large)


def _bias_tables(rel_bias, l):
    tab = rel_bias.astype(F32) * LOG2E
    by_dist = tab[_t5_bucket(jnp.arange(l))].T
    nc = l // CMP_STRIDE
    w = 2 * K_TILE

    def shifted_rows(v, n_rows, step):
        flat = jnp.tile(v, (1,) * (v.ndim - 1) + (n_rows,))[..., :n_rows * (w - step)]
        return flat.reshape(v.shape[:-1] + (n_rows, w - step))[..., :Q_TILE]

    def group_layout(t, valid):
        _, n, r, q = t.shape
        t = jnp.where(jnp.asarray(valid)[None], t, NEG_INF)
        return t.reshape(NSA_G, NSA_R, n, r, q).transpose(0, 2, 3, 1, 4).reshape(NSA_G, n, r, NSA_R * q)

    tq = np.arange(Q_TILE)[None, None, :]
    key = np.arange(K_TILE)[None, :, None]
    blocks = jnp.pad(by_dist, ((0, 0), (K_TILE, w)), mode='edge').reshape(NSA_H, -1, K_TILE)
    vec = jnp.concatenate([blocks[:, 1:N_TOEPLITZ + 1], blocks[:, 0:N_TOEPLITZ]], axis=-1)
    toep = shifted_rows(vec, K_TILE, 1)
    masked_tile = jnp.full((NSA_G, 1, K_TILE, GR_LANES), NEG_INF, F32)
    d = np.arange(N_TOEPLITZ)[:, None, None] * K_TILE + tq - key
    tz = jnp.concatenate([masked_tile, group_layout(toep, d >= 0)], axis=1)
    d = (N_WIN_TILES - 1 - np.arange(N_WIN_TILES))[:, None, None] * K_TILE + tq - key
    wz = group_layout(toep[:, N_WIN_TILES - 1::-1], (d >= 0) & (d < WIN))
    wz = jnp.concatenate([masked_tile, wz, masked_tile], axis=1)
    na = (2 * nc - SUBLANES) // SUBLANES
    c0 = CMP_STRIDE * (nc - SUBLANES) - (CMP_LEN - 1)
    front = K_TILE * (na + 1)
    off = c0 % K_TILE
    padded = jnp.pad(by_dist, ((0, 0), (front, w)), mode='edge')
    nb = (padded.shape[1] - off) // K_TILE
    blocks = padded[:, off:off + nb * K_TILE].reshape(NSA_H, nb, K_TILE)
    k0 = (front + c0 - off) // K_TILE
    vec = jnp.concatenate([blocks[:, k0 - na + 1:k0 + 1][:, ::-1], blocks[:, k0 - na:k0][:, ::-1]], axis=-1)
    cmp_rows = shifted_rows(vec, SUBLANES, CMP_STRIDE).reshape(NSA_H, 1, na * SUBLANES, Q_TILE)
    u = np.arange(na * SUBLANES)[None, :, None]
    mc = group_layout(cmp_rows, tq - CMP_STRIDE * (u - (nc - SUBLANES)) - (CMP_LEN - 1) >= 0)[:, 0]
    return tz, wz, mc


def _nsa_kernel(qt_ref, gt_ref, kc_ref, vct_ref, ksl_ref, kwn_ref, vslt_ref, vwnt_ref, mc_ref, tz_ref, wz_ref,
                ind_ref, expand_ref, o_ref, imp_scr, y_scr, *group_scr, nc, nsb):
    per_group = len(group_scr) // NSA_G
    qi = pl.program_id(1)
    q0 = qi * Q_TILE
    n_sel = min(N_SEL, nsb)
    cmp_per_q = Q_TILE // CMP_STRIDE
    cmp_per_slc = SLC_LEN // CMP_STRIDE
    pair = 2 * K_TILE
    n_last = qi // 2
    ones_rows = (lax.broadcasted_iota(jnp.int32, (SLAB, pair), 0) == 0).astype(BF)

    def lanes4(rows):
        return jnp.concatenate(rows, axis=1)

    def group_rows(g):
        return slice(g * NSA_DH, (g + 1) * NSA_DH)

    def scores(g, kp, selected):
        qt_scr, slab_scr = group_scr[g * per_group:g * per_group + 2]
        k0 = pl.multiple_of(kp * pair, pair)
        if selected:
            lhs = jnp.concatenate([ksl_ref[0, pl.ds(k0, pair), :], ind_ref[...]], axis=1)
            slab = slab_scr[pl.ds(pl.multiple_of(kp * SLAB, SLAB), SLAB), :]
            rhs = jnp.concatenate([qt_scr[...], slab, jnp.zeros((KV_W - SLAB, GR_LANES), BF)], axis=0)
            bias = [tz_ref[g, jnp.clip(qi - (2 * kp + j), -1, N_TOEPLITZ - 1) + 1] for j in range(2)]
        else:
            lhs, rhs = kwn_ref[0, pl.ds(k0, pair), :], qt_scr[...]
            bias = [wz_ref[g, 2 * kp + j - (qi - (N_WIN_TILES - 1)) + 1] for j in range(2)]
        s = _dot(lhs, rhs) + jnp.concatenate(bias, axis=0)
        return s, jnp.max(s, axis=0, keepdims=True)

    def accumulate(g, kp, vt_ref, s, s_max):
        m_scr, acc_scr = group_scr[g * per_group + 2:g * per_group + 4]
        k0 = pl.multiple_of(kp * pair, pair)
        m_prev = m_scr[...]
        m_new = jnp.maximum(m_prev, s_max)
        p = jnp.exp2(s - m_new).astype(BF)
        vt = jnp.concatenate([vt_ref[0, group_rows(g), pl.ds(k0, pair)], ones_rows], axis=0)
        acc_scr[...] = jnp.exp2(m_prev - m_new) * acc_scr[...] + _dot(vt, p)
        m_scr[...] = m_new

    def sweep(first, selected, vt_ref):
        def fill(kp, buf):
            for g in range(NSA_G):
                s, s_max = scores(g, kp, selected)
                group_scr[g * per_group + 6 + buf][...] = s
                group_scr[g * per_group + 8 + buf][...] = s_max

        def drain(kp, buf):
            for g in range(NSA_G):
                accumulate(g, kp, vt_ref, group_scr[g * per_group + 6 + buf][...],
                           group_scr[g * per_group + 8 + buf][...])

        fill(first, 0)

        def body(j, carry):
            kp = first + 2 * j
            fill(kp + 1, 1)
            drain(kp, 0)
            fill(kp + 2, 0)
            drain(kp + 1, 1)
            return carry

        remaining = n_last - first
        lax.fori_loop(0, remaining // 2, body, 0)

        @pl.when(remaining % 2 == 1)
        def _():
            fill(n_last, 1)
            drain(n_last - 1, 0)
            drain(n_last, 1)

        @pl.when(remaining % 2 == 0)
        def _():
            drain(n_last, 0)

    def reset(g):
        m_scr, acc_scr = group_scr[g * per_group + 2:g * per_group + 4]
        m_scr[...] = jnp.full_like(m_scr, NEG_INF)
        acc_scr[...] = jnp.zeros_like(acc_scr)

    def result(g):
        acc_scr = group_scr[g * per_group + 3]
        return acc_scr[0:NSA_DH] * (1.0 / acc_scr[NSA_DH:NSA_DH + 1])

    for g in range(NSA_G):
        qt_scr, slab_scr = group_scr[g * per_group], group_scr[g * per_group + 1]
        ocmp_scr = group_scr[g * per_group + 4]
        zeros = jnp.zeros((NSA_DH, Q_TILE), BF)
        parts = []
        for r in range(NSA_R):
            h = g * NSA_R + r
            qh = qt_ref[0, h * NSA_DH:(h + 1) * NSA_DH, :]
            parts.append(jnp.concatenate([qh, zeros] if g == 0 else [zeros, qh], axis=0))
        qt = lanes4(parts)
        qt_scr[...] = qt

        bias_c = mc_ref[g, pl.ds(pl.multiple_of((nc - SUBLANES) - cmp_per_q * qi, SUBLANES), nc), :]
        s = _dot(kc_ref[0], qt) + bias_c
        m = jnp.max(s, axis=0, keepdims=True)
        e = jnp.exp2(s - m)
        lsum = jnp.sum(e, axis=0, keepdims=True)
        tq4 = q0 + lax.broadcasted_iota(jnp.int32, (1, GR_LANES), 1) % Q_TILE
        inv = jnp.where(tq4 >= CMP_LEN - 1, 1.0 / jnp.maximum(lsum, TINY), 0.0)
        p = e * inv
        ocmp_scr[...] = _dot(vct_ref[0, group_rows(g), :], p.astype(BF))
        imp = p[:, 0:Q_TILE]
        for r in range(1, NSA_R):
            imp = imp + p[:, r * Q_TILE:(r + 1) * Q_TILE]

        imp_scr[0:SUBLANES, :] = jnp.zeros((SUBLANES, Q_TILE), F32)
        imp_scr[SUBLANES:SUBLANES + nc, :] = imp
        a = [imp_scr[pl.ds(SUBLANES - 1 + k, nsb, stride=cmp_per_slc), :] for k in range(cmp_per_slc + 1)]
        p_slc = a[0] + a[1]
        for k in range(1, cmp_per_slc):
            p_slc = p_slc + a[k] + a[k + 1]

        jidx = lax.broadcasted_iota(jnp.int32, (nsb, Q_TILE), 0)
        tq = q0 + lax.broadcasted_iota(jnp.int32, (nsb, Q_TILE), 1)
        cur = tq // SLC_LEN
        forced = (jidx == 0) | (jidx == cur) | (jidx == cur - 1)
        score = jnp.where(jidx * SLC_LEN <= tq, p_slc + jnp.where(forced, FORCE_BONUS, 0.0), NEG_INF)
        selb = jnp.full((nsb, Q_TILE), NEG_INF, F32)
        for _ in range(n_sel):
            mx = jnp.max(score, axis=0, keepdims=True)
            first = jnp.min(jnp.where(score == mx, jidx, nsb), axis=0, keepdims=True)
            hit = jidx == first
            selb = jnp.where(hit, 0.0, selb)
            score = jnp.where(hit, -jnp.inf, score)
        slab_scr[...] = _dot(expand_ref[...], lanes4([selb.astype(BF)] * NSA_R)).astype(BF)
        reset(g)

    sweep(jnp.int32(0), True, vslt_ref)
    for g in range(NSA_G):
        group_scr[g * per_group + 5][...] = result(g)
        reset(g)

    sweep(jnp.maximum(qi - (N_WIN_TILES - 1), 0) // 2, False, vwnt_ref)

    for g in range(NSA_G):
        def gate(branch):
            return lanes4([gt_ref[0, pl.ds((g * NSA_R + r) * 3 + branch, 1), :] for r in range(NSA_R)])

        yt = (gate(0) * group_scr[g * per_group + 4][...] + gate(1) * group_scr[g * per_group + 5][...]
              + gate(2) * result(g))
        for r in range(NSA_R):
            h = g * NSA_R + r
            y_scr[h * NSA_DH:(h + 1) * NSA_DH, :] = yt[:, r * Q_TILE:(r + 1) * Q_TILE]

    o_ref[0] = y_scr[...].T


def _nsa(qt, gt, kc, vct, ksw, vt, tz, wz, mc):
    b, _, l = qt.shape
    nc = l // CMP_STRIDE
    nsb = l // SLC_LEN
    per_b = lambda shape, idx: pl.BlockSpec((1,) + shape, lambda bi, i: (bi,) + idx)
    pair = 2 * K_TILE
    blocks_per_pair = pair // SLC_LEN
    n_pairs = l // pair
    ind = (np.arange(pair)[:, None] // SLC_LEN == np.arange(KV_W)[None, :]).astype(np.float32)
    expand = np.zeros((n_pairs * SLAB, nsb), np.float32)
    for k in range(n_pairs):
        for j in range(blocks_per_pair):
            expand[k * SLAB + j, k * blocks_per_pair + j] = 1.0
    ind, expand = jnp.asarray(ind, BF), jnp.asarray(expand, BF)
    return pl.pallas_call(
        functools.partial(_nsa_kernel, nc=nc, nsb=nsb),
        grid=(b, l // Q_TILE),
        in_specs=[pl.BlockSpec((1, NSA_W, Q_TILE), lambda bi, i: (bi, 0, i)),
                  pl.BlockSpec((1, 32, Q_TILE), lambda bi, i: (bi, 0, i)),
                  per_b((nc, KV_W), (0, 0)), per_b((KV_W, nc), (0, 0)),
                  per_b((l, KV_W), (0, 0)), per_b((l, KV_W), (0, 1)),
                  per_b((KV_W, l), (0, 0)), per_b((KV_W, l), (1, 0)),
                  _const_spec(mc.shape), _const_spec(tz.shape), _const_spec(wz.shape),
                  _const_spec(ind.shape), _const_spec(expand.shape)],
        out_specs=pl.BlockSpec((1, Q_TILE, NSA_W), lambda bi, i: (bi, i, 0)),
        out_shape=jax.ShapeDtypeStruct((b, l, NSA_W), F32),
        scratch_shapes=[pltpu.VMEM((nc + SUBLANES, Q_TILE), F32), pltpu.VMEM((NSA_W, Q_TILE), F32)]
                       + NSA_G * [pltpu.VMEM((KV_W, GR_LANES), BF), pltpu.VMEM((n_pairs * SLAB, GR_LANES), BF),
                                  pltpu.VMEM((1, GR_LANES), F32), pltpu.VMEM((NSA_DH + SLAB, GR_LANES), F32),
                                  pltpu.VMEM((NSA_DH, GR_LANES), F32), pltpu.VMEM((NSA_DH, GR_LANES), F32),
                                  pltpu.VMEM((pair, GR_LANES), F32), pltpu.VMEM((pair, GR_LANES), F32),
                                  pltpu.VMEM((1, GR_LANES), F32), pltpu.VMEM((1, GR_LANES), F32)],
        compiler_params=_cparams("arbitrary", "arbitrary"),
        name="nsa",
    )(qt, gt, kc, vct, ksw, ksw, vt, vt, mc, tz, wz, ind, expand)


def _split_w_in(w_in):
    o_q = N_TOK_A
    o_kv = o_q + NSA_W
    o_gate = o_kv + 6 * KV_W
    kv = lambda j: w_in[:, o_kv + j * KV_W:o_kv + (j + 1) * KV_W]
    w_tok = jnp.concatenate([w_in[:, :N_TOK_A], kv(0), kv(1), kv(2), kv(4)], axis=1).astype(BF)
    gates = w_in[:, o_gate:]
    pad = jnp.zeros((w_in.shape[0], 32 - gates.shape[1]), w_in.dtype)
    w_t = jnp.concatenate([w_in[:, o_q:o_kv], kv(3), kv(5), gates, pad], axis=1).T.astype(BF)
    return w_tok, w_t


def kernel(x, ffn1_norm, ffn1_w_gate, ffn1_w_up, ffn1_w_down, mix_norm, w_in, w_out, s5_lambda_re, s5_lambda_im, s5_log_dt, s5_b_re, s5_b_im, s5_c_re, s5_c_im, s5_d, s5_w_glu, hgrn_lb_logits, hgrn_norm, nsa_cmp_pos_k, nsa_cmp_w1_k, nsa_cmp_w2_k, nsa_cmp_pos_v, nsa_cmp_w1_v, nsa_cmp_w2_v, rel_bias, ffn2_norm, ffn2_w_gate, ffn2_w_up, ffn2_w_down, final_norm):
    b, l, d = x.shape
    depth = w_in.shape[0]
    gam = jax.nn.softmax(hgrn_lb_logits.astype(F32), axis=0)
    lower_bounds = jnp.cumsum(gam, axis=0) - gam[0:1]
    tz, wz, mc = _bias_tables(rel_bias, l)
    bf = lambda w: w.astype(BF)

    x2 = x.reshape(b * l, d)
    for i in range(depth):
        x2 = _ffn(x2, ffn1_norm[i], bf(ffn1_w_gate[i]), bf(ffn1_w_up[i]), bf(ffn1_w_down[i]))
        w_tok, w_t = _split_w_in(w_in[i])
        za, kcr, vcr, ksw, qt, vt, gt = _inproj(x2.reshape(b, l, d), mix_norm[i], w_tok, w_t)
        bblk, cblk, coef = _s5_params(s5_lambda_re[i], s5_lambda_im[i], s5_log_dt[i], s5_b_re[i], s5_b_im[i],
                                      s5_c_re[i], s5_c_im[i])
        y_s5 = _s5(za, bblk, cblk, coef, s5_d[i], s5_w_glu[i])
        y_hg = _hgrn(za, lower_bounds[i], hgrn_norm[i])
        kc, vct = _compress(kcr, vcr,
                            _compress_params(nsa_cmp_pos_k[i], nsa_cmp_w1_k[i], nsa_cmp_w2_k[i]),
                            _compress_params(nsa_cmp_pos_v[i], nsa_cmp_w1_v[i], nsa_cmp_w2_v[i], transpose_out=True))
        y_nsa = _nsa(qt, gt, kc, vct, ksw, vt, tz, wz, mc)
        proj = (y_s5.reshape(b * l, S5_W), y_hg.reshape(b * l, HG_W), y_nsa.reshape(b * l, NSA_W), bf(w_out[i]))
        x2 = _ffn(x2, ffn2_norm[i], bf(ffn2_w_gate[i]), bf(ffn2_w_up[i]), bf(ffn2_w_down[i]), proj=proj,
                  final_g=final_norm if i + 1 == depth else None)
    return x2.reshape(b, l, d)
```

```python
import functools
import math

import numpy as np
import jax
import jax.numpy as jnp
from jax import lax
from jax.experimental import pallas as pl
from jax.experimental.pallas import tpu as pltpu

BF = jnp.bfloat16
F32 = jnp.float32

EPS = 1e-6
NEG_INF = -1e30
TINY = 1e-30
FORCE_BONUS = 1e4
LOG2E = math.log2(math.e)

S5_W = 256
S5_G = 16
S5_GROUP = 16
S5_P = 64
S5_N = S5_G * S5_P
HG_W = 256
HG_HEADS = 4
HG_D = 64
NSA_W = 512
NSA_DH = 64
NSA_H = 8
NSA_G = 2
NSA_R = 4
KV_W = NSA_G * NSA_DH
CMP_LEN = 32
CMP_STRIDE = 16
SLC_LEN = 64
N_SEL = 16
WIN = 512
REL_BUCKETS = 32
REL_MAX_DIST = 1024

Q_TILE = 128
K_TILE = 128
GR_LANES = NSA_R * Q_TILE
N_TOEPLITZ = REL_MAX_DIST // K_TILE + 2
N_WIN_TILES = WIN // K_TILE + 1
SLAB = 16
HG_CHUNK = 64
HG_LEVELS = 6
SUBLANES = 8
HG_FINE_LEVELS = 3
VMEM_LIMIT_BYTES = 56 * 1024 * 1024


def _cparams(*sem):
    return pltpu.CompilerParams(dimension_semantics=sem, vmem_limit_bytes=VMEM_LIMIT_BYTES)


def _const_spec(shape):
    nd = len(shape)
    return pl.BlockSpec(shape, lambda *_: (0,) * nd)


def _rms(x, g_row):
    ms = jnp.mean(x * x, axis=-1, keepdims=True)
    return x * lax.rsqrt(ms + EPS) * g_row


def _silu(x):
    return x * jax.nn.sigmoid(x)


def _gelu_tanh(x):
    return 0.5 * x * (1.0 + jnp.tanh(math.sqrt(2.0 / math.pi) * (x + 0.044715 * (x * x * x))))


def _dot(a, b):
    return jnp.dot(a, b, preferred_element_type=F32)


def _dot_nt(a, b):
    return lax.dot_general(a, b, (((1,), (1,)), ((), ())), preferred_element_type=F32)


def _dot_tn(a, b):
    return lax.dot_general(a, b, (((0,), (0,)), ((), ())), preferred_element_type=F32)


def _dot_exact_lhs(c_bf, x):
    hi = x.astype(BF)
    r1 = x - hi.astype(F32)
    mid = r1.astype(BF)
    lo = (r1 - mid.astype(F32)).astype(BF)
    return _dot(c_bf, hi) + _dot(c_bf, mid) + _dot(c_bf, lo)


def _dot_exact_rhs(x, c_bf):
    hi = x.astype(BF)
    r1 = x - hi.astype(F32)
    mid = r1.astype(BF)
    lo = (r1 - mid.astype(F32)).astype(BF)
    return _dot(hi, c_bf) + _dot(mid, c_bf) + _dot(lo, c_bf)


def _ffn_kernel(*refs, n_chunks, tf, with_proj, with_final):
    it = iter(refs)
    x_ref = next(it)
    if with_proj:
        ys5_ref, yhg_ref, ynsa_ref, wo_ref = next(it), next(it), next(it), next(it)
    g_ref, wg_ref, wu_ref, wd_ref = next(it), next(it), next(it), next(it)
    if with_final:
        fg_ref = next(it)
    o_ref, h_scr, a_scr = next(it), next(it), next(it)

    x = x_ref[...]
    if with_proj:
        x = (x + _dot(ys5_ref[...].astype(BF), wo_ref[0:S5_W, :])
             + _dot(yhg_ref[...].astype(BF), wo_ref[S5_W:S5_W + HG_W, :])
             + _dot(ynsa_ref[...].astype(BF), wo_ref[S5_W + HG_W:, :]))
    h_scr[...] = _rms(x, g_ref[...]).astype(BF)
    for c in range(n_chunks):
        sl = slice(c * tf, (c + 1) * tf)
        h = h_scr[...]
        gate = _dot(h, wg_ref[:, sl])
        up = _dot(h, wu_ref[:, sl])
        a_scr[:, sl] = (_silu(gate) * up).astype(BF)
    x = x + 0.5 * _dot(a_scr[...], wd_ref[...])
    if with_final:
        x = _rms(x, fg_ref[...])
    o_ref[...] = x


def _ffn(x2, g, wg, wu, wd, proj=None, final_g=None, tm=512, tf=256):
    n, d = x2.shape
    dff = wg.shape[1]
    assert n % tm == 0 and dff % tf == 0
    row = lambda i: (i, 0)
    in_specs = [pl.BlockSpec((tm, d), row)]
    args = [x2]
    if proj is not None:
        ys5, yhg, ynsa, wo = proj
        in_specs += [pl.BlockSpec((tm, S5_W), row), pl.BlockSpec((tm, HG_W), row),
                     pl.BlockSpec((tm, NSA_W), row), _const_spec(wo.shape)]
        args += [ys5, yhg, ynsa, wo]
    in_specs += [_const_spec((1, d)), _const_spec(wg.shape), _const_spec(wu.shape), _const_spec(wd.shape)]
    args += [g.reshape(1, d), wg, wu, wd]
    if final_g is not None:
        in_specs.append(_const_spec((1, d)))
        args.append(final_g.reshape(1, d))
    kern = functools.partial(_ffn_kernel, n_chunks=dff // tf, tf=tf,
                             with_proj=proj is not None, with_final=final_g is not None)
    return pl.pallas_call(
        kern,
        grid=(n // tm,),
        in_specs=in_specs,
        out_specs=pl.BlockSpec((tm, d), row),
        out_shape=jax.ShapeDtypeStruct((n, d), F32),
        scratch_shapes=[pltpu.VMEM((tm, d), BF), pltpu.VMEM((tm, dff), BF)],
        compiler_params=_cparams("arbitrary"),
        name="ffn",
    )(*args)


N_TOK_A = S5_W + 4 * HG_W
N_T_ROWS = NSA_W + 2 * KV_W + 32


def _inproj_kernel(x_ref, g_ref, wtok_ref, wt_ref, za_ref, kcr_ref, vcr_ref, ksw_ref, qt_ref, vt_ref, gt_ref):
    h = _rms(x_ref[0], g_ref[...]).astype(BF)
    for c in range(N_TOK_A // 256):
        sl = slice(c * 256, (c + 1) * 256)
        za_ref[0, :, sl] = _dot(h, wtok_ref[:, sl])
    kcr_ref[0] = _dot(h, wtok_ref[:, N_TOK_A:N_TOK_A + KV_W])
    vcr_ref[0] = _dot(h, wtok_ref[:, N_TOK_A + KV_W:N_TOK_A + 2 * KV_W])
    ksw_ref[0] = _dot(h, wtok_ref[:, N_TOK_A + 2 * KV_W:N_TOK_A + 4 * KV_W]).astype(BF)
    qt_ref[0] = (_dot_nt(wt_ref[0:NSA_W, :], h) * (NSA_DH ** -0.5 * LOG2E)).astype(BF)
    vt_ref[0] = _dot_nt(wt_ref[NSA_W:NSA_W + 2 * KV_W, :], h).astype(BF)
    gt_ref[0] = jax.nn.sigmoid(_dot_nt(wt_ref[NSA_W + 2 * KV_W:, :], h))


def _inproj(x3, g, w_tok, w_t, tm=512):
    b, l, d = x3.shape
    assert l % tm == 0
    tok = lambda w: pl.BlockSpec((1, tm, w), lambda bi, i: (bi, i, 0))
    tr = lambda r: pl.BlockSpec((1, r, tm), lambda bi, i: (bi, 0, i))
    return pl.pallas_call(
        _inproj_kernel,
        grid=(b, l // tm),
        in_specs=[tok(d), _const_spec((1, d)), _const_spec(w_tok.shape), _const_spec(w_t.shape)],
        out_specs=[tok(N_TOK_A), tok(KV_W), tok(KV_W), tok(2 * KV_W), tr(NSA_W), tr(2 * KV_W), tr(32)],
        out_shape=[jax.ShapeDtypeStruct((b, l, N_TOK_A), F32),
                   jax.ShapeDtypeStruct((b, l, KV_W), F32),
                   jax.ShapeDtypeStruct((b, l, KV_W), F32),
                   jax.ShapeDtypeStruct((b, l, 2 * KV_W), BF),
                   jax.ShapeDtypeStruct((b, NSA_W, l), BF),
                   jax.ShapeDtypeStruct((b, 2 * KV_W, l), BF),
                   jax.ShapeDtypeStruct((b, 32, l), F32)],
        compiler_params=_cparams("arbitrary", "arbitrary"),
        name="inproj",
    )(x3, g.reshape(1, d), w_tok, w_t)


def _s5_kernel(u_ref, bblk_ref, cblk_ref, coef_ref, d_ref, wglu_ref, y_ref, xs_scr, carry_scr, *, tc):
    @pl.when(pl.program_id(1) == 0)
    def _():
        carry_scr[...] = jnp.zeros_like(carry_scr)

    u = u_ref[0]
    xs_scr[...] = _dot(u.astype(BF), bblk_ref[...])

    def body(r, carry):
        cre, cim = carry
        row = pl.multiple_of(r * SUBLANES, SUBLANES)
        xre = xs_scr[pl.ds(row, SUBLANES), 0:S5_N]
        xim = xs_scr[pl.ds(row, SUBLANES), S5_N:2 * S5_N]
        for idx, k in enumerate((1, 2, 4)):
            are, aim = coef_ref[idx, 0], coef_ref[idx, 1]
            sre, sim = pltpu.roll(xre, k, 0), pltpu.roll(xim, k, 0)
            xre, xim = xre + (are * sre - aim * sim), xim + (are * sim + aim * sre)
        pre, pim = coef_ref[3, 0], coef_ref[3, 1]
        xre, xim = xre + (pre * cre - pim * cim), xim + (pre * cim + pim * cre)
        xs_scr[pl.ds(row, SUBLANES), 0:S5_N] = xre
        xs_scr[pl.ds(row, SUBLANES), S5_N:2 * S5_N] = xim
        return xre[SUBLANES - 1:SUBLANES], xim[SUBLANES - 1:SUBLANES]

    cre, cim = lax.fori_loop(0, tc // SUBLANES, body, (carry_scr[0:1], carry_scr[1:2]))
    carry_scr[0:1] = cre
    carry_scr[1:2] = cim

    y = _dot(xs_scr[...].astype(BF), cblk_ref[...]) + d_ref[...] * u
    y = _gelu_tanh(y)
    y_ref[0] = y * jax.nn.sigmoid(_dot(y.astype(BF), wglu_ref[...]))


def _s5_params(lam_re, lam_im, log_dt, b_re, b_im, c_re, c_im):
    lr, li = lam_re.astype(F32), lam_im.astype(F32)
    dt = jnp.exp(log_dt.astype(F32))[:, None]
    mag = jnp.exp(lr * dt)
    ab_re, ab_im = mag * jnp.cos(li * dt), mag * jnp.sin(li * dt)
    den = lr * lr + li * li
    nr, ni = ab_re - 1.0, ab_im
    g_re = (nr * lr + ni * li) / den
    g_im = (ni * lr - nr * li) / den
    br, bi = b_re.astype(F32), b_im.astype(F32)
    bb_re = g_re[..., None] * br - g_im[..., None] * bi
    bb_im = g_re[..., None] * bi + g_im[..., None] * br
    eye = jnp.eye(S5_G, dtype=F32)
    blk = lambda w: jnp.einsum('gph,gk->ghkp', w, eye).reshape(S5_W, S5_N)
    bblk = jnp.concatenate([blk(bb_re), blk(bb_im)], axis=1).astype(BF)
    cblk_f = lambda w: jnp.einsum('ghp,gk->gpkh', w, eye).reshape(S5_N, S5_W)
    cblk = jnp.concatenate([cblk_f(c_re.astype(F32)), -cblk_f(c_im.astype(F32))], axis=0).astype(BF)
    are, aim = ab_re.reshape(1, S5_N), ab_im.reshape(1, S5_N)
    pw = [(are, aim)]
    for _ in range(SUBLANES - 1):
        pr, pi = pw[-1]
        pw.append((pr * are - pi * aim, pr * aim + pi * are))
    rows = np.arange(SUBLANES)[:, None]
    coef = []
    for k in (1, 2, 4):
        m = jnp.asarray((rows >= k).astype(np.float32))
        coef.append(jnp.stack([m * pw[k - 1][0], m * pw[k - 1][1]]))
    coef.append(jnp.stack([jnp.concatenate([p[0] for p in pw], axis=0), jnp.concatenate([p[1] for p in pw], axis=0)]))
    return bblk, cblk, jnp.stack(coef)


def _s5(za, bblk, cblk, coef, d, w_glu, tc=512):
    b, l, _ = za.shape
    assert l % tc == 0
    return pl.pallas_call(
        functools.partial(_s5_kernel, tc=tc),
        grid=(b, l // tc),
        in_specs=[pl.BlockSpec((1, tc, S5_W), lambda bi, i: (bi, i, 0)),
                  _const_spec(bblk.shape), _const_spec(cblk.shape), _const_spec(coef.shape),
                  _const_spec((1, S5_W)), _const_spec(w_glu.shape)],
        out_specs=pl.BlockSpec((1, tc, S5_W), lambda bi, i: (bi, i, 0)),
        out_shape=jax.ShapeDtypeStruct((b, l, S5_W), F32),
        scratch_shapes=[pltpu.VMEM((tc, 2 * S5_N), F32), pltpu.VMEM((2, S5_N), F32)],
        compiler_params=_cparams("arbitrary", "arbitrary"),
        name="s5",
    )(za, bblk, cblk, coef, d.reshape(1, S5_W).astype(F32), w_glu.astype(BF))


def _hgrn_constants():
    c = HG_CHUNK
    t = np.arange(c)[:, None]
    u = np.arange(c)[None, :]
    mats = [(u <= t)]
    masks = []
    for lv in range(HG_LEVELS):
        n = c >> lv
        half = n // 2
        ref = (t // n) * n + half - 1
        lower = (t % n) >= half
        if half < SUBLANES:
            mats.append(np.where(lower, (u > ref) & (u <= t), (u > t) & (u <= ref)))
        same = (t // n) == (u // n)
        masks.append(same & lower & ((u % n) < half))
    masks.append(t == u)
    gall = np.concatenate(mats, axis=0).astype(np.float32)
    mstk = np.stack([np.tile(m, (HG_HEADS, 1)) for m in masks]).astype(np.float32)
    lane_head = np.arange(HG_W)[None, :] // HG_D
    hmask = (np.repeat(np.arange(HG_HEADS), c)[:, None] == lane_head).astype(np.float32)
    bd = (np.arange(HG_W)[:, None] // HG_D == lane_head).astype(np.float32)
    return gall, mstk, hmask, bd


def _hgrn_kernel(q_ref, f_ref, i_ref, g_ref, lb_ref, gain_ref, gall_ref, mstk_ref, hmask_ref, bd_ref, bdn_ref,
                 o_ref, st_scr, *, tt):
    c = HG_CHUNK

    @pl.when(pl.program_id(1) == 0)
    def _():
        st_scr[...] = jnp.zeros_like(st_scr)

    lb = lb_ref[...]
    hmask = hmask_ref[...]

    def level_decay(b, sums, lv):
        n = c >> lv
        half = n // 2
        if half < SUBLANES:
            fine = lv - (HG_LEVELS - HG_FINE_LEVELS)
            return jnp.exp(sums[(1 + fine) * c:(2 + fine) * c])
        pieces = []
        for blk in range(c // n):
            ref = blk * n + half - 1
            pieces.append(b[ref:ref + 1] - b[blk * n:blk * n + half])
            pieces.append(b[blk * n + half:(blk + 1) * n] - b[ref:ref + 1])
        return jnp.exp(jnp.concatenate(pieces, axis=0))

    def chunk(row, fl, sums, carry):
        qf = _silu(q_ref[0, pl.ds(row, c), :])
        kf = (1.0 - lb) * jax.nn.sigmoid(-fl)
        v = i_ref[0, pl.ds(row, c), :]
        v_bf = v.astype(BF)
        b = sums[0:c]
        e_b = jnp.exp(b)
        e_suf = jnp.exp(b[c - 1:c] - b)

        att = jnp.zeros((HG_HEADS * c, c), F32)
        for lv in range(HG_LEVELS + 1):
            if lv < HG_LEVELS:
                e = level_decay(b, sums, lv)
                z, w = qf * e, kf * e
            else:
                z, w = qf, kf
            zs = (jnp.concatenate([z] * HG_HEADS, axis=0) * hmask).astype(BF)
            att = att + mstk_ref[lv] * _dot_nt(zs, w.astype(BF))
        o4 = _dot(att.astype(BF), v_bf) * hmask
        o = o4[0:c]
        for h in range(1, HG_HEADS):
            o = o + o4[h * c:(h + 1) * c]

        st = st_scr[...]
        o = o + _dot_nt((qf * e_b).astype(BF), st.astype(BF))
        st_scr[...] = e_b[c - 1:c] * st + bd_ref[...] * _dot_tn(v_bf, (kf * e_suf).astype(BF))

        ms = _dot_exact_rhs(o * o, bdn_ref[...])
        o = o * lax.rsqrt(ms + EPS) * gain_ref[...]
        o_ref[0, pl.ds(row, c), :] = o * _silu(g_ref[0, pl.ds(row, c), :])
        return carry

    def chunk_pair(cp, carry):
        rows = [pl.multiple_of((2 * cp + j) * c, c) for j in range(2)]
        fls = [f_ref[0, pl.ds(r, c), :] for r in rows]
        lfs = [jnp.log(jnp.maximum(lb + (1.0 - lb) * jax.nn.sigmoid(fl), TINY)) for fl in fls]
        sums = _dot_exact_lhs(gall_ref[...], jnp.concatenate(lfs, axis=1))
        for j in range(2):
            carry = chunk(rows[j], fls[j], sums[:, j * HG_W:(j + 1) * HG_W], carry)
        return carry

    lax.fori_loop(0, tt // (2 * c), chunk_pair, 0)


def _hgrn(za, lb, gain, tt=512):
    b, l, _ = za.shape
    assert l % tt == 0
    gall, mstk, hmask, bd = _hgrn_constants()
    col = lambda j: pl.BlockSpec((1, tt, HG_W), lambda bi, i: (bi, i, j))
    consts = [jnp.asarray(gall, BF), jnp.asarray(mstk, F32), jnp.asarray(hmask, F32), jnp.asarray(bd, F32),
              jnp.asarray(bd / HG_D, BF)]
    return pl.pallas_call(
        functools.partial(_hgrn_kernel, tt=tt),
        grid=(b, l // tt),
        in_specs=[col(1), col(2), col(3), col(4), _const_spec((1, HG_W)), _const_spec((1, HG_W))]
                 + [_const_spec(x.shape) for x in consts],
        out_specs=pl.BlockSpec((1, tt, HG_W), lambda bi, i: (bi, i, 0)),
        out_shape=jax.ShapeDtypeStruct((b, l, HG_W), F32),
        scratch_shapes=[pltpu.VMEM((HG_W, HG_W), F32)],
        compiler_params=_cparams("arbitrary", "arbitrary"),
        name="hgrn2",
    )(za, za, za, za, lb.reshape(1, HG_W).astype(F32), jnp.tile(gain.astype(F32), HG_HEADS).reshape(1, HG_W),
      *consts)


def _compress_kernel(k16_ref, v16_ref, kpa_ref, kpb_ref, kw1a_ref, kw1b_ref, kw2_ref,
                     vpa_ref, vpb_ref, vw1a_ref, vw1b_ref, vw2_ref, kc_ref, vct_ref, *, nc):
    def hidden(x16, pa, pb, w1a, w1b):
        first = _dot((x16 + pa).astype(BF), w1a)
        second = _dot((x16 + pb).astype(BF), w1b)
        pre = first + pltpu.roll(second, nc - 1, 0)
        rows = lax.broadcasted_iota(jnp.int32, pre.shape, 0)
        pre = jnp.where(rows < nc - 1, pre, 0.0)
        return _gelu_tanh(pre).astype(BF)

    hk = hidden(k16_ref[0], kpa_ref[...], kpb_ref[...], kw1a_ref[...], kw1b_ref[...])
    kc_ref[0] = _dot(hk, kw2_ref[...]).astype(BF)
    hv = hidden(v16_ref[0], vpa_ref[...], vpb_ref[...], vw1a_ref[...], vw1b_ref[...])
    vct_ref[0] = _dot_nt(vw2_ref[...], hv).astype(BF)


def _compress_params(pos, w1, w2, transpose_out=False):
    eye = jnp.eye(NSA_G, dtype=F32)
    w1r = w1.astype(F32).reshape(CMP_LEN, NSA_DH, NSA_DH)
    wexp = jnp.einsum('jde,gh->jgdhe', w1r, eye).reshape(CMP_LEN, KV_W, KV_W)
    half = CMP_LEN // 2
    w1a = wexp[:half].reshape(half * KV_W, KV_W).astype(BF)
    w1b = wexp[half:].reshape(half * KV_W, KV_W).astype(BF)
    pt = jnp.broadcast_to(pos.astype(F32)[:, None, :], (CMP_LEN, NSA_G, NSA_DH))
    pa = pt[:half].reshape(1, half * KV_W)
    pb = pt[half:].reshape(1, half * KV_W)
    w2bd = jnp.einsum('de,gh->gdhe', w2.astype(F32), eye).reshape(KV_W, KV_W).astype(BF)
    return pa, pb, w1a, w1b, (w2bd.T if transpose_out else w2bd)


def _compress(kcr, vcr, kparams, vparams):
    b, l, _ = kcr.shape
    nc = l // CMP_STRIDE
    w = CMP_STRIDE * KV_W
    k16 = kcr.reshape(b, nc, w)
    v16 = vcr.reshape(b, nc, w)
    per_b = lambda r, cc: pl.BlockSpec((1, r, cc), lambda bi: (bi, 0, 0))
    params = list(kparams) + list(vparams)
    return pl.pallas_call(
        functools.partial(_compress_kernel, nc=nc),
        grid=(b,),
        in_specs=[per_b(nc, w), per_b(nc, w)] + [_const_spec(p.shape) for p in params],
        out_specs=[per_b(nc, KV_W), per_b(KV_W, nc)],
        out_shape=[jax.ShapeDtypeStruct((b, nc, KV_W), BF), jax.ShapeDtypeStruct((b, KV_W, nc), BF)],
        compiler_params=_cparams("arbitrary"),
        name="compress",
    )(k16, v16, *params)


def _t5_bucket(dist):
    n = jnp.maximum(dist, 0)
    max_exact = REL_BUCKETS // 2
    nf = jnp.maximum(n, max_exact).astype(jnp.float32)
    large = max_exact + (jnp.log(nf / max_exact) / math.log(REL_MAX_DIST / max_exact)
                         * (REL_BUCKETS - max_exact)).astype(jnp.int32)
    large = jnp.minimum(large, REL_BUCKETS - 1)
    return jnp.where(n < max_exact, n, large)


def _bias_tables(rel_bias, l):
    tab = rel_bias.astype(F32) * LOG2E
    by_dist = tab[_t5_bucket(jnp.arange(l))].T
    nc = l // CMP_STRIDE
    w = 2 * K_TILE

    def shifted_rows(v, n_rows, step):
        flat = jnp.tile(v, (1,) * (v.ndim - 1) + (n_rows,))[..., :n_rows * (w - step)]
        return flat.reshape(v.shape[:-1] + (n_rows, w - step))[..., :Q_TILE]

    def group_layout(t, valid):
        _, n, r, q = t.shape
        t = jnp.where(jnp.asarray(valid)[None], t, NEG_INF)
        return t.reshape(NSA_G, NSA_R, n, r, q).transpose(0, 2, 3, 1, 4).reshape(NSA_G, n, r, NSA_R * q)

    tq = np.arange(Q_TILE)[None, None, :]
    key = np.arange(K_TILE)[None, :, None]
    blocks = jnp.pad(by_dist, ((0, 0), (K_TILE, w)), mode='edge').reshape(NSA_H, -1, K_TILE)
    vec = jnp.concatenate([blocks[:, 1:N_TOEPLITZ + 1], blocks[:, 0:N_TOEPLITZ]], axis=-1)
    toep = shifted_rows(vec, K_TILE, 1)
    masked_tile = jnp.full((NSA_G, 1, K_TILE, GR_LANES), NEG_INF, F32)
    d = np.arange(N_TOEPLITZ)[:, None, None] * K_TILE + tq - key
    tz = jnp.concatenate([masked_tile, group_layout(toep, d >= 0)], axis=1)
    d = (N_WIN_TILES - 1 - np.arange(N_WIN_TILES))[:, None, None] * K_TILE + tq - key
    wz = group_layout(toep[:, N_WIN_TILES - 1::-1], (d >= 0) & (d < WIN))
    wz = jnp.concatenate([masked_tile, wz, masked_tile], axis=1)
    na = (2 * nc - SUBLANES) // SUBLANES
    c0 = CMP_STRIDE * (nc - SUBLANES) - (CMP_LEN - 1)
    front = K_TILE * (na + 1)
    off = c0 % K_TILE
    padded = jnp.pad(by_dist, ((0, 0), (front, w)), mode='edge')
    nb = (padded.shape[1] - off) // K_TILE
    blocks = padded[:, off:off + nb * K_TILE].reshape(NSA_H, nb, K_TILE)
    k0 = (front + c0 - off) // K_TILE
    vec = jnp.concatenate([blocks[:, k0 - na + 1:k0 + 1][:, ::-1], blocks[:, k0 - na:k0][:, ::-1]], axis=-1)
    cmp_rows = shifted_rows(vec, SUBLANES, CMP_STRIDE).reshape(NSA_H, 1, na * SUBLANES, Q_TILE)
    u = np.arange(na * SUBLANES)[None, :, None]
    mc = group_layout(cmp_rows, tq - CMP_STRIDE * (u - (nc - SUBLANES)) - (CMP_LEN - 1) >= 0)[:, 0]
    return tz, wz, mc


def _nsa_kernel(qt_ref, gt_ref, kc_ref, vct_ref, ksl_ref, kwn_ref, vslt_ref, vwnt_ref, mc_ref, tz_ref, wz_ref,
                ind_ref, expand_ref, o_ref, imp_scr, y_scr, *group_scr, nc, nsb):
    per_group = len(group_scr) // NSA_G
    qi = pl.program_id(1)
    q0 = qi * Q_TILE
    n_sel = min(N_SEL, nsb)
    cmp_per_q = Q_TILE // CMP_STRIDE
    cmp_per_slc = SLC_LEN // CMP_STRIDE
    pair = 2 * K_TILE
    n_last = qi // 2
    ones_rows = (lax.broadcasted_iota(jnp.int32, (SLAB, pair), 0) == 0).astype(BF)

    def lanes4(rows):
        return jnp.concatenate(rows, axis=1)

    def group_rows(g):
        return slice(g * NSA_DH, (g + 1) * NSA_DH)

    def scores(g, kp, selected):
        qt_scr, slab_scr = group_scr[g * per_group:g * per_group + 2]
        k0 = pl.multiple_of(kp * pair, pair)
        if selected:
            lhs = jnp.concatenate([ksl_ref[0, pl.ds(k0, pair), :], ind_ref[...]], axis=1)
            slab = lanes4([slab_scr[pl.ds(pl.multiple_of(kp * SLAB, SLAB), SLAB), :]] * NSA_R)
            rhs = jnp.concatenate([qt_scr[...], slab, jnp.zeros((KV_W - SLAB, GR_LANES), BF)], axis=0)
            bias = [tz_ref[g, jnp.clip(qi - (2 * kp + j), -1, N_TOEPLITZ - 1) + 1] for j in range(2)]
        else:
            lhs, rhs = kwn_ref[0, pl.ds(k0, pair), :], qt_scr[...]
            bias = [wz_ref[g, 2 * kp + j - (qi - (N_WIN_TILES - 1)) + 1] for j in range(2)]
        s = _dot(lhs, rhs) + jnp.concatenate(bias, axis=0)
        return s, jnp.max(s, axis=0, keepdims=True)

    def accumulate(g, kp, vt_ref, s, s_max):
        m_scr, acc_scr = group_scr[g * per_group + 2:g * per_group + 4]
        k0 = pl.multiple_of(kp * pair, pair)
        m_prev = m_scr[...]
        m_new = jnp.maximum(m_prev, s_max)
        p = jnp.exp2(s - m_new).astype(BF)
        vt = jnp.concatenate([vt_ref[0, group_rows(g), pl.ds(k0, pair)], ones_rows], axis=0)
        acc_scr[...] = jnp.exp2(m_prev - m_new) * acc_scr[...] + _dot(vt, p)
        m_scr[...] = m_new

    def sweep(first, selected, vt_ref):
        def fill(kp, buf):
            for g in range(NSA_G):
                s, s_max = scores(g, kp, selected)
                group_scr[g * per_group + 6 + buf][...] = s
                group_scr[g * per_group + 8 + buf][...] = s_max

        def drain(kp, buf):
            for g in range(NSA_G):
                accumulate(g, kp, vt_ref, group_scr[g * per_group + 6 + buf][...],
                           group_scr[g * per_group + 8 + buf][...])

        fill(first, 0)

        def body(j, carry):
            kp = first + 2 * j
            fill(kp + 1, 1)
            drain(kp, 0)
            fill(kp + 2, 0)
            drain(kp + 1, 1)
            return carry

        remaining = n_last - first
        lax.fori_loop(0, remaining // 2, body, 0)

        @pl.when(remaining % 2 == 1)
        def _():
            fill(n_last, 1)
            drain(n_last - 1, 0)
            drain(n_last, 1)

        @pl.when(remaining % 2 == 0)
        def _():
            drain(n_last, 0)

    def reset(g):
        m_scr, acc_scr = group_scr[g * per_group + 2:g * per_group + 4]
        m_scr[...] = jnp.full_like(m_scr, NEG_INF)
        acc_scr[...] = jnp.zeros_like(acc_scr)

    def result(g):
        acc_scr = group_scr[g * per_group + 3]
        return acc_scr[0:NSA_DH] * (1.0 / acc_scr[NSA_DH:NSA_DH + 1])

    for g in range(NSA_G):
        qt_scr, slab_scr = group_scr[g * per_group], group_scr[g * per_group + 1]
        ocmp_scr = group_scr[g * per_group + 4]
        zeros = jnp.zeros((NSA_DH, Q_TILE), BF)
        parts = []
        for r in range(NSA_R):
            h = g * NSA_R + r
            qh = qt_ref[0, h * NSA_DH:(h + 1) * NSA_DH, :]
            parts.append(jnp.concatenate([qh, zeros] if g == 0 else [zeros, qh], axis=0))
        qt = lanes4(parts)
        qt_scr[...] = qt

        bias_c = mc_ref[g, pl.ds(pl.multiple_of((nc - SUBLANES) - cmp_per_q * qi, SUBLANES), nc), :]
        s = _dot(kc_ref[0], qt) + bias_c
        m = jnp.max(s, axis=0, keepdims=True)
        e = jnp.exp2(s - m)
        lsum = jnp.sum(e, axis=0, keepdims=True)
        tq4 = q0 + lax.broadcasted_iota(jnp.int32, (1, GR_LANES), 1) % Q_TILE
        inv = jnp.where(tq4 >= CMP_LEN - 1, 1.0 / jnp.maximum(lsum, TINY), 0.0)
        p = e * inv
        ocmp_scr[...] = _dot(vct_ref[0, group_rows(g), :], p.astype(BF))
        imp = p[:, 0:Q_TILE]
        for r in range(1, NSA_R):
            imp = imp + p[:, r * Q_TILE:(r + 1) * Q_TILE]

        imp_scr[0:SUBLANES, :] = jnp.zeros((SUBLANES, Q_TILE), F32)
        imp_scr[SUBLANES:SUBLANES + nc, :] = imp
        a = [imp_scr[pl.ds(SUBLANES - 1 + k, nsb, stride=cmp_per_slc), :] for k in range(cmp_per_slc + 1)]
        p_slc = a[0] + a[1]
        for k in range(1, cmp_per_slc):
            p_slc = p_slc + a[k] + a[k + 1]

        jidx = lax.broadcasted_iota(jnp.int32, (nsb, Q_TILE), 0)
        tq = q0 + lax.broadcasted_iota(jnp.int32, (nsb, Q_TILE), 1)
        cur = tq // SLC_LEN
        forced = (jidx == 0) | (jidx == cur) | (jidx == cur - 1)
        score = jnp.where(jidx * SLC_LEN <= tq, p_slc + jnp.where(forced, FORCE_BONUS, 0.0), NEG_INF)
        for _ in range(n_sel):
            mx = jnp.max(score, axis=0, keepdims=True)
            first = jnp.min(jnp.where(score == mx, jidx, nsb), axis=0, keepdims=True)
            score = jnp.where(jidx == first, -jnp.inf, score)
        selb = jnp.where(score == -jnp.inf, 0.0, NEG_INF)
        slab_scr[...] = _dot(expand_ref[...], selb.astype(BF)).astype(BF)
        reset(g)

    sweep(jnp.int32(0), True, vslt_ref)
    for g in range(NSA_G):
        group_scr[g * per_group + 5][...] = result(g)
        reset(g)

    sweep(jnp.maximum(qi - (N_WIN_TILES - 1), 0) // 2, False, vwnt_ref)

    for g in range(NSA_G):
        def gate(branch):
            return lanes4([gt_ref[0, pl.ds((g * NSA_R + r) * 3 + branch, 1), :] for r in range(NSA_R)])

        yt = (gate(0) * group_scr[g * per_group + 4][...] + gate(1) * group_scr[g * per_group + 5][...]
              + gate(2) * result(g))
        for r in range(NSA_R):
            h = g * NSA_R + r
            y_scr[h * NSA_DH:(h + 1) * NSA_DH, :] = yt[:, r * Q_TILE:(r + 1) * Q_TILE]

    o_ref[0] = y_scr[...].T


def _nsa(qt, gt, kc, vct, ksw, vt, tz, wz, mc):
    b, _, l = qt.shape
    nc = l // CMP_STRIDE
    nsb = l // SLC_LEN
    per_b = lambda shape, idx: pl.BlockSpec((1,) + shape, lambda bi, i: (bi,) + idx)
    pair = 2 * K_TILE
    blocks_per_pair = pair // SLC_LEN
    n_pairs = l // pair
    ind = (np.arange(pair)[:, None] // SLC_LEN == np.arange(KV_W)[None, :]).astype(np.float32)
    expand = np.zeros((n_pairs * SLAB, nsb), np.float32)
    for k in range(n_pairs):
        for j in range(blocks_per_pair):
            expand[k * SLAB + j, k * blocks_per_pair + j] = 1.0
    ind, expand = jnp.asarray(ind, BF), jnp.asarray(expand, BF)
    return pl.pallas_call(
        functools.partial(_nsa_kernel, nc=nc, nsb=nsb),
        grid=(b, l // Q_TILE),
        in_specs=[pl.BlockSpec((1, NSA_W, Q_TILE), lambda bi, i: (bi, 0, i)),
                  pl.BlockSpec((1, 32, Q_TILE), lambda bi, i: (bi, 0, i)),
                  per_b((nc, KV_W), (0, 0)), per_b((KV_W, nc), (0, 0)),
                  per_b((l, KV_W), (0, 0)), per_b((l, KV_W), (0, 1)),
                  per_b((KV_W, l), (0, 0)), per_b((KV_W, l), (1, 0)),
                  _const_spec(mc.shape), _const_spec(tz.shape), _const_spec(wz.shape),
                  _const_spec(ind.shape), _const_spec(expand.shape)],
        out_specs=pl.BlockSpec((1, Q_TILE, NSA_W), lambda bi, i: (bi, i, 0)),
        out_shape=jax.ShapeDtypeStruct((b, l, NSA_W), F32),
        scratch_shapes=[pltpu.VMEM((nc + SUBLANES, Q_TILE), F32), pltpu.VMEM((NSA_W, Q_TILE), F32)]
                       + NSA_G * [pltpu.VMEM((KV_W, GR_LANES), BF), pltpu.VMEM((n_pairs * SLAB, Q_TILE), BF),
                                  pltpu.VMEM((1, GR_LANES), F32), pltpu.VMEM((NSA_DH + SLAB, GR_LANES), F32),
                                  pltpu.VMEM((NSA_DH, GR_LANES), F32), pltpu.VMEM((NSA_DH, GR_LANES), F32),
                                  pltpu.VMEM((pair, GR_LANES), F32), pltpu.VMEM((pair, GR_LANES), F32),
                                  pltpu.VMEM((1, GR_LANES), F32), pltpu.VMEM((1, GR_LANES), F32)],
        compiler_params=_cparams("arbitrary", "arbitrary"),
        name="nsa",
    )(qt, gt, kc, vct, ksw, ksw, vt, vt, mc, tz, wz, ind, expand)


def _split_w_in(w_in):
    o_q = N_TOK_A
    o_kv = o_q + NSA_W
    o_gate = o_kv + 6 * KV_W
    kv = lambda j: w_in[:, o_kv + j * KV_W:o_kv + (j + 1) * KV_W]
    w_tok = jnp.concatenate([w_in[:, :N_TOK_A], kv(0), kv(1), kv(2), kv(4)], axis=1).astype(BF)
    gates = w_in[:, o_gate:]
    pad = jnp.zeros((w_in.shape[0], 32 - gates.shape[1]), w_in.dtype)
    w_t = jnp.concatenate([w_in[:, o_q:o_kv], kv(3), kv(5), gates, pad], axis=1).T.astype(BF)
    return w_tok, w_t


def kernel(x, ffn1_norm, ffn1_w_gate, ffn1_w_up, ffn1_w_down, mix_norm, w_in, w_out, s5_lambda_re, s5_lambda_im, s5_log_dt, s5_b_re, s5_b_im, s5_c_re, s5_c_im, s5_d, s5_w_glu, hgrn_lb_logits, hgrn_norm, nsa_cmp_pos_k, nsa_cmp_w1_k, nsa_cmp_w2_k, nsa_cmp_pos_v, nsa_cmp_w1_v, nsa_cmp_w2_v, rel_bias, ffn2_norm, ffn2_w_gate, ffn2_w_up, ffn2_w_down, final_norm):
    b, l, d = x.shape
    depth = w_in.shape[0]
    gam = jax.nn.softmax(hgrn_lb_logits.astype(F32), axis=0)
    lower_bounds = jnp.cumsum(gam, axis=0) - gam[0:1]
    tz, wz, mc = _bias_tables(rel_bias, l)
    bf = lambda w: w.astype(BF)

    x2 = x.reshape(b * l, d)
    for i in range(depth):
        x2 = _ffn(x2, ffn1_norm[i], bf(ffn1_w_gate[i]), bf(ffn1_w_up[i]), bf(ffn1_w_down[i]))
        w_tok, w_t = _split_w_in(w_in[i])
        za, kcr, vcr, ksw, qt, vt, gt = _inproj(x2.reshape(b, l, d), mix_norm[i], w_tok, w_t)
        bblk, cblk, coef = _s5_params(s5_lambda_re[i], s5_lambda_im[i], s5_log_dt[i], s5_b_re[i], s5_b_im[i],
                                      s5_c_re[i], s5_c_im[i])
        y_s5 = _s5(za, bblk, cblk, coef, s5_d[i], s5_w_glu[i])
        y_hg = _hgrn(za, lower_bounds[i], hgrn_norm[i])
        kc, vct = _compress(kcr, vcr,
                            _compress_params(nsa_cmp_pos_k[i], nsa_cmp_w1_k[i], nsa_cmp_w2_k[i]),
                            _compress_params(nsa_cmp_pos_v[i], nsa_cmp_w1_v[i], nsa_cmp_w2_v[i], transpose_out=True))
        y_nsa = _nsa(qt, gt, kc, vct, ksw, vt, tz, wz, mc)
        proj = (y_s5.reshape(b * l, S5_W), y_hg.reshape(b * l, HG_W), y_nsa.reshape(b * l, NSA_W), bf(w_out[i]))
        x2 = _ffn(x2, ffn2_norm[i], bf(ffn2_w_gate[i]), bf(ffn2_w_up[i]), bf(ffn2_w_down[i]), proj=proj,
                  final_g=final_norm if i + 1 == depth else None)
    return x2.reshape(b, l, d)
```

```python
import functools
import math

import numpy as np
import jax
import jax.numpy as jnp
from jax import lax
from jax.experimental import pallas as pl
from jax.experimental.pallas import tpu as pltpu

BF = jnp.bfloat16
F32 = jnp.float32

EPS = 1e-6
NEG_INF = -1e30
TINY = 1e-30
LOG2E = math.log2(math.e)

S5_W = 256
S5_G = 16
S5_GROUP = 16
S5_P = 64
S5_N = S5_G * S5_P
HG_W = 256
HG_HEADS = 4
HG_D = 64
NSA_W = 512
NSA_DH = 64
NSA_H = 8
NSA_G = 2
NSA_R = 4
KV_W = NSA_G * NSA_DH
CMP_LEN = 32
CMP_STRIDE = 16
SLC_LEN = 64
N_SEL = 16
WIN = 512
REL_BUCKETS = 32
REL_MAX_DIST = 1024

Q_TILE = 128
K_TILE = 128
GR_LANES = NSA_R * Q_TILE
N_TOEPLITZ = REL_MAX_DIST // K_TILE + 2
N_WIN_TILES = WIN // K_TILE + 1
SLAB = 16
HG_CHUNK = 64
HG_LEVELS = 6
SUBLANES = 8
HG_FINE_LEVELS = 3
VMEM_LIMIT_BYTES = 56 * 1024 * 1024


def _cparams(*sem):
    return pltpu.CompilerParams(dimension_semantics=sem, vmem_limit_bytes=VMEM_LIMIT_BYTES)


def _const_spec(shape):
    nd = len(shape)
    return pl.BlockSpec(shape, lambda *_: (0,) * nd)


def _rms(x, g_row):
    ms = jnp.mean(x * x, axis=-1, keepdims=True)
    return x * lax.rsqrt(ms + EPS) * g_row


def _silu(x):
    return x * jax.nn.sigmoid(x)


def _gelu_tanh(x):
    return 0.5 * x * (1.0 + jnp.tanh(math.sqrt(2.0 / math.pi) * (x + 0.044715 * (x * x * x))))


def _dot(a, b):
    return jnp.dot(a, b, preferred_element_type=F32)


def _dot_nt(a, b):
    return lax.dot_general(a, b, (((1,), (1,)), ((), ())), preferred_element_type=F32)


def _dot_tn(a, b):
    return lax.dot_general(a, b, (((0,), (0,)), ((), ())), preferred_element_type=F32)


def _dot_exact_lhs(c_bf, x):
    hi = x.astype(BF)
    r1 = x - hi.astype(F32)
    mid = r1.astype(BF)
    lo = (r1 - mid.astype(F32)).astype(BF)
    return _dot(c_bf, hi) + _dot(c_bf, mid) + _dot(c_bf, lo)


def _dot_exact_rhs(x, c_bf):
    hi = x.astype(BF)
    r1 = x - hi.astype(F32)
    mid = r1.astype(BF)
    lo = (r1 - mid.astype(F32)).astype(BF)
    return _dot(hi, c_bf) + _dot(mid, c_bf) + _dot(lo, c_bf)


def _ffn_kernel(*refs, n_chunks, tf, with_proj, with_final):
    it = iter(refs)
    x_ref = next(it)
    if with_proj:
        ys5_ref, yhg_ref, ynsa_ref, wo_ref = next(it), next(it), next(it), next(it)
    g_ref, wg_ref, wu_ref, wd_ref = next(it), next(it), next(it), next(it)
    if with_final:
        fg_ref = next(it)
    o_ref, h_scr, a_scr = next(it), next(it), next(it)

    x = x_ref[...]
    if with_proj:
        x = (x + _dot(ys5_ref[...].astype(BF), wo_ref[0:S5_W, :])
             + _dot(yhg_ref[...].astype(BF), wo_ref[S5_W:S5_W + HG_W, :])
             + _dot(ynsa_ref[...].astype(BF), wo_ref[S5_W + HG_W:, :]))
    h_scr[...] = _rms(x, g_ref[...]).astype(BF)
    for c in range(n_chunks):
        sl = slice(c * tf, (c + 1) * tf)
        h = h_scr[...]
        gate = _dot(h, wg_ref[:, sl])
        up = _dot(h, wu_ref[:, sl])
        a_scr[:, sl] = (_silu(gate) * up).astype(BF)
    x = x + 0.5 * _dot(a_scr[...], wd_ref[...])
    if with_final:
        x = _rms(x, fg_ref[...])
    o_ref[...] = x


def _ffn(x2, g, wg, wu, wd, proj=None, final_g=None, tm=512, tf=256):
    n, d = x2.shape
    dff = wg.shape[1]
    assert n % tm == 0 and dff % tf == 0
    row = lambda i: (i, 0)
    in_specs = [pl.BlockSpec((tm, d), row)]
    args = [x2]
    if proj is not None:
        ys5, yhg, ynsa, wo = proj
        in_specs += [pl.BlockSpec((tm, S5_W), row), pl.BlockSpec((tm, HG_W), row),
                     pl.BlockSpec((tm, NSA_W), row), _const_spec(wo.shape)]
        args += [ys5, yhg, ynsa, wo]
    in_specs += [_const_spec((1, d)), _const_spec(wg.shape), _const_spec(wu.shape), _const_spec(wd.shape)]
    args += [g.reshape(1, d), wg, wu, wd]
    if final_g is not None:
        in_specs.append(_const_spec((1, d)))
        args.append(final_g.reshape(1, d))
    kern = functools.partial(_ffn_kernel, n_chunks=dff // tf, tf=tf,
                             with_proj=proj is not None, with_final=final_g is not None)
    return pl.pallas_call(
        kern,
        grid=(n // tm,),
        in_specs=in_specs,
        out_specs=pl.BlockSpec((tm, d), row),
        out_shape=jax.ShapeDtypeStruct((n, d), F32),
        scratch_shapes=[pltpu.VMEM((tm, d), BF), pltpu.VMEM((tm, dff), BF)],
        compiler_params=_cparams("arbitrary"),
        name="ffn",
    )(*args)


N_TOK_A = S5_W + 4 * HG_W
N_T_ROWS = NSA_W + 2 * KV_W + 32


def _inproj_kernel(x_ref, g_ref, wtok_ref, wt_ref, za_ref, kcr_ref, vcr_ref, ksw_ref, qt_ref, vt_ref, gt_ref):
    h = _rms(x_ref[0], g_ref[...]).astype(BF)
    for c in range(N_TOK_A // 256):
        sl = slice(c * 256, (c + 1) * 256)
        za_ref[0, :, sl] = _dot(h, wtok_ref[:, sl])
    kcr_ref[0] = _dot(h, wtok_ref[:, N_TOK_A:N_TOK_A + KV_W])
    vcr_ref[0] = _dot(h, wtok_ref[:, N_TOK_A + KV_W:N_TOK_A + 2 * KV_W])
    ksw_ref[0] = _dot(h, wtok_ref[:, N_TOK_A + 2 * KV_W:N_TOK_A + 4 * KV_W]).astype(BF)
    qt_ref[0] = (_dot_nt(wt_ref[0:NSA_W, :], h) * (NSA_DH ** -0.5 * LOG2E)).astype(BF)
    vt_ref[0] = _dot_nt(wt_ref[NSA_W:NSA_W + 2 * KV_W, :], h).astype(BF)
    gt_ref[0] = jax.nn.sigmoid(_dot_nt(wt_ref[NSA_W + 2 * KV_W:, :], h))


def _inproj(x3, g, w_tok, w_t, tm=512):
    b, l, d = x3.shape
    assert l % tm == 0
    tok = lambda w: pl.BlockSpec((1, tm, w), lambda bi, i: (bi, i, 0))
    tr = lambda r: pl.BlockSpec((1, r, tm), lambda bi, i: (bi, 0, i))
    return pl.pallas_call(
        _inproj_kernel,
        grid=(b, l // tm),
        in_specs=[tok(d), _const_spec((1, d)), _const_spec(w_tok.shape), _const_spec(w_t.shape)],
        out_specs=[tok(N_TOK_A), tok(KV_W), tok(KV_W), tok(2 * KV_W), tr(NSA_W), tr(2 * KV_W), tr(32)],
        out_shape=[jax.ShapeDtypeStruct((b, l, N_TOK_A), F32),
                   jax.ShapeDtypeStruct((b, l, KV_W), F32),
                   jax.ShapeDtypeStruct((b, l, KV_W), F32),
                   jax.ShapeDtypeStruct((b, l, 2 * KV_W), BF),
                   jax.ShapeDtypeStruct((b, NSA_W, l), BF),
                   jax.ShapeDtypeStruct((b, 2 * KV_W, l), BF),
                   jax.ShapeDtypeStruct((b, 32, l), F32)],
        compiler_params=_cparams("arbitrary", "arbitrary"),
        name="inproj",
    )(x3, g.reshape(1, d), w_tok, w_t)


def _s5_kernel(u_ref, bblk_ref, cblk_ref, coef_ref, d_ref, wglu_ref, y_ref, xs_scr, carry_scr, *, tc):
    @pl.when(pl.program_id(1) == 0)
    def _():
        carry_scr[...] = jnp.zeros_like(carry_scr)

    u = u_ref[0]
    xs_scr[...] = _dot(u.astype(BF), bblk_ref[...])

    def body(r, carry):
        cre, cim = carry
        row = pl.multiple_of(r * SUBLANES, SUBLANES)
        xre = xs_scr[pl.ds(row, SUBLANES), 0:S5_N]
        xim = xs_scr[pl.ds(row, SUBLANES), S5_N:2 * S5_N]
        for idx, k in enumerate((1, 2, 4)):
            are, aim = coef_ref[idx, 0], coef_ref[idx, 1]
            sre, sim = pltpu.roll(xre, k, 0), pltpu.roll(xim, k, 0)
            xre, xim = xre + (are * sre - aim * sim), xim + (are * sim + aim * sre)
        pre, pim = coef_ref[3, 0], coef_ref[3, 1]
        xre, xim = xre + (pre * cre - pim * cim), xim + (pre * cim + pim * cre)
        xs_scr[pl.ds(row, SUBLANES), 0:S5_N] = xre
        xs_scr[pl.ds(row, SUBLANES), S5_N:2 * S5_N] = xim
        return xre[SUBLANES - 1:SUBLANES], xim[SUBLANES - 1:SUBLANES]

    cre, cim = lax.fori_loop(0, tc // SUBLANES, body, (carry_scr[0:1], carry_scr[1:2]))
    carry_scr[0:1] = cre
    carry_scr[1:2] = cim

    y = _dot(xs_scr[...].astype(BF), cblk_ref[...]) + d_ref[...] * u
    y = _gelu_tanh(y)
    y_ref[0] = y * jax.nn.sigmoid(_dot(y.astype(BF), wglu_ref[...]))


def _s5_params(lam_re, lam_im, log_dt, b_re, b_im, c_re, c_im):
    lr, li = lam_re.astype(F32), lam_im.astype(F32)
    dt = jnp.exp(log_dt.astype(F32))[:, None]
    mag = jnp.exp(lr * dt)
    ab_re, ab_im = mag * jnp.cos(li * dt), mag * jnp.sin(li * dt)
    den = lr * lr + li * li
    nr, ni = ab_re - 1.0, ab_im
    g_re = (nr * lr + ni * li) / den
    g_im = (ni * lr - nr * li) / den
    br, bi = b_re.astype(F32), b_im.astype(F32)
    bb_re = g_re[..., None] * br - g_im[..., None] * bi
    bb_im = g_re[..., None] * bi + g_im[..., None] * br
    eye = jnp.eye(S5_G, dtype=F32)
    blk = lambda w: jnp.einsum('gph,gk->ghkp', w, eye).reshape(S5_W, S5_N)
    bblk = jnp.concatenate([blk(bb_re), blk(bb_im)], axis=1).astype(BF)
    cblk_f = lambda w: jnp.einsum('ghp,gk->gpkh', w, eye).reshape(S5_N, S5_W)
    cblk = jnp.concatenate([cblk_f(c_re.astype(F32)), -cblk_f(c_im.astype(F32))], axis=0).astype(BF)
    are, aim = ab_re.reshape(1, S5_N), ab_im.reshape(1, S5_N)
    pw = [(are, aim)]
    for _ in range(SUBLANES - 1):
        pr, pi = pw[-1]
        pw.append((pr * are - pi * aim, pr * aim + pi * are))
    rows = np.arange(SUBLANES)[:, None]
    coef = []
    for k in (1, 2, 4):
        m = jnp.asarray((rows >= k).astype(np.float32))
        coef.append(jnp.stack([m * pw[k - 1][0], m * pw[k - 1][1]]))
    coef.append(jnp.stack([jnp.concatenate([p[0] for p in pw], axis=0), jnp.concatenate([p[1] for p in pw], axis=0)]))
    return bblk, cblk, jnp.stack(coef)


def _s5(za, bblk, cblk, coef, d, w_glu, tc=512):
    b, l, _ = za.shape
    assert l % tc == 0
    return pl.pallas_call(
        functools.partial(_s5_kernel, tc=tc),
        grid=(b, l // tc),
        in_specs=[pl.BlockSpec((1, tc, S5_W), lambda bi, i: (bi, i, 0)),
                  _const_spec(bblk.shape), _const_spec(cblk.shape), _const_spec(coef.shape),
                  _const_spec((1, S5_W)), _const_spec(w_glu.shape)],
        out_specs=pl.BlockSpec((1, tc, S5_W), lambda bi, i: (bi, i, 0)),
        out_shape=jax.ShapeDtypeStruct((b, l, S5_W), F32),
        scratch_shapes=[pltpu.VMEM((tc, 2 * S5_N), F32), pltpu.VMEM((2, S5_N), F32)],
        compiler_params=_cparams("arbitrary", "arbitrary"),
        name="s5",
    )(za, bblk, cblk, coef, d.reshape(1, S5_W).astype(F32), w_glu.astype(BF))


def _hgrn_constants():
    c = HG_CHUNK
    t = np.arange(c)[:, None]
    u = np.arange(c)[None, :]
    mats = [(u <= t)]
    masks = []
    for lv in range(HG_LEVELS):
        n = c >> lv
        half = n // 2
        ref = (t // n) * n + half - 1
        lower = (t % n) >= half
        if half < SUBLANES:
            mats.append(np.where(lower, (u > ref) & (u <= t), (u > t) & (u <= ref)))
        same = (t // n) == (u // n)
        masks.append(same & lower & ((u % n) < half))
    masks.append(t == u)
    gall = np.concatenate(mats, axis=0).astype(np.float32)
    mstk = np.stack([np.tile(m, (HG_HEADS, 1)) for m in masks]).astype(np.float32)
    lane_head = np.arange(HG_W)[None, :] // HG_D
    hmask = (np.repeat(np.arange(HG_HEADS), c)[:, None] == lane_head).astype(np.float32)
    bd = (np.arange(HG_W)[:, None] // HG_D == lane_head).astype(np.float32)
    return gall, mstk, hmask, bd


def _hgrn_kernel(q_ref, f_ref, i_ref, g_ref, lb_ref, gain_ref, gall_ref, mstk_ref, hmask_ref, bd_ref, bdn_ref,
                 o_ref, st_scr, *, tt):
    c = HG_CHUNK

    @pl.when(pl.program_id(1) == 0)
    def _():
        st_scr[...] = jnp.zeros_like(st_scr)

    lb = lb_ref[...]
    hmask = hmask_ref[...]

    def level_decay(b, sums, lv):
        n = c >> lv
        half = n // 2
        if half < SUBLANES:
            fine = lv - (HG_LEVELS - HG_FINE_LEVELS)
            return jnp.exp(sums[(1 + fine) * c:(2 + fine) * c])
        pieces = []
        for blk in range(c // n):
            ref = blk * n + half - 1
            pieces.append(b[ref:ref + 1] - b[blk * n:blk * n + half])
            pieces.append(b[blk * n + half:(blk + 1) * n] - b[ref:ref + 1])
        return jnp.exp(jnp.concatenate(pieces, axis=0))

    def chunk(row, fl, sums, carry):
        qf = _silu(q_ref[0, pl.ds(row, c), :])
        kf = (1.0 - lb) * jax.nn.sigmoid(-fl)
        v = i_ref[0, pl.ds(row, c), :]
        v_bf = v.astype(BF)
        b = sums[0:c]
        e_b = jnp.exp(b)
        e_suf = jnp.exp(b[c - 1:c] - b)

        att = jnp.zeros((HG_HEADS * c, c), F32)
        for lv in range(HG_LEVELS + 1):
            if lv < HG_LEVELS:
                e = level_decay(b, sums, lv)
                z, w = qf * e, kf * e
            else:
                z, w = qf, kf
            zs = (jnp.concatenate([z] * HG_HEADS, axis=0) * hmask).astype(BF)
            att = att + mstk_ref[lv] * _dot_nt(zs, w.astype(BF))
        o4 = _dot(att.astype(BF), v_bf) * hmask
        o = o4[0:c]
        for h in range(1, HG_HEADS):
            o = o + o4[h * c:(h + 1) * c]

        st = st_scr[...]
        o = o + _dot_nt((qf * e_b).astype(BF), st.astype(BF))
        st_scr[...] = e_b[c - 1:c] * st + bd_ref[...] * _dot_tn(v_bf, (kf * e_suf).astype(BF))

        ms = _dot_exact_rhs(o * o, bdn_ref[...])
        o = o * lax.rsqrt(ms + EPS) * gain_ref[...]
        o_ref[0, pl.ds(row, c), :] = o * _silu(g_ref[0, pl.ds(row, c), :])
        return carry

    def chunk_pair(cp, carry):
        rows = [pl.multiple_of((2 * cp + j) * c, c) for j in range(2)]
        fls = [f_ref[0, pl.ds(r, c), :] for r in rows]
        lfs = [jnp.log(jnp.maximum(lb + (1.0 - lb) * jax.nn.sigmoid(fl), TINY)) for fl in fls]
        sums = _dot_exact_lhs(gall_ref[...], jnp.concatenate(lfs, axis=1))
        for j in range(2):
            carry = chunk(rows[j], fls[j], sums[:, j * HG_W:(j + 1) * HG_W], carry)
        return carry

    lax.fori_loop(0, tt // (2 * c), chunk_pair, 0)


def _hgrn(za, lb, gain, tt=512):
    b, l, _ = za.shape
    assert l % tt == 0
    gall, mstk, hmask, bd = _hgrn_constants()
    col = lambda j: pl.BlockSpec((1, tt, HG_W), lambda bi, i: (bi, i, j))
    consts = [jnp.asarray(gall, BF), jnp.asarray(mstk, F32), jnp.asarray(hmask, F32), jnp.asarray(bd, F32),
              jnp.asarray(bd / HG_D, BF)]
    return pl.pallas_call(
        functools.partial(_hgrn_kernel, tt=tt),
        grid=(b, l // tt),
        in_specs=[col(1), col(2), col(3), col(4), _const_spec((1, HG_W)), _const_spec((1, HG_W))]
                 + [_const_spec(x.shape) for x in consts],
        out_specs=pl.BlockSpec((1, tt, HG_W), lambda bi, i: (bi, i, 0)),
        out_shape=jax.ShapeDtypeStruct((b, l, HG_W), F32),
        scratch_shapes=[pltpu.VMEM((HG_W, HG_W), F32)],
        compiler_params=_cparams("arbitrary", "arbitrary"),
        name="hgrn2",
    )(za, za, za, za, lb.reshape(1, HG_W).astype(F32), jnp.tile(gain.astype(F32), HG_HEADS).reshape(1, HG_W),
      *consts)


def _compress_kernel(k16_ref, v16_ref, kpa_ref, kpb_ref, kw1a_ref, kw1b_ref, kw2_ref,
                     vpa_ref, vpb_ref, vw1a_ref, vw1b_ref, vw2_ref, kc_ref, vct_ref, *, nc):
    def hidden(x_ref, pa, pb, w1a, w1b):
        x16 = jnp.concatenate([x_ref[0, pl.ds(j, nc, stride=CMP_STRIDE), :] for j in range(CMP_STRIDE)], axis=1)
        first = _dot((x16 + pa).astype(BF), w1a)
        second = _dot((x16 + pb).astype(BF), w1b)
        pre = first + pltpu.roll(second, nc - 1, 0)
        rows = lax.broadcasted_iota(jnp.int32, pre.shape, 0)
        pre = jnp.where(rows < nc - 1, pre, 0.0)
        return _gelu_tanh(pre).astype(BF)

    hk = hidden(k16_ref, kpa_ref[...], kpb_ref[...], kw1a_ref[...], kw1b_ref[...])
    kc_ref[0] = _dot(hk, kw2_ref[...]).astype(BF)
    hv = hidden(v16_ref, vpa_ref[...], vpb_ref[...], vw1a_ref[...], vw1b_ref[...])
    vct_ref[0] = _dot_nt(vw2_ref[...], hv).astype(BF)


def _compress_params(pos, w1, w2, transpose_out=False):
    eye = jnp.eye(NSA_G, dtype=F32)
    w1r = w1.astype(F32).reshape(CMP_LEN, NSA_DH, NSA_DH)
    wexp = jnp.einsum('jde,gh->jgdhe', w1r, eye).reshape(CMP_LEN, KV_W, KV_W)
    half = CMP_LEN // 2
    w1a = wexp[:half].reshape(half * KV_W, KV_W).astype(BF)
    w1b = wexp[half:].reshape(half * KV_W, KV_W).astype(BF)
    pt = jnp.broadcast_to(pos.astype(F32)[:, None, :], (CMP_LEN, NSA_G, NSA_DH))
    pa = pt[:half].reshape(1, half * KV_W)
    pb = pt[half:].reshape(1, half * KV_W)
    w2bd = jnp.einsum('de,gh->gdhe', w2.astype(F32), eye).reshape(KV_W, KV_W).astype(BF)
    return pa, pb, w1a, w1b, (w2bd.T if transpose_out else w2bd)


def _compress(kcr, vcr, kparams, vparams):
    b, l, _ = kcr.shape
    nc = l // CMP_STRIDE
    per_b = lambda r, cc: pl.BlockSpec((1, r, cc), lambda bi: (bi, 0, 0))
    params = list(kparams) + list(vparams)
    return pl.pallas_call(
        functools.partial(_compress_kernel, nc=nc),
        grid=(b,),
        in_specs=[per_b(l, KV_W), per_b(l, KV_W)] + [_const_spec(p.shape) for p in params],
        out_specs=[per_b(nc, KV_W), per_b(KV_W, nc)],
        out_shape=[jax.ShapeDtypeStruct((b, nc, KV_W), BF), jax.ShapeDtypeStruct((b, KV_W, nc), BF)],
        compiler_params=_cparams("arbitrary"),
        name="compress",
    )(kcr, vcr, *params)


def _t5_bucket(dist):
    n = jnp.maximum(dist, 0)
    max_exact = REL_BUCKETS // 2
    nf = jnp.maximum(n, max_exact).astype(jnp.float32)
    large = max_exact + (jnp.log(nf / max_exact) / math.log(REL_MAX_DIST / max_exact)
                         * (REL_BUCKETS - max_exact)).astype(jnp.int32)
    large = jnp.minimum(large, REL_BUCKETS - 1)
    return jnp.where(n < max_exact, n, large)


def _bias_tables(rel_bias, l):
    tab = rel_bias.astype(F32) * LOG2E
    bucket = _t5_bucket(jnp.arange(l))
    by_dist = jnp.sum(jnp.where(bucket[None, :, None] == jnp.arange(REL_BUCKETS)[None, None, :],
                                tab.T[:, None, :], 0.0), axis=-1)
    nc = l // CMP_STRIDE
    w = 2 * K_TILE

    def shifted_rows(v, n_rows, step):
        flat = jnp.tile(v, (1,) * (v.ndim - 1) + (n_rows,))[..., :n_rows * (w - step)]
        return flat.reshape(v.shape[:-1] + (n_rows, w - step))[..., :Q_TILE]

    def group_layout(t, valid):
        _, n, r, q = t.shape
        t = jnp.where(jnp.asarray(valid)[None], t, NEG_INF)
        return t.reshape(NSA_G, NSA_R, n, r, q).transpose(0, 2, 3, 1, 4).reshape(NSA_G, n, r, NSA_R * q)

    tq = np.arange(Q_TILE)[None, None, :]
    key = np.arange(K_TILE)[None, :, None]
    blocks = jnp.pad(by_dist, ((0, 0), (K_TILE, w)), mode='edge').reshape(NSA_H, -1, K_TILE)
    vec = jnp.concatenate([blocks[:, 1:N_TOEPLITZ + 1], blocks[:, 0:N_TOEPLITZ]], axis=-1)
    toep = shifted_rows(vec, K_TILE, 1)
    masked_tile = jnp.full((NSA_G, 1, K_TILE, GR_LANES), NEG_INF, F32)
    d = np.arange(N_TOEPLITZ)[:, None, None] * K_TILE + tq - key
    tz = jnp.concatenate([masked_tile, group_layout(toep, d >= 0)], axis=1)
    d = (N_WIN_TILES - 1 - np.arange(N_WIN_TILES))[:, None, None] * K_TILE + tq - key
    wz = group_layout(toep[:, N_WIN_TILES - 1::-1], (d >= 0) & (d < WIN))
    wz = jnp.concatenate([masked_tile, wz, masked_tile], axis=1)
    na = (2 * nc - SUBLANES) // SUBLANES
    c0 = CMP_STRIDE * (nc - SUBLANES) - (CMP_LEN - 1)
    front = K_TILE * (na + 1)
    off = c0 % K_TILE
    padded = jnp.pad(by_dist, ((0, 0), (front, w)), mode='edge')
    nb = (padded.shape[1] - off) // K_TILE
    blocks = padded[:, off:off + nb * K_TILE].reshape(NSA_H, nb, K_TILE)
    k0 = (front + c0 - off) // K_TILE
    vec = jnp.concatenate([blocks[:, k0 - na + 1:k0 + 1][:, ::-1], blocks[:, k0 - na:k0][:, ::-1]], axis=-1)
    cmp_rows = shifted_rows(vec, SUBLANES, CMP_STRIDE).reshape(NSA_H, 1, na * SUBLANES, Q_TILE)
    u = np.arange(na * SUBLANES)[None, :, None]
    mc = group_layout(cmp_rows, tq - CMP_STRIDE * (u - (nc - SUBLANES)) - (CMP_LEN - 1) >= 0)[:, 0]
    return tz, wz, mc


def _nsa_kernel(qt_ref, gt_ref, kc_ref, vct_ref, ksl_ref, kwn_ref, vslt_ref, vwnt_ref, mc_ref, tz_ref, wz_ref,
                ind_ref, expand_ref, o_ref, imp_scr, y_scr, *group_scr, nc, nsb):
    per_group = len(group_scr) // NSA_G
    SLC, WIN_BRANCH = 0, 1

    def branch_scr(g, branch):
        m_scr, acc_scr, s_a, s_b, max_a, max_b = group_scr[g * per_group + 3 + 6 * branch:
                                                           g * per_group + 9 + 6 * branch]
        return m_scr, acc_scr, (s_a, s_b), (max_a, max_b)
    qi = pl.program_id(1)
    q0 = qi * Q_TILE
    n_sel = min(N_SEL, nsb)
    cmp_per_q = Q_TILE // CMP_STRIDE
    cmp_per_slc = SLC_LEN // CMP_STRIDE
    pair = 2 * K_TILE
    n_last = qi // 2
    ones_rows = (lax.broadcasted_iota(jnp.int32, (SLAB, pair), 0) == 0).astype(BF)

    def lanes4(rows):
        return jnp.concatenate(rows, axis=1)

    def group_rows(g):
        return slice(g * NSA_DH, (g + 1) * NSA_DH)

    def fill(branch, kp, buf):
        kp_mem = jnp.maximum(kp, 0)
        k0 = pl.multiple_of(kp_mem * pair, pair)
        for g in range(NSA_G):
            qt_scr, slab_scr = group_scr[g * per_group:g * per_group + 2]
            _, _, s_bufs, max_bufs = branch_scr(g, branch)
            if branch == SLC:
                lhs = jnp.concatenate([ksl_ref[0, pl.ds(k0, pair), :], ind_ref[...]], axis=1)
                slab = lanes4([slab_scr[pl.ds(pl.multiple_of(kp * SLAB, SLAB), SLAB), :]] * NSA_R)
                rhs = jnp.concatenate([qt_scr[...], slab, jnp.zeros((KV_W - SLAB, GR_LANES), BF)], axis=0)
                bias = [tz_ref[g, jnp.clip(qi - (2 * kp + j), -1, N_TOEPLITZ - 1) + 1] for j in range(2)]
            else:
                lhs, rhs = kwn_ref[0, pl.ds(k0, pair), :], qt_scr[...]
                rel = [2 * kp + j - (qi - (N_WIN_TILES - 1)) + 1 for j in range(2)]
                bias = [wz_ref[g, jnp.where(kp < 0, 0, rel[j])] for j in range(2)]
            s = _dot(lhs, rhs) + jnp.concatenate(bias, axis=0)
            s_bufs[buf][...] = s
            max_bufs[buf][...] = jnp.max(s, axis=0, keepdims=True)

    def drain(branch, kp, buf):
        vt_ref = vslt_ref if branch == SLC else vwnt_ref
        k0 = pl.multiple_of(jnp.maximum(kp, 0) * pair, pair)
        for g in range(NSA_G):
            m_scr, acc_scr, s_bufs, max_bufs = branch_scr(g, branch)
            m_prev = m_scr[...]
            m_new = jnp.maximum(m_prev, max_bufs[buf][...])
            p = jnp.exp2(s_bufs[buf][...] - m_new).astype(BF)
            vt = jnp.concatenate([vt_ref[0, group_rows(g), pl.ds(k0, pair)], ones_rows], axis=0)
            acc_scr[...] = jnp.exp2(m_prev - m_new) * acc_scr[...] + _dot(vt, p)
            m_scr[...] = m_new

    def result(g, branch):
        _, acc_scr, _, _ = branch_scr(g, branch)
        return acc_scr[0:NSA_DH] * (1.0 / acc_scr[NSA_DH:NSA_DH + 1])

    block_scores = []
    for g in range(NSA_G):
        qt_scr = group_scr[g * per_group]
        ocmp_scr = group_scr[g * per_group + 2]
        zeros = jnp.zeros((NSA_DH, Q_TILE), BF)
        parts = []
        for r in range(NSA_R):
            h = g * NSA_R + r
            qh = qt_ref[0, h * NSA_DH:(h + 1) * NSA_DH, :]
            parts.append(jnp.concatenate([qh, zeros] if g == 0 else [zeros, qh], axis=0))
        qt = lanes4(parts)
        qt_scr[...] = qt

        bias_c = mc_ref[g, pl.ds(pl.multiple_of((nc - SUBLANES) - cmp_per_q * qi, SUBLANES), nc), :]
        s = _dot(kc_ref[0], qt) + bias_c
        m = jnp.max(s, axis=0, keepdims=True)
        e = jnp.exp2(s - m)
        lsum = jnp.sum(e, axis=0, keepdims=True)
        tq4 = q0 + lax.broadcasted_iota(jnp.int32, (1, GR_LANES), 1) % Q_TILE
        inv = jnp.where(tq4 >= CMP_LEN - 1, 1.0 / jnp.maximum(lsum, TINY), 0.0)
        p = e * inv
        ocmp_scr[...] = _dot(vct_ref[0, group_rows(g), :], p.astype(BF))
        imp = p[:, 0:Q_TILE]
        for r in range(1, NSA_R):
            imp = imp + p[:, r * Q_TILE:(r + 1) * Q_TILE]

        imp_scr[0:SUBLANES, :] = jnp.zeros((SUBLANES, Q_TILE), F32)
        imp_scr[SUBLANES:SUBLANES + nc, :] = imp
        a = [imp_scr[pl.ds(SUBLANES - 1 + k, nsb, stride=cmp_per_slc), :] for k in range(cmp_per_slc + 1)]
        p_slc = a[0] + a[1]
        for k in range(1, cmp_per_slc):
            p_slc = p_slc + a[k] + a[k + 1]

        block_scores.append(p_slc)
        for branch in (SLC, WIN_BRANCH):
            m_scr, acc_scr, _, _ = branch_scr(g, branch)
            m_scr[...] = jnp.full_like(m_scr, NEG_INF)
            acc_scr[...] = jnp.zeros_like(acc_scr)

    jidx = lax.broadcasted_iota(jnp.int32, (nsb, Q_TILE), 0)
    tq = q0 + lax.broadcasted_iota(jnp.int32, (nsb, Q_TILE), 1)
    valid = jidx * SLC_LEN <= tq

    def store_masks(masks):
        for g in range(NSA_G):
            group_scr[g * per_group + 1][...] = _dot(expand_ref[...], masks[g].astype(BF)).astype(BF)

    n_forced = 3
    few_blocks_tiles = (n_sel * SLC_LEN) // Q_TILE

    @pl.when(qi < few_blocks_tiles)
    def _():
        store_masks([jnp.where(valid, 0.0, NEG_INF)] * NSA_G)

    @pl.when(qi >= few_blocks_tiles)
    def _():
        cur = tq // SLC_LEN
        forced = (jidx == 0) | (jidx == cur) | (jidx == cur - 1)
        masks = []
        for g in range(NSA_G):
            score = jnp.where(forced, -jnp.inf, jnp.where(valid, block_scores[g], NEG_INF))
            for _ in range(n_sel - n_forced):
                mx = jnp.max(score, axis=0, keepdims=True)
                first = jnp.min(jnp.where(score == mx, jidx, nsb), axis=0, keepdims=True)
                score = jnp.where(jidx == first, -jnp.inf, score)
            masks.append(jnp.where(score == -jnp.inf, 0.0, NEG_INF))
        store_masks(masks)

    win_first = n_last - (N_WIN_TILES // 2)
    fill(SLC, jnp.int32(0), 0)

    def slc_body(j, carry):
        kp = 2 * j
        fill(SLC, kp + 1, 1)
        drain(SLC, kp, 0)
        fill(SLC, kp + 2, 0)
        drain(SLC, kp + 1, 1)
        return carry

    lax.fori_loop(0, n_last // 2, slc_body, 0)

    @pl.when(n_last % 2 == 1)
    def _():
        fill(SLC, n_last, 1)
        fill(WIN_BRANCH, win_first, 0)
        drain(SLC, n_last - 1, 0)
        drain(SLC, n_last, 1)

    @pl.when(n_last % 2 == 0)
    def _():
        fill(WIN_BRANCH, win_first, 0)
        drain(SLC, n_last, 0)

    fill(WIN_BRANCH, win_first + 1, 1)
    drain(WIN_BRANCH, win_first, 0)
    fill(WIN_BRANCH, win_first + 2, 0)
    drain(WIN_BRANCH, win_first + 1, 1)
    drain(WIN_BRANCH, win_first + 2, 0)

    for g in range(NSA_G):
        def gate(branch):
            return lanes4([gt_ref[0, pl.ds((g * NSA_R + r) * 3 + branch, 1), :] for r in range(NSA_R)])

        yt = (gate(0) * group_scr[g * per_group + 2][...] + gate(1) * result(g, SLC)
              + gate(2) * result(g, WIN_BRANCH))
        for r in range(NSA_R):
            h = g * NSA_R + r
            y_scr[h * NSA_DH:(h + 1) * NSA_DH, :] = yt[:, r * Q_TILE:(r + 1) * Q_TILE]

    o_ref[0] = y_scr[...].T


def _nsa(qt, gt, kc, vct, ksw, vt, tz, wz, mc):
    b, _, l = qt.shape
    nc = l // CMP_STRIDE
    nsb = l // SLC_LEN
    per_b = lambda shape, idx: pl.BlockSpec((1,) + shape, lambda bi, i: (bi,) + idx)
    pair = 2 * K_TILE
    blocks_per_pair = pair // SLC_LEN
    n_pairs = l // pair
    ind = (np.arange(pair)[:, None] // SLC_LEN == np.arange(KV_W)[None, :]).astype(np.float32)
    expand = np.zeros((n_pairs * SLAB, nsb), np.float32)
    for k in range(n_pairs):
        for j in range(blocks_per_pair):
            expand[k * SLAB + j, k * blocks_per_pair + j] = 1.0
    ind, expand = jnp.asarray(ind, BF), jnp.asarray(expand, BF)
    return pl.pallas_call(
        functools.partial(_nsa_kernel, nc=nc, nsb=nsb),
        grid=(b, l // Q_TILE),
        in_specs=[pl.BlockSpec((1, NSA_W, Q_TILE), lambda bi, i: (bi, 0, i)),
                  pl.BlockSpec((1, 32, Q_TILE), lambda bi, i: (bi, 0, i)),
                  per_b((nc, KV_W), (0, 0)), per_b((KV_W, nc), (0, 0)),
                  per_b((l, KV_W), (0, 0)), per_b((l, KV_W), (0, 1)),
                  per_b((KV_W, l), (0, 0)), per_b((KV_W, l), (1, 0)),
                  _const_spec(mc.shape), _const_spec(tz.shape), _const_spec(wz.shape),
                  _const_spec(ind.shape), _const_spec(expand.shape)],
        out_specs=pl.BlockSpec((1, Q_TILE, NSA_W), lambda bi, i: (bi, i, 0)),
        out_shape=jax.ShapeDtypeStruct((b, l, NSA_W), F32),
        scratch_shapes=[pltpu.VMEM((nc + SUBLANES, Q_TILE), F32), pltpu.VMEM((NSA_W, Q_TILE), F32)]
                       + NSA_G * ([pltpu.VMEM((KV_W, GR_LANES), BF), pltpu.VMEM((n_pairs * SLAB, Q_TILE), BF),
                                   pltpu.VMEM((NSA_DH, GR_LANES), F32)]
                                  + 2 * [pltpu.VMEM((1, GR_LANES), F32), pltpu.VMEM((NSA_DH + SLAB, GR_LANES), F32),
                                         pltpu.VMEM((pair, GR_LANES), F32), pltpu.VMEM((pair, GR_LANES), F32),
                                         pltpu.VMEM((1, GR_LANES), F32), pltpu.VMEM((1, GR_LANES), F32)]),
        compiler_params=_cparams("arbitrary", "arbitrary"),
        name="nsa",
    )(qt, gt, kc, vct, ksw, ksw, vt, vt, mc, tz, wz, ind, expand)


def _split_w_in(w_in):
    o_q = N_TOK_A
    o_kv = o_q + NSA_W
    o_gate = o_kv + 6 * KV_W
    kv = lambda j: w_in[:, o_kv + j * KV_W:o_kv + (j + 1) * KV_W]
    w_tok = jnp.concatenate([w_in[:, :N_TOK_A], kv(0), kv(1), kv(2), kv(4)], axis=1).astype(BF)
    gates = w_in[:, o_gate:]
    pad = jnp.zeros((w_in.shape[0], 32 - gates.shape[1]), w_in.dtype)
    w_t = jnp.concatenate([w_in[:, o_q:o_kv], kv(3), kv(5), gates, pad], axis=1).T.astype(BF)
    return w_tok, w_t


def kernel(x, ffn1_norm, ffn1_w_gate, ffn1_w_up, ffn1_w_down, mix_norm, w_in, w_out, s5_lambda_re, s5_lambda_im, s5_log_dt, s5_b_re, s5_b_im, s5_c_re, s5_c_im, s5_d, s5_w_glu, hgrn_lb_logits, hgrn_norm, nsa_cmp_pos_k, nsa_cmp_w1_k, nsa_cmp_w2_k, nsa_cmp_pos_v, nsa_cmp_w1_v, nsa_cmp_w2_v, rel_bias, ffn2_norm, ffn2_w_gate, ffn2_w_up, ffn2_w_down, final_norm):
    b, l, d = x.shape
    depth = w_in.shape[0]
    gam = jax.nn.softmax(hgrn_lb_logits.astype(F32), axis=0)
    lower_bounds = jnp.cumsum(gam, axis=0) - gam[0:1]
    tz, wz, mc = _bias_tables(rel_bias, l)
    bf = lambda w: w.astype(BF)

    x2 = x.reshape(b * l, d)
    for i in range(depth):
        x2 = _ffn(x2, ffn1_norm[i], bf(ffn1_w_gate[i]), bf(ffn1_w_up[i]), bf(ffn1_w_down[i]))
        w_tok, w_t = _split_w_in(w_in[i])
        za, kcr, vcr, ksw, qt, vt, gt = _inproj(x2.reshape(b, l, d), mix_norm[i], w_tok, w_t)
        bblk, cblk, coef = _s5_params(s5_lambda_re[i], s5_lambda_im[i], s5_log_dt[i], s5_b_re[i], s5_b_im[i],
                                      s5_c_re[i], s5_c_im[i])
        y_s5 = _s5(za, bblk, cblk, coef, s5_d[i], s5_w_glu[i])
        y_hg = _hgrn(za, lower_bounds[i], hgrn_norm[i])
        kc, vct = _compress(kcr, vcr,
                            _compress_params(nsa_cmp_pos_k[i], nsa_cmp_w1_k[i], nsa_cmp_w2_k[i]),
                            _compress_params(nsa_cmp_pos_v[i], nsa_cmp_w1_v[i], nsa_cmp_w2_v[i], transpose_out=True))
        y_nsa = _nsa(qt, gt, kc, vct, ksw, vt, tz, wz, mc)
        proj = (y_s5.reshape(b * l, S5_W), y_hg.reshape(b * l, HG_W), y_nsa.reshape(b * l, NSA_W), bf(w_out[i]))
        x2 = _ffn(x2, ffn2_norm[i], bf(ffn2_w_gate[i]), bf(ffn2_w_up[i]), bf(ffn2_w_down[i]), proj=proj,
                  final_g=final_norm if i + 1 == depth else None)
    return x2.reshape(b, l, d)
```

```python
import functools
import math

import numpy as np
import jax
import jax.numpy as jnp
from jax import lax
from jax.experimental import pallas as pl
from jax.experimental.pallas import tpu as pltpu

BF = jnp.bfloat16
F32 = jnp.float32

EPS = 1e-6
NEG_INF = -1e30
TINY = 1e-30
LOG2E = math.log2(math.e)

S5_W = 256
S5_G = 16
S5_GROUP = 16
S5_P = 64
S5_N = S5_G * S5_P
HG_W = 256
HG_HEADS = 4
HG_D = 64
NSA_W = 512
NSA_DH = 64
NSA_H = 8
NSA_G = 2
NSA_R = 4
KV_W = NSA_G * NSA_DH
CMP_LEN = 32
CMP_STRIDE = 16
SLC_LEN = 64
N_SEL = 16
WIN = 512
REL_BUCKETS = 32
REL_MAX_DIST = 1024

Q_TILE = 128
K_TILE = 128
GR_LANES = NSA_R * Q_TILE
QT_STEP = 2
STEP_Q = QT_STEP * Q_TILE
STEP_LANES = QT_STEP * GR_LANES
N_TOEPLITZ = REL_MAX_DIST // K_TILE + 2
N_WIN_TILES = WIN // K_TILE + 1
SLAB = 16
HG_CHUNK = 64
HG_LEVELS = 6
SUBLANES = 8
HG_FINE_LEVELS = 3
VMEM_LIMIT_BYTES = 56 * 1024 * 1024


def _cparams(*sem):
    return pltpu.CompilerParams(dimension_semantics=sem, vmem_limit_bytes=VMEM_LIMIT_BYTES)


def _const_spec(shape):
    nd = len(shape)
    return pl.BlockSpec(shape, lambda *_: (0,) * nd, pipeline_mode=pl.Buffered(1))


def _rms(x, g_row):
    ms = jnp.mean(x * x, axis=-1, keepdims=True)
    return x * lax.rsqrt(ms + EPS) * g_row


def _silu(x):
    return x * jax.nn.sigmoid(x)


def _gelu_tanh(x):
    return 0.5 * x * (1.0 + jnp.tanh(math.sqrt(2.0 / math.pi) * (x + 0.044715 * (x * x * x))))


def _dot(a, b):
    return jnp.dot(a, b, preferred_element_type=F32)


def _dot_nt(a, b):
    return lax.dot_general(a, b, (((1,), (1,)), ((), ())), preferred_element_type=F32)


def _dot_tn(a, b):
    return lax.dot_general(a, b, (((0,), (0,)), ((), ())), preferred_element_type=F32)


def _dot_exact_lhs(c_bf, x):
    hi = x.astype(BF)
    r1 = x - hi.astype(F32)
    mid = r1.astype(BF)
    lo = (r1 - mid.astype(F32)).astype(BF)
    return _dot(c_bf, hi) + _dot(c_bf, mid) + _dot(c_bf, lo)


def _dot_exact_rhs(x, c_bf):
    hi = x.astype(BF)
    r1 = x - hi.astype(F32)
    mid = r1.astype(BF)
    lo = (r1 - mid.astype(F32)).astype(BF)
    return _dot(hi, c_bf) + _dot(mid, c_bf) + _dot(lo, c_bf)


def _ffn_kernel(*refs, n_chunks, tf, with_proj, with_final):
    it = iter(refs)
    x_ref = next(it)
    if with_proj:
        ys5_ref, yhg_ref, ynsa_ref, wo_ref = next(it), next(it), next(it), next(it)
    g_ref, wg_ref, wu_ref, wd_ref = next(it), next(it), next(it), next(it)
    if with_final:
        fg_ref = next(it)
    o_ref, h_scr, a_scr = next(it), next(it), next(it)

    x = x_ref[...]
    if with_proj:
        x = (x + _dot(ys5_ref[...].astype(BF), wo_ref[0:S5_W, :])
             + _dot(yhg_ref[...].astype(BF), wo_ref[S5_W:S5_W + HG_W, :])
             + _dot(ynsa_ref[...].astype(BF), wo_ref[S5_W + HG_W:, :]))
    h_scr[...] = _rms(x, g_ref[...]).astype(BF)
    for c in range(n_chunks):
        sl = slice(c * tf, (c + 1) * tf)
        h = h_scr[...]
        gate = _dot(h, wg_ref[:, sl])
        up = _dot(h, wu_ref[:, sl])
        a_scr[:, sl] = (_silu(gate) * up).astype(BF)
    x = x + 0.5 * _dot(a_scr[...], wd_ref[...])
    if with_final:
        x = _rms(x, fg_ref[...])
    o_ref[...] = x


def _ffn(x2, g, wg, wu, wd, proj=None, final_g=None, tm=512, tf=256):
    n, d = x2.shape
    dff = wg.shape[1]
    assert n % tm == 0 and dff % tf == 0
    row = lambda i: (i, 0)
    in_specs = [pl.BlockSpec((tm, d), row)]
    args = [x2]
    if proj is not None:
        ys5, yhg, ynsa, wo = proj
        in_specs += [pl.BlockSpec((tm, S5_W), row), pl.BlockSpec((tm, HG_W), row),
                     pl.BlockSpec((tm, NSA_W), row), _const_spec(wo.shape)]
        args += [ys5, yhg, ynsa, wo]
    in_specs += [_const_spec((1, d)), _const_spec(wg.shape), _const_spec(wu.shape), _const_spec(wd.shape)]
    args += [g.reshape(1, d), wg, wu, wd]
    if final_g is not None:
        in_specs.append(_const_spec((1, d)))
        args.append(final_g.reshape(1, d))
    kern = functools.partial(_ffn_kernel, n_chunks=dff // tf, tf=tf,
                             with_proj=proj is not None, with_final=final_g is not None)
    return pl.pallas_call(
        kern,
        grid=(n // tm,),
        in_specs=in_specs,
        out_specs=pl.BlockSpec((tm, d), row),
        out_shape=jax.ShapeDtypeStruct((n, d), F32),
        scratch_shapes=[pltpu.VMEM((tm, d), BF), pltpu.VMEM((tm, dff), BF)],
        compiler_params=_cparams("arbitrary"),
        name="ffn",
    )(*args)


N_TOK_A = S5_W + 4 * HG_W
N_T_ROWS = NSA_W + 2 * KV_W + 32


def _inproj_kernel(x_ref, g_ref, wtok_ref, wt_ref, za_ref, kcr_ref, vcr_ref, ksw_ref, qt_ref, vt_ref, gt_ref):
    h = _rms(x_ref[0], g_ref[...]).astype(BF)
    for c in range(N_TOK_A // 256):
        sl = slice(c * 256, (c + 1) * 256)
        za_ref[0, :, sl] = _dot(h, wtok_ref[:, sl])
    kcr_ref[0] = _dot(h, wtok_ref[:, N_TOK_A:N_TOK_A + KV_W])
    vcr_ref[0] = _dot(h, wtok_ref[:, N_TOK_A + KV_W:N_TOK_A + 2 * KV_W])
    ksw_ref[0] = _dot(h, wtok_ref[:, N_TOK_A + 2 * KV_W:N_TOK_A + 4 * KV_W]).astype(BF)
    qt_ref[0] = (_dot_nt(wt_ref[0:NSA_W, :], h) * (NSA_DH ** -0.5 * LOG2E)).astype(BF)
    vt_ref[0] = _dot_nt(wt_ref[NSA_W:NSA_W + 2 * KV_W, :], h).astype(BF)
    gt_ref[0] = jax.nn.sigmoid(_dot_nt(wt_ref[NSA_W + 2 * KV_W:, :], h))


def _inproj(x3, g, w_tok, w_t, tm=512):
    b, l, d = x3.shape
    assert l % tm == 0
    tok = lambda w: pl.BlockSpec((1, tm, w), lambda bi, i: (bi, i, 0))
    tr = lambda r: pl.BlockSpec((1, r, tm), lambda bi, i: (bi, 0, i))
    return pl.pallas_call(
        _inproj_kernel,
        grid=(b, l // tm),
        in_specs=[tok(d), _const_spec((1, d)), _const_spec(w_tok.shape), _const_spec(w_t.shape)],
        out_specs=[tok(N_TOK_A), tok(KV_W), tok(KV_W), tok(2 * KV_W), tr(NSA_W), tr(2 * KV_W), tr(32)],
        out_shape=[jax.ShapeDtypeStruct((b, l, N_TOK_A), F32),
                   jax.ShapeDtypeStruct((b, l, KV_W), F32),
                   jax.ShapeDtypeStruct((b, l, KV_W), F32),
                   jax.ShapeDtypeStruct((b, l, 2 * KV_W), BF),
                   jax.ShapeDtypeStruct((b, NSA_W, l), BF),
                   jax.ShapeDtypeStruct((b, 2 * KV_W, l), BF),
                   jax.ShapeDtypeStruct((b, 32, l), F32)],
        compiler_params=_cparams("arbitrary", "arbitrary"),
        name="inproj",
    )(x3, g.reshape(1, d), w_tok, w_t)


def _s5_kernel(u_ref, bblk_ref, cblk_ref, coef_ref, d_ref, wglu_ref, y_ref, xs_scr, carry_scr, *, tc):
    @pl.when(pl.program_id(1) == 0)
    def _():
        carry_scr[...] = jnp.zeros_like(carry_scr)

    u = u_ref[0]
    xs_scr[...] = _dot(u.astype(BF), bblk_ref[...])

    def body(r, carry):
        cre, cim = carry
        row = pl.multiple_of(r * SUBLANES, SUBLANES)
        xre = xs_scr[pl.ds(row, SUBLANES), 0:S5_N]
        xim = xs_scr[pl.ds(row, SUBLANES), S5_N:2 * S5_N]
        for idx, k in enumerate((1, 2, 4)):
            are, aim = coef_ref[idx, 0], coef_ref[idx, 1]
            sre, sim = pltpu.roll(xre, k, 0), pltpu.roll(xim, k, 0)
            xre, xim = xre + (are * sre - aim * sim), xim + (are * sim + aim * sre)
        pre, pim = coef_ref[3, 0], coef_ref[3, 1]
        xre, xim = xre + (pre * cre - pim * cim), xim + (pre * cim + pim * cre)
        xs_scr[pl.ds(row, SUBLANES), 0:S5_N] = xre
        xs_scr[pl.ds(row, SUBLANES), S5_N:2 * S5_N] = xim
        return xre[SUBLANES - 1:SUBLANES], xim[SUBLANES - 1:SUBLANES]

    cre, cim = lax.fori_loop(0, tc // SUBLANES, body, (carry_scr[0:1], carry_scr[1:2]))
    carry_scr[0:1] = cre
    carry_scr[1:2] = cim

    y = _dot(xs_scr[...].astype(BF), cblk_ref[...]) + d_ref[...] * u
    y = _gelu_tanh(y)
    y_ref[0] = y * jax.nn.sigmoid(_dot(y.astype(BF), wglu_ref[...]))


def _s5_params(lam_re, lam_im, log_dt, b_re, b_im, c_re, c_im):
    lr, li = lam_re.astype(F32), lam_im.astype(F32)
    dt = jnp.exp(log_dt.astype(F32))[:, None]
    mag = jnp.exp(lr * dt)
    ab_re, ab_im = mag * jnp.cos(li * dt), mag * jnp.sin(li * dt)
    den = lr * lr + li * li
    nr, ni = ab_re - 1.0, ab_im
    g_re = (nr * lr + ni * li) / den
    g_im = (ni * lr - nr * li) / den
    br, bi = b_re.astype(F32), b_im.astype(F32)
    bb_re = g_re[..., None] * br - g_im[..., None] * bi
    bb_im = g_re[..., None] * bi + g_im[..., None] * br
    eye = jnp.eye(S5_G, dtype=F32)
    blk = lambda w: jnp.einsum('gph,gk->ghkp', w, eye).reshape(S5_W, S5_N)
    bblk = jnp.concatenate([blk(bb_re), blk(bb_im)], axis=1).astype(BF)
    cblk_f = lambda w: jnp.einsum('ghp,gk->gpkh', w, eye).reshape(S5_N, S5_W)
    cblk = jnp.concatenate([cblk_f(c_re.astype(F32)), -cblk_f(c_im.astype(F32))], axis=0).astype(BF)
    are, aim = ab_re.reshape(1, S5_N), ab_im.reshape(1, S5_N)
    pw = [(are, aim)]
    for _ in range(SUBLANES - 1):
        pr, pi = pw[-1]
        pw.append((pr * are - pi * aim, pr * aim + pi * are))
    rows = np.arange(SUBLANES)[:, None]
    coef = []
    for k in (1, 2, 4):
        m = jnp.asarray((rows >= k).astype(np.float32))
        coef.append(jnp.stack([m * pw[k - 1][0], m * pw[k - 1][1]]))
    coef.append(jnp.stack([jnp.concatenate([p[0] for p in pw], axis=0), jnp.concatenate([p[1] for p in pw], axis=0)]))
    return bblk, cblk, jnp.stack(coef)


def _s5(za, bblk, cblk, coef, d, w_glu, tc=512):
    b, l, _ = za.shape
    assert l % tc == 0
    return pl.pallas_call(
        functools.partial(_s5_kernel, tc=tc),
        grid=(b, l // tc),
        in_specs=[pl.BlockSpec((1, tc, S5_W), lambda bi, i: (bi, i, 0)),
                  _const_spec(bblk.shape), _const_spec(cblk.shape), _const_spec(coef.shape),
                  _const_spec((1, S5_W)), _const_spec(w_glu.shape)],
        out_specs=pl.BlockSpec((1, tc, S5_W), lambda bi, i: (bi, i, 0)),
        out_shape=jax.ShapeDtypeStruct((b, l, S5_W), F32),
        scratch_shapes=[pltpu.VMEM((tc, 2 * S5_N), F32), pltpu.VMEM((2, S5_N), F32)],
        compiler_params=_cparams("arbitrary", "arbitrary"),
        name="s5",
    )(za, bblk, cblk, coef, d.reshape(1, S5_W).astype(F32), w_glu.astype(BF))


def _hgrn_constants():
    c = HG_CHUNK
    t = np.arange(c)[:, None]
    u = np.arange(c)[None, :]
    mats = [(u <= t)]
    masks = []
    for lv in range(HG_LEVELS):
        n = c >> lv
        half = n // 2
        ref = (t // n) * n + half - 1
        lower = (t % n) >= half
        if half < SUBLANES:
            mats.append(np.where(lower, (u > ref) & (u <= t), (u > t) & (u <= ref)))
        same = (t // n) == (u // n)
        masks.append(same & lower & ((u % n) < half))
    masks.append(t == u)
    gall = np.concatenate(mats, axis=0).astype(np.float32)
    mstk = np.stack([np.tile(m, (1, HG_HEADS)) for m in masks]).astype(np.float32)
    lane_head = np.arange(HG_W)[None, :] // HG_D
    hmask = (np.repeat(np.arange(HG_HEADS), c)[:, None] == lane_head).astype(np.float32)
    bd = (np.arange(HG_W)[:, None] // HG_D == lane_head).astype(np.float32)
    return gall, mstk, hmask, bd


def _hgrn_kernel(q_ref, f_ref, i_ref, g_ref, lb_ref, gain_ref, gall_ref, mstk_ref, hmask_ref, bd_ref, bdn_ref,
                 o_ref, st_scr, *, tt):
    c = HG_CHUNK

    @pl.when(pl.program_id(1) == 0)
    def _():
        st_scr[...] = jnp.zeros_like(st_scr)

    lb = lb_ref[...]
    hmask = hmask_ref[...]

    def level_decay(b, sums, lv):
        n = c >> lv
        half = n // 2
        if half < SUBLANES:
            fine = lv - (HG_LEVELS - HG_FINE_LEVELS)
            return jnp.exp(sums[(1 + fine) * c:(2 + fine) * c])
        pieces = []
        for blk in range(c // n):
            ref = blk * n + half - 1
            pieces.append(b[ref:ref + 1] - b[blk * n:blk * n + half])
            pieces.append(b[blk * n + half:(blk + 1) * n] - b[ref:ref + 1])
        return jnp.exp(jnp.concatenate(pieces, axis=0))

    def chunk(row, fl, sums, carry):
        qf = _silu(q_ref[0, pl.ds(row, c), :])
        kf = (1.0 - lb) * jax.nn.sigmoid(-fl)
        v = i_ref[0, pl.ds(row, c), :]
        v_bf = v.astype(BF)
        b = sums[0:c]
        e_b = jnp.exp(b)
        e_suf = jnp.exp(b[c - 1:c] - b)

        att = jnp.zeros((c, HG_HEADS * c), F32)
        for lv in range(HG_LEVELS + 1):
            if lv < HG_LEVELS:
                e = level_decay(b, sums, lv)
                z, w = qf * e, kf * e
            else:
                z, w = qf, kf
            ws = (jnp.concatenate([w] * HG_HEADS, axis=0) * hmask).astype(BF)
            att = att + mstk_ref[lv] * _dot_nt(z.astype(BF), ws)
        o = _dot(att.astype(BF), (jnp.concatenate([v] * HG_HEADS, axis=0) * hmask).astype(BF))

        st = st_scr[...]
        o = o + _dot_nt((qf * e_b).astype(BF), st.astype(BF))
        st_scr[...] = e_b[c - 1:c] * st + bd_ref[...] * _dot_tn(v_bf, (kf * e_suf).astype(BF))

        ms = _dot_exact_rhs(o * o, bdn_ref[...])
        o = o * lax.rsqrt(ms + EPS) * gain_ref[...]
        o_ref[0, pl.ds(row, c), :] = o * _silu(g_ref[0, pl.ds(row, c), :])
        return carry

    def chunk_pair(cp, carry):
        rows = [pl.multiple_of((2 * cp + j) * c, c) for j in range(2)]
        fls = [f_ref[0, pl.ds(r, c), :] for r in rows]
        lfs = [jnp.log(jnp.maximum(lb + (1.0 - lb) * jax.nn.sigmoid(fl), TINY)) for fl in fls]
        sums = _dot_exact_lhs(gall_ref[...], jnp.concatenate(lfs, axis=1))
        for j in range(2):
            carry = chunk(rows[j], fls[j], sums[:, j * HG_W:(j + 1) * HG_W], carry)
        return carry

    lax.fori_loop(0, tt // (2 * c), chunk_pair, 0)


def _hgrn(za, lb, gain, tt=512):
    b, l, _ = za.shape
    assert l % tt == 0
    gall, mstk, hmask, bd = _hgrn_constants()
    col = lambda j: pl.BlockSpec((1, tt, HG_W), lambda bi, i: (bi, i, j))
    consts = [jnp.asarray(gall, BF), jnp.asarray(mstk, F32), jnp.asarray(hmask, F32), jnp.asarray(bd, F32),
              jnp.asarray(bd / HG_D, BF)]
    return pl.pallas_call(
        functools.partial(_hgrn_kernel, tt=tt),
        grid=(b, l // tt),
        in_specs=[col(1), col(2), col(3), col(4), _const_spec((1, HG_W)), _const_spec((1, HG_W))]
                 + [_const_spec(x.shape) for x in consts],
        out_specs=pl.BlockSpec((1, tt, HG_W), lambda bi, i: (bi, i, 0)),
        out_shape=jax.ShapeDtypeStruct((b, l, HG_W), F32),
        scratch_shapes=[pltpu.VMEM((HG_W, HG_W), F32)],
        compiler_params=_cparams("arbitrary", "arbitrary"),
        name="hgrn2",
    )(za, za, za, za, lb.reshape(1, HG_W).astype(F32), jnp.tile(gain.astype(F32), HG_HEADS).reshape(1, HG_W),
      *consts)


def _compress_kernel(k16_ref, v16_ref, kpa_ref, kpb_ref, kw1a_ref, kw1b_ref, kw2_ref,
                     vpa_ref, vpb_ref, vw1a_ref, vw1b_ref, vw2_ref, kc_ref, vct_ref, *, nc):
    def hidden(x_ref, pa, pb, w1a, w1b):
        x16 = jnp.concatenate([x_ref[0, pl.ds(j, nc, stride=CMP_STRIDE), :] for j in range(CMP_STRIDE)], axis=1)
        first = _dot((x16 + pa).astype(BF), w1a)
        second = _dot((x16 + pb).astype(BF), w1b)
        pre = first + pltpu.roll(second, nc - 1, 0)
        rows = lax.broadcasted_iota(jnp.int32, pre.shape, 0)
        pre = jnp.where(rows < nc - 1, pre, 0.0)
        return _gelu_tanh(pre).astype(BF)

    hk = hidden(k16_ref, kpa_ref[...], kpb_ref[...], kw1a_ref[...], kw1b_ref[...])
    kc_ref[0] = _dot(hk, kw2_ref[...]).astype(BF)
    hv = hidden(v16_ref, vpa_ref[...], vpb_ref[...], vw1a_ref[...], vw1b_ref[...])
    vct_ref[0] = _dot_nt(vw2_ref[...], hv).astype(BF)


def _compress_params(pos, w1, w2, transpose_out=False):
    eye = jnp.eye(NSA_G, dtype=F32)
    w1r = w1.astype(F32).reshape(CMP_LEN, NSA_DH, NSA_DH)
    wexp = jnp.einsum('jde,gh->jgdhe', w1r, eye).reshape(CMP_LEN, KV_W, KV_W)
    half = CMP_LEN // 2
    w1a = wexp[:half].reshape(half * KV_W, KV_W).astype(BF)
    w1b = wexp[half:].reshape(half * KV_W, KV_W).astype(BF)
    pt = jnp.broadcast_to(pos.astype(F32)[:, None, :], (CMP_LEN, NSA_G, NSA_DH))
    pa = pt[:half].reshape(1, half * KV_W)
    pb = pt[half:].reshape(1, half * KV_W)
    w2bd = jnp.einsum('de,gh->gdhe', w2.astype(F32), eye).reshape(KV_W, KV_W).astype(BF)
    return pa, pb, w1a, w1b, (w2bd.T if transpose_out else w2bd)


def _compress(kcr, vcr, kparams, vparams):
    b, l, _ = kcr.shape
    nc = l // CMP_STRIDE
    per_b = lambda r, cc: pl.BlockSpec((1, r, cc), lambda bi: (bi, 0, 0))
    params = list(kparams) + list(vparams)
    return pl.pallas_call(
        functools.partial(_compress_kernel, nc=nc),
        grid=(b,),
        in_specs=[per_b(l, KV_W), per_b(l, KV_W)] + [_const_spec(p.shape) for p in params],
        out_specs=[per_b(nc, KV_W), per_b(KV_W, nc)],
        out_shape=[jax.ShapeDtypeStruct((b, nc, KV_W), BF), jax.ShapeDtypeStruct((b, KV_W, nc), BF)],
        compiler_params=_cparams("arbitrary"),
        name="compress",
    )(kcr, vcr, *params)


def _t5_bucket(dist):
    n = jnp.maximum(dist, 0)
    max_exact = REL_BUCKETS // 2
    nf = jnp.maximum(n, max_exact).astype(jnp.float32)
    large = max_exact + (jnp.log(nf / max_exact) / math.log(REL_MAX_DIST / max_exact)
                         * (REL_BUCKETS - max_exact)).astype(jnp.int32)
    large = jnp.minimum(large, REL_BUCKETS - 1)
    return jnp.where(n < max_exact, n, large)


def _bias_tables(rel_bias, l):
    tab = rel_bias.astype(F32) * LOG2E
    bucket = _t5_bucket(jnp.arange(l))
    by_dist = jnp.sum(jnp.where(bucket[None, :, None] == jnp.arange(REL_BUCKETS)[None, None, :],
                                tab.T[:, None, :], 0.0), axis=-1)
    nc = l // CMP_STRIDE
    w = 2 * K_TILE

    def shifted_rows(v, n_rows, step):
        flat = jnp.tile(v, (1,) * (v.ndim - 1) + (n_rows,))[..., :n_rows * (w - step)]
        return flat.reshape(v.shape[:-1] + (n_rows, w - step))[..., :Q_TILE]

    def group_layout(t, valid):
        _, n, r, q = t.shape
        t = jnp.where(jnp.asarray(valid)[None], t, NEG_INF)
        return t.reshape(NSA_G, NSA_R, n, r, q).transpose(0, 2, 3, 1, 4).reshape(NSA_G, n, r, NSA_R * q)

    tq = np.arange(Q_TILE)[None, None, :]
    key = np.arange(K_TILE)[None, :, None]
    blocks = jnp.pad(by_dist, ((0, 0), (K_TILE, w)), mode='edge').reshape(NSA_H, -1, K_TILE)
    vec = jnp.concatenate([blocks[:, 1:N_TOEPLITZ + 1], blocks[:, 0:N_TOEPLITZ]], axis=-1)
    toep = shifted_rows(vec, K_TILE, 1)
    masked_tile = jnp.full((NSA_G, 1, K_TILE, GR_LANES), NEG_INF, F32)
    d = np.arange(N_TOEPLITZ)[:, None, None] * K_TILE + tq - key
    tz = jnp.concatenate([masked_tile, group_layout(toep, d >= 0)], axis=1)
    d = (N_WIN_TILES - 1 - np.arange(N_WIN_TILES))[:, None, None] * K_TILE + tq - key
    wz = group_layout(toep[:, N_WIN_TILES - 1::-1], (d >= 0) & (d < WIN))
    wz = jnp.concatenate([masked_tile, wz, masked_tile], axis=1)
    na = (2 * nc - SUBLANES) // SUBLANES
    c0 = CMP_STRIDE * (nc - SUBLANES) - (CMP_LEN - 1)
    front = K_TILE * (na + 1)
    off = c0 % K_TILE
    padded = jnp.pad(by_dist, ((0, 0), (front, w)), mode='edge')
    nb = (padded.shape[1] - off) // K_TILE
    blocks = padded[:, off:off + nb * K_TILE].reshape(NSA_H, nb, K_TILE)
    k0 = (front + c0 - off) // K_TILE
    vec = jnp.concatenate([blocks[:, k0 - na + 1:k0 + 1][:, ::-1], blocks[:, k0 - na:k0][:, ::-1]], axis=-1)
    cmp_rows = shifted_rows(vec, SUBLANES, CMP_STRIDE).reshape(NSA_H, 1, na * SUBLANES, Q_TILE)
    u = np.arange(na * SUBLANES)[None, :, None]
    mc = group_layout(cmp_rows, tq - CMP_STRIDE * (u - (nc - SUBLANES)) - (CMP_LEN - 1) >= 0)[:, 0]
    return tz, wz, mc


def _nsa_kernel(qt_ref, gt_ref, kc_ref, vct_ref, ksl_ref, kwn_ref, vslt_ref, vwnt_ref, mc_ref, tz_ref, wz_ref,
                ind_ref, expand_ref, o_ref, imp_scr, y_scr, *group_scr, nc, nsb):
    per_group = len(group_scr) // NSA_G
    SLC, WIN_BRANCH = 0, 1

    def branch_scr(g, branch):
        m_scr, acc_scr, s_a, s_b, max_a, max_b = group_scr[g * per_group + 3 + 6 * branch:
                                                           g * per_group + 9 + 6 * branch]
        return m_scr, acc_scr, (s_a, s_b), (max_a, max_b)

    step = pl.program_id(1)
    tiles = [QT_STEP * step + a for a in range(QT_STEP)]
    t0 = step * STEP_Q
    n_sel = min(N_SEL, nsb)
    cmp_per_q = Q_TILE // CMP_STRIDE
    cmp_per_slc = SLC_LEN // CMP_STRIDE
    pair = 2 * K_TILE
    n_last = tiles[-1] // 2
    ones_rows = (lax.broadcasted_iota(jnp.int32, (SLAB, pair), 0) == 0).astype(BF)

    def lanes(parts):
        return jnp.concatenate(parts, axis=1)

    def per_head_lanes(x):
        return lanes([x[:, a * Q_TILE:(a + 1) * Q_TILE] for a in range(QT_STEP) for _ in range(NSA_R)])

    def group_rows(g):
        return slice(g * NSA_DH, (g + 1) * NSA_DH)

    def fill(branch, kp, buf):
        k0 = pl.multiple_of(jnp.maximum(kp, 0) * pair, pair)
        for g in range(NSA_G):
            qt_scr, slab_scr = group_scr[g * per_group:g * per_group + 2]
            _, _, s_bufs, max_bufs = branch_scr(g, branch)
            if branch == SLC:
                lhs = jnp.concatenate([ksl_ref[0, pl.ds(k0, pair), :], ind_ref[...]], axis=1)
                slab = per_head_lanes(slab_scr[pl.ds(pl.multiple_of(kp * SLAB, SLAB), SLAB), :])
                rhs = jnp.concatenate([qt_scr[...], slab, jnp.zeros((KV_W - SLAB, STEP_LANES), BF)], axis=0)
                bias = [lanes([tz_ref[g, jnp.clip(qi - (2 * kp + j), -1, N_TOEPLITZ - 1) + 1] for qi in tiles])
                        for j in range(2)]
            else:
                lhs, rhs = kwn_ref[0, pl.ds(k0, pair), :], qt_scr[...]
                bias = [lanes([wz_ref[g, jnp.where(kp < 0, 0, 2 * kp + j - (qi - (N_WIN_TILES - 1)) + 1)]
                               for qi in tiles]) for j in range(2)]
            s = _dot(lhs, rhs) + jnp.concatenate(bias, axis=0)
            s_bufs[buf][...] = s
            max_bufs[buf][0:1, :] = jnp.max(s, axis=0, keepdims=True)

    def drain(branch, kp, buf):
        vt_ref = vslt_ref if branch == SLC else vwnt_ref
        k0 = pl.multiple_of(jnp.maximum(kp, 0) * pair, pair)
        for g in range(NSA_G):
            m_scr, acc_scr, s_bufs, max_bufs = branch_scr(g, branch)
            m_prev = m_scr[0:1, :]
            m_new = jnp.maximum(m_prev, max_bufs[buf][0:1, :])
            p = jnp.exp2(s_bufs[buf][...] - m_new).astype(BF)
            vt = jnp.concatenate([vt_ref[0, group_rows(g), pl.ds(k0, pair)], ones_rows], axis=0)
            acc_scr[...] = jnp.exp2(m_prev - m_new) * acc_scr[...] + _dot(vt, p)
            m_scr[0:1, :] = m_new

    def result(g, branch):
        _, acc_scr, _, _ = branch_scr(g, branch)
        return acc_scr[0:NSA_DH] * (1.0 / acc_scr[NSA_DH:NSA_DH + 1])

    lane = lax.broadcasted_iota(jnp.int32, (1, STEP_LANES), 1)
    t_lane = t0 + (lane // GR_LANES) * Q_TILE + lane % Q_TILE
    block_scores = []
    for g in range(NSA_G):
        qt_scr = group_scr[g * per_group]
        ocmp_scr = group_scr[g * per_group + 2]
        zeros = jnp.zeros((NSA_DH, Q_TILE), BF)
        parts = []
        for a in range(QT_STEP):
            for r in range(NSA_R):
                h = g * NSA_R + r
                qh = qt_ref[0, h * NSA_DH:(h + 1) * NSA_DH, a * Q_TILE:(a + 1) * Q_TILE]
                parts.append(jnp.concatenate([qh, zeros] if g == 0 else [zeros, qh], axis=0))
        qt = lanes(parts)
        qt_scr[...] = qt

        bias_c = lanes([mc_ref[g, pl.ds(pl.multiple_of((nc - SUBLANES) - cmp_per_q * qi, SUBLANES), nc), :]
                        for qi in tiles])
        s = _dot(kc_ref[0], qt) + bias_c
        m = jnp.max(s, axis=0, keepdims=True)
        e = jnp.exp2(s - m)
        lsum = jnp.sum(e, axis=0, keepdims=True)
        inv = jnp.where(t_lane >= CMP_LEN - 1, 1.0 / jnp.maximum(lsum, TINY), 0.0)
        p = e * inv
        ocmp_scr[...] = _dot(vct_ref[0, group_rows(g), :], p.astype(BF))
        p_slc = []
        for a in range(QT_STEP):
            first = a * GR_LANES
            imp = p[:, first:first + Q_TILE]
            for r in range(1, NSA_R):
                imp = imp + p[:, first + r * Q_TILE:first + (r + 1) * Q_TILE]
            imp_scr[a, 0:SUBLANES, :] = jnp.zeros((SUBLANES, Q_TILE), F32)
            imp_scr[a, SUBLANES:SUBLANES + nc, :] = imp
            a_k = [imp_scr[a, pl.ds(SUBLANES - 1 + k, nsb, stride=cmp_per_slc), :] for k in range(cmp_per_slc + 1)]
            acc = a_k[0] + a_k[1]
            for k in range(1, cmp_per_slc):
                acc = acc + a_k[k] + a_k[k + 1]
            p_slc.append(acc)

        block_scores.append(lanes(p_slc))
        for branch in (SLC, WIN_BRANCH):
            m_scr, acc_scr, _, _ = branch_scr(g, branch)
            m_scr[...] = jnp.full_like(m_scr, NEG_INF)
            acc_scr[...] = jnp.zeros_like(acc_scr)

    jidx = lax.broadcasted_iota(jnp.int32, (nsb, STEP_Q), 0)
    tq = t0 + lax.broadcasted_iota(jnp.int32, (nsb, STEP_Q), 1)
    valid = jidx * SLC_LEN <= tq

    def store_masks(masks):
        for g in range(NSA_G):
            group_scr[g * per_group + 1][...] = _dot(expand_ref[...], masks[g].astype(BF)).astype(BF)

    n_forced = 3
    few_blocks_steps = (n_sel * SLC_LEN) // STEP_Q

    @pl.when(step < few_blocks_steps)
    def _():
        store_masks([jnp.where(valid, 0.0, NEG_INF)] * NSA_G)

    @pl.when(step >= few_blocks_steps)
    def _():
        cur = tq // SLC_LEN
        forced = (jidx == 0) | (jidx == cur) | (jidx == cur - 1)
        masks = []
        for g in range(NSA_G):
            score = jnp.where(forced, -jnp.inf, jnp.where(valid, block_scores[g], NEG_INF))
            for _ in range(n_sel - n_forced):
                mx = jnp.max(score, axis=0, keepdims=True)
                first = jnp.min(jnp.where(score == mx, jidx, nsb), axis=0, keepdims=True)
                score = jnp.where(jidx == first, -jnp.inf, score)
            masks.append(jnp.where(score == -jnp.inf, 0.0, NEG_INF))
        store_masks(masks)

    win_first = n_last - (N_WIN_TILES // 2)
    fill(SLC, jnp.int32(0), 0)

    def slc_body(j, carry):
        kp = 2 * j
        fill(SLC, kp + 1, 1)
        drain(SLC, kp, 0)
        fill(SLC, kp + 2, 0)
        drain(SLC, kp + 1, 1)
        return carry

    lax.fori_loop(0, n_last // 2, slc_body, 0)

    @pl.when(n_last % 2 == 1)
    def _():
        fill(SLC, n_last, 1)
        fill(WIN_BRANCH, win_first, 0)
        drain(SLC, n_last - 1, 0)
        drain(SLC, n_last, 1)

    @pl.when(n_last % 2 == 0)
    def _():
        fill(WIN_BRANCH, win_first, 0)
        drain(SLC, n_last, 0)

    fill(WIN_BRANCH, win_first + 1, 1)
    drain(WIN_BRANCH, win_first, 0)
    fill(WIN_BRANCH, win_first + 2, 0)
    drain(WIN_BRANCH, win_first + 1, 1)
    drain(WIN_BRANCH, win_first + 2, 0)

    for g in range(NSA_G):
        def gate(branch):
            return lanes([gt_ref[0, pl.ds((g * NSA_R + r) * 3 + branch, 1), a * Q_TILE:(a + 1) * Q_TILE]
                          for a in range(QT_STEP) for r in range(NSA_R)])

        yt = (gate(0) * group_scr[g * per_group + 2][...] + gate(1) * result(g, SLC)
              + gate(2) * result(g, WIN_BRANCH))
        for a in range(QT_STEP):
            for r in range(NSA_R):
                h = g * NSA_R + r
                first = (a * NSA_R + r) * Q_TILE
                y_scr[h * NSA_DH:(h + 1) * NSA_DH, a * Q_TILE:(a + 1) * Q_TILE] = yt[:, first:first + Q_TILE]

    o_ref[0] = y_scr[...].T


def _nsa(qt, gt, kc, vct, ksw, vt, tz, wz, mc):
    b, _, l = qt.shape
    nc = l // CMP_STRIDE
    nsb = l // SLC_LEN
    per_b = lambda shape, idx: pl.BlockSpec((1,) + shape, lambda bi, i: (bi,) + idx, pipeline_mode=pl.Buffered(1))
    assert l % STEP_Q == 0
    pair = 2 * K_TILE
    blocks_per_pair = pair // SLC_LEN
    n_pairs = l // pair
    ind = (np.arange(pair)[:, None] // SLC_LEN == np.arange(KV_W)[None, :]).astype(np.float32)
    expand = np.zeros((n_pairs * SLAB, nsb), np.float32)
    for k in range(n_pairs):
        for j in range(blocks_per_pair):
            expand[k * SLAB + j, k * blocks_per_pair + j] = 1.0
    ind, expand = jnp.asarray(ind, BF), jnp.asarray(expand, BF)
    return pl.pallas_call(
        functools.partial(_nsa_kernel, nc=nc, nsb=nsb),
        grid=(b, l // STEP_Q),
        in_specs=[pl.BlockSpec((1, NSA_W, STEP_Q), lambda bi, i: (bi, 0, i)),
                  pl.BlockSpec((1, 32, STEP_Q), lambda bi, i: (bi, 0, i)),
                  per_b((nc, KV_W), (0, 0)), per_b((KV_W, nc), (0, 0)),
                  per_b((l, KV_W), (0, 0)), per_b((l, KV_W), (0, 1)),
                  per_b((KV_W, l), (0, 0)), per_b((KV_W, l), (1, 0)),
                  _const_spec(mc.shape), _const_spec(tz.shape), _const_spec(wz.shape),
                  _const_spec(ind.shape), _const_spec(expand.shape)],
        out_specs=pl.BlockSpec((1, STEP_Q, NSA_W), lambda bi, i: (bi, i, 0)),
        out_shape=jax.ShapeDtypeStruct((b, l, NSA_W), F32),
        scratch_shapes=[pltpu.VMEM((QT_STEP, nc + 4 * SUBLANES, Q_TILE), F32), pltpu.VMEM((NSA_W, STEP_Q), F32)]
                       + NSA_G * ([pltpu.VMEM((KV_W, STEP_LANES), BF), pltpu.VMEM((n_pairs * SLAB, STEP_Q), BF),
                                   pltpu.VMEM((NSA_DH, STEP_LANES), F32)]
                                  + 2 * [pltpu.VMEM((SUBLANES, STEP_LANES), F32),
                                         pltpu.VMEM((NSA_DH + SLAB, STEP_LANES), F32),
                                         pltpu.VMEM((pair, STEP_LANES), F32), pltpu.VMEM((pair, STEP_LANES), F32),
                                         pltpu.VMEM((SUBLANES, STEP_LANES), F32),
                                         pltpu.VMEM((SUBLANES, STEP_LANES), F32)]),
        compiler_params=_cparams("arbitrary", "arbitrary"),
        name="nsa",
    )(qt, gt, kc, vct, ksw, ksw, vt, vt, mc, tz, wz, ind, expand)


def _split_w_in(w_in):
    o_q = N_TOK_A
    o_kv = o_q + NSA_W
    o_gate = o_kv + 6 * KV_W
    kv = lambda j: w_in[:, o_kv + j * KV_W:o_kv + (j + 1) * KV_W]
    w_tok = jnp.concatenate([w_in[:, :N_TOK_A], kv(0), kv(1), kv(2), kv(4)], axis=1).astype(BF)
    gates = w_in[:, o_gate:]
    pad = jnp.zeros((w_in.shape[0], 32 - gates.shape[1]), w_in.dtype)
    w_t = jnp.concatenate([w_in[:, o_q:o_kv], kv(3), kv(5), gates, pad], axis=1).T.astype(BF)
    return w_tok, w_t


def kernel(x, ffn1_norm, ffn1_w_gate, ffn1_w_up, ffn1_w_down, mix_norm, w_in, w_out, s5_lambda_re, s5_lambda_im, s5_log_dt, s5_b_re, s5_b_im, s5_c_re, s5_c_im, s5_d, s5_w_glu, hgrn_lb_logits, hgrn_norm, nsa_cmp_pos_k, nsa_cmp_w1_k, nsa_cmp_w2_k, nsa_cmp_pos_v, nsa_cmp_w1_v, nsa_cmp_w2_v, rel_bias, ffn2_norm, ffn2_w_gate, ffn2_w_up, ffn2_w_down, final_norm):
    b, l, d = x.shape
    depth = w_in.shape[0]
    gam = jax.nn.softmax(hgrn_lb_logits.astype(F32), axis=0)
    lower_bounds = jnp.cumsum(gam, axis=0) - gam[0:1]
    tz, wz, mc = _bias_tables(rel_bias, l)
    bf = lambda w: w.astype(BF)

    x2 = x.reshape(b * l, d)
    for i in range(depth):
        x2 = _ffn(x2, ffn1_norm[i], bf(ffn1_w_gate[i]), bf(ffn1_w_up[i]), bf(ffn1_w_down[i]))
        w_tok, w_t = _split_w_in(w_in[i])
        za, kcr, vcr, ksw, qt, vt, gt = _inproj(x2.reshape(b, l, d), mix_norm[i], w_tok, w_t)
        bblk, cblk, coef = _s5_params(s5_lambda_re[i], s5_lambda_im[i], s5_log_dt[i], s5_b_re[i], s5_b_im[i],
                                      s5_c_re[i], s5_c_im[i])
        y_s5 = _s5(za, bblk, cblk, coef, s5_d[i], s5_w_glu[i])
        y_hg = _hgrn(za, lower_bounds[i], hgrn_norm[i])
        kc, vct = _compress(kcr, vcr,
                            _compress_params(nsa_cmp_pos_k[i], nsa_cmp_w1_k[i], nsa_cmp_w2_k[i]),
                            _compress_params(nsa_cmp_pos_v[i], nsa_cmp_w1_v[i], nsa_cmp_w2_v[i], transpose_out=True))
        y_nsa = _nsa(qt, gt, kc, vct, ksw, vt, tz, wz, mc)
        proj = (y_s5.reshape(b * l, S5_W), y_hg.reshape(b * l, HG_W), y_nsa.reshape(b * l, NSA_W), bf(w_out[i]))
        x2 = _ffn(x2, ffn2_norm[i], bf(ffn2_w_gate[i]), bf(ffn2_w_up[i]), bf(ffn2_w_down[i]), proj=proj,
                  final_g=final_norm if i + 1 == depth else None)
    return x2.reshape(b, l, d)
```

```python
import functools
import math

import numpy as np
import jax
import jax.numpy as jnp
from jax import lax
from jax.experimental import pallas as pl
from jax.experimental.pallas import tpu as pltpu

BF = jnp.bfloat16
F32 = jnp.float32

EPS = 1e-6
NEG_INF = -1e30
TINY = 1e-30
LOG2E = math.log2(math.e)

S5_W = 256
S5_G = 16
S5_GROUP = 16
S5_P = 64
S5_N = S5_G * S5_P
HG_W = 256
HG_HEADS = 4
HG_D = 64
NSA_W = 512
NSA_DH = 64
NSA_H = 8
NSA_G = 2
NSA_R = 4
KV_W = NSA_G * NSA_DH
CMP_LEN = 32
CMP_STRIDE = 16
SLC_LEN = 64
N_SEL = 16
WIN = 512
REL_BUCKETS = 32
REL_MAX_DIST = 1024

Q_TILE = 128
K_TILE = 128
GR_LANES = NSA_R * Q_TILE
QT_STEP = 4
STEP_Q = QT_STEP * Q_TILE
STEP_LANES = QT_STEP * GR_LANES
N_TOEPLITZ = REL_MAX_DIST // K_TILE + 2
N_WIN_TILES = WIN // K_TILE + 1
SLAB = 16
HG_CHUNK = 64
HG_LEVELS = 6
SUBLANES = 8
HG_FINE_LEVELS = 3
VMEM_LIMIT_BYTES = 56 * 1024 * 1024


def _cparams(*sem):
    return pltpu.CompilerParams(dimension_semantics=sem, vmem_limit_bytes=VMEM_LIMIT_BYTES)


def _const_spec(shape):
    nd = len(shape)
    return pl.BlockSpec(shape, lambda *_: (0,) * nd, pipeline_mode=pl.Buffered(1))


def _rms(x, g_row):
    ms = jnp.mean(x * x, axis=-1, keepdims=True)
    return x * lax.rsqrt(ms + EPS) * g_row


def _silu(x):
    return x * jax.nn.sigmoid(x)


def _gelu_tanh(x):
    return 0.5 * x * (1.0 + jnp.tanh(math.sqrt(2.0 / math.pi) * (x + 0.044715 * (x * x * x))))


def _dot(a, b):
    return jnp.dot(a, b, preferred_element_type=F32)


def _dot_nt(a, b):
    return lax.dot_general(a, b, (((1,), (1,)), ((), ())), preferred_element_type=F32)


def _dot_tn(a, b):
    return lax.dot_general(a, b, (((0,), (0,)), ((), ())), preferred_element_type=F32)


def _dot_exact_lhs(c_bf, x):
    hi = x.astype(BF)
    r1 = x - hi.astype(F32)
    mid = r1.astype(BF)
    lo = (r1 - mid.astype(F32)).astype(BF)
    return _dot(c_bf, hi) + _dot(c_bf, mid) + _dot(c_bf, lo)


def _dot_exact_rhs(x, c_bf):
    hi = x.astype(BF)
    r1 = x - hi.astype(F32)
    mid = r1.astype(BF)
    lo = (r1 - mid.astype(F32)).astype(BF)
    return _dot(hi, c_bf) + _dot(mid, c_bf) + _dot(lo, c_bf)


def _ffn_kernel(*refs, n_chunks, tf, with_proj, with_final):
    it = iter(refs)
    x_ref = next(it)
    if with_proj:
        ys5_ref, yhg_ref, ynsa_ref, wo_ref = next(it), next(it), next(it), next(it)
    g_ref, wg_ref, wu_ref, wd_ref = next(it), next(it), next(it), next(it)
    if with_final:
        fg_ref = next(it)
    o_ref, h_scr, a_scr = next(it), next(it), next(it)

    x = x_ref[...]
    if with_proj:
        x = (x + _dot(ys5_ref[...].astype(BF), wo_ref[0:S5_W, :])
             + _dot(yhg_ref[...].astype(BF), wo_ref[S5_W:S5_W + HG_W, :])
             + _dot(ynsa_ref[...].astype(BF), wo_ref[S5_W + HG_W:, :]))
    h_scr[...] = _rms(x, g_ref[...]).astype(BF)
    for c in range(n_chunks):
        sl = slice(c * tf, (c + 1) * tf)
        h = h_scr[...]
        gate = _dot(h, wg_ref[:, sl])
        up = _dot(h, wu_ref[:, sl])
        a_scr[:, sl] = (_silu(gate) * up).astype(BF)
    x = x + 0.5 * _dot(a_scr[...], wd_ref[...])
    if with_final:
        x = _rms(x, fg_ref[...])
    o_ref[...] = x


def _ffn(x2, g, wg, wu, wd, proj=None, final_g=None, tm=512, tf=256):
    n, d = x2.shape
    dff = wg.shape[1]
    assert n % tm == 0 and dff % tf == 0
    row = lambda i: (i, 0)
    in_specs = [pl.BlockSpec((tm, d), row)]
    args = [x2]
    if proj is not None:
        ys5, yhg, ynsa, wo = proj
        in_specs += [pl.BlockSpec((tm, S5_W), row), pl.BlockSpec((tm, HG_W), row),
                     pl.BlockSpec((tm, NSA_W), row), _const_spec(wo.shape)]
        args += [ys5, yhg, ynsa, wo]
    in_specs += [_const_spec((1, d)), _const_spec(wg.shape), _const_spec(wu.shape), _const_spec(wd.shape)]
    args += [g.reshape(1, d), wg, wu, wd]
    if final_g is not None:
        in_specs.append(_const_spec((1, d)))
        args.append(final_g.reshape(1, d))
    kern = functools.partial(_ffn_kernel, n_chunks=dff // tf, tf=tf,
                             with_proj=proj is not None, with_final=final_g is not None)
    return pl.pallas_call(
        kern,
        grid=(n // tm,),
        in_specs=in_specs,
        out_specs=pl.BlockSpec((tm, d), row),
        out_shape=jax.ShapeDtypeStruct((n, d), F32),
        scratch_shapes=[pltpu.VMEM((tm, d), BF), pltpu.VMEM((tm, dff), BF)],
        compiler_params=_cparams("arbitrary"),
        name="ffn",
    )(*args)


N_TOK_A = S5_W + 4 * HG_W
N_T_ROWS = NSA_W + 2 * KV_W + 32


def _inproj_kernel(x_ref, g_ref, wtok_ref, wt_ref, za_ref, kcr_ref, vcr_ref, ksw_ref, qt_ref, vt_ref, gt_ref):
    h = _rms(x_ref[0], g_ref[...]).astype(BF)
    for c in range(N_TOK_A // 256):
        sl = slice(c * 256, (c + 1) * 256)
        za_ref[0, :, sl] = _dot(h, wtok_ref[:, sl])
    kcr_ref[0] = _dot(h, wtok_ref[:, N_TOK_A:N_TOK_A + KV_W])
    vcr_ref[0] = _dot(h, wtok_ref[:, N_TOK_A + KV_W:N_TOK_A + 2 * KV_W])
    ksw_ref[0] = _dot(h, wtok_ref[:, N_TOK_A + 2 * KV_W:N_TOK_A + 4 * KV_W]).astype(BF)
    qt_ref[0] = (_dot_nt(wt_ref[0:NSA_W, :], h) * (NSA_DH ** -0.5 * LOG2E)).astype(BF)
    vt_ref[0] = _dot_nt(wt_ref[NSA_W:NSA_W + 2 * KV_W, :], h).astype(BF)
    gt_ref[0] = jax.nn.sigmoid(_dot_nt(wt_ref[NSA_W + 2 * KV_W:, :], h))


def _inproj(x3, g, w_tok, w_t, tm=512):
    b, l, d = x3.shape
    assert l % tm == 0
    tok = lambda w: pl.BlockSpec((1, tm, w), lambda bi, i: (bi, i, 0))
    tr = lambda r: pl.BlockSpec((1, r, tm), lambda bi, i: (bi, 0, i))
    return pl.pallas_call(
        _inproj_kernel,
        grid=(b, l // tm),
        in_specs=[tok(d), _const_spec((1, d)), _const_spec(w_tok.shape), _const_spec(w_t.shape)],
        out_specs=[tok(N_TOK_A), tok(KV_W), tok(KV_W), tok(2 * KV_W), tr(NSA_W), tr(2 * KV_W), tr(32)],
        out_shape=[jax.ShapeDtypeStruct((b, l, N_TOK_A), F32),
                   jax.ShapeDtypeStruct((b, l, KV_W), F32),
                   jax.ShapeDtypeStruct((b, l, KV_W), F32),
                   jax.ShapeDtypeStruct((b, l, 2 * KV_W), BF),
                   jax.ShapeDtypeStruct((b, NSA_W, l), BF),
                   jax.ShapeDtypeStruct((b, 2 * KV_W, l), BF),
                   jax.ShapeDtypeStruct((b, 32, l), F32)],
        compiler_params=_cparams("arbitrary", "arbitrary"),
        name="inproj",
    )(x3, g.reshape(1, d), w_tok, w_t)


def _s5_kernel(u_ref, bblk_ref, cblk_ref, coef_ref, d_ref, wglu_ref, y_ref, xs_scr, carry_scr, *, tc):
    @pl.when(pl.program_id(1) == 0)
    def _():
        carry_scr[...] = jnp.zeros_like(carry_scr)

    u = u_ref[0]
    xs_scr[...] = _dot(u.astype(BF), bblk_ref[...])

    def body(r, carry):
        cre, cim = carry
        row = pl.multiple_of(r * SUBLANES, SUBLANES)
        xre = xs_scr[pl.ds(row, SUBLANES), 0:S5_N]
        xim = xs_scr[pl.ds(row, SUBLANES), S5_N:2 * S5_N]
        for idx, k in enumerate((1, 2, 4)):
            are, aim = coef_ref[idx, 0], coef_ref[idx, 1]
            sre, sim = pltpu.roll(xre, k, 0), pltpu.roll(xim, k, 0)
            xre, xim = xre + (are * sre - aim * sim), xim + (are * sim + aim * sre)
        pre, pim = coef_ref[3, 0], coef_ref[3, 1]
        xre, xim = xre + (pre * cre - pim * cim), xim + (pre * cim + pim * cre)
        xs_scr[pl.ds(row, SUBLANES), 0:S5_N] = xre
        xs_scr[pl.ds(row, SUBLANES), S5_N:2 * S5_N] = xim
        return xre[SUBLANES - 1:SUBLANES], xim[SUBLANES - 1:SUBLANES]

    cre, cim = lax.fori_loop(0, tc // SUBLANES, body, (carry_scr[0:1], carry_scr[1:2]))
    carry_scr[0:1] = cre
    carry_scr[1:2] = cim

    y = _dot(xs_scr[...].astype(BF), cblk_ref[...]) + d_ref[...] * u
    y = _gelu_tanh(y)
    y_ref[0] = y * jax.nn.sigmoid(_dot(y.astype(BF), wglu_ref[...]))


def _s5_params(lam_re, lam_im, log_dt, b_re, b_im, c_re, c_im):
    lr, li = lam_re.astype(F32), lam_im.astype(F32)
    dt = jnp.exp(log_dt.astype(F32))[:, None]
    mag = jnp.exp(lr * dt)
    ab_re, ab_im = mag * jnp.cos(li * dt), mag * jnp.sin(li * dt)
    den = lr * lr + li * li
    nr, ni = ab_re - 1.0, ab_im
    g_re = (nr * lr + ni * li) / den
    g_im = (ni * lr - nr * li) / den
    br, bi = b_re.astype(F32), b_im.astype(F32)
    bb_re = g_re[..., None] * br - g_im[..., None] * bi
    bb_im = g_re[..., None] * bi + g_im[..., None] * br
    eye = jnp.eye(S5_G, dtype=F32)
    blk = lambda w: jnp.einsum('gph,gk->ghkp', w, eye).reshape(S5_W, S5_N)
    bblk = jnp.concatenate([blk(bb_re), blk(bb_im)], axis=1).astype(BF)
    cblk_f = lambda w: jnp.einsum('ghp,gk->gpkh', w, eye).reshape(S5_N, S5_W)
    cblk = jnp.concatenate([cblk_f(c_re.astype(F32)), -cblk_f(c_im.astype(F32))], axis=0).astype(BF)
    are, aim = ab_re.reshape(1, S5_N), ab_im.reshape(1, S5_N)
    pw = [(are, aim)]
    for _ in range(SUBLANES - 1):
        pr, pi = pw[-1]
        pw.append((pr * are - pi * aim, pr * aim + pi * are))
    rows = np.arange(SUBLANES)[:, None]
    coef = []
    for k in (1, 2, 4):
        m = jnp.asarray((rows >= k).astype(np.float32))
        coef.append(jnp.stack([m * pw[k - 1][0], m * pw[k - 1][1]]))
    coef.append(jnp.stack([jnp.concatenate([p[0] for p in pw], axis=0), jnp.concatenate([p[1] for p in pw], axis=0)]))
    return bblk, cblk, jnp.stack(coef)


def _s5(za, bblk, cblk, coef, d, w_glu, tc=512):
    b, l, _ = za.shape
    assert l % tc == 0
    return pl.pallas_call(
        functools.partial(_s5_kernel, tc=tc),
        grid=(b, l // tc),
        in_specs=[pl.BlockSpec((1, tc, S5_W), lambda bi, i: (bi, i, 0)),
                  _const_spec(bblk.shape), _const_spec(cblk.shape), _const_spec(coef.shape),
                  _const_spec((1, S5_W)), _const_spec(w_glu.shape)],
        out_specs=pl.BlockSpec((1, tc, S5_W), lambda bi, i: (bi, i, 0)),
        out_shape=jax.ShapeDtypeStruct((b, l, S5_W), F32),
        scratch_shapes=[pltpu.VMEM((tc, 2 * S5_N), F32), pltpu.VMEM((2, S5_N), F32)],
        compiler_params=_cparams("arbitrary", "arbitrary"),
        name="s5",
    )(za, bblk, cblk, coef, d.reshape(1, S5_W).astype(F32), w_glu.astype(BF))


def _hgrn_constants():
    c = HG_CHUNK
    t = np.arange(c)[:, None]
    u = np.arange(c)[None, :]
    mats = [(u <= t)]
    masks = []
    for lv in range(HG_LEVELS):
        n = c >> lv
        half = n // 2
        ref = (t // n) * n + half - 1
        lower = (t % n) >= half
        if half < SUBLANES:
            mats.append(np.where(lower, (u > ref) & (u <= t), (u > t) & (u <= ref)))
        same = (t // n) == (u // n)
        masks.append(same & lower & ((u % n) < half))
    masks.append(t == u)
    gall = np.concatenate(mats, axis=0).astype(np.float32)
    mstk = np.stack([np.tile(m, (1, HG_HEADS)) for m in masks]).astype(np.float32)
    lane_head = np.arange(HG_W)[None, :] // HG_D
    hmask = (np.repeat(np.arange(HG_HEADS), c)[:, None] == lane_head).astype(np.float32)
    bd = (np.arange(HG_W)[:, None] // HG_D == lane_head).astype(np.float32)
    return gall, mstk, hmask, bd


def _hgrn_kernel(q_ref, f_ref, i_ref, g_ref, lb_ref, gain_ref, gall_ref, mstk_ref, hmask_ref, bd_ref, bdn_ref,
                 o_ref, st_scr, *, tt):
    c = HG_CHUNK

    @pl.when(pl.program_id(1) == 0)
    def _():
        st_scr[...] = jnp.zeros_like(st_scr)

    lb = lb_ref[...]
    hmask = hmask_ref[...]

    def level_decay(b, sums, lv):
        n = c >> lv
        half = n // 2
        if half < SUBLANES:
            fine = lv - (HG_LEVELS - HG_FINE_LEVELS)
            return jnp.exp(sums[(1 + fine) * c:(2 + fine) * c])
        pieces = []
        for blk in range(c // n):
            ref = blk * n + half - 1
            pieces.append(b[ref:ref + 1] - b[blk * n:blk * n + half])
            pieces.append(b[blk * n + half:(blk + 1) * n] - b[ref:ref + 1])
        return jnp.exp(jnp.concatenate(pieces, axis=0))

    def chunk(row, fl, sums, carry):
        qf = _silu(q_ref[0, pl.ds(row, c), :])
        kf = (1.0 - lb) * jax.nn.sigmoid(-fl)
        v = i_ref[0, pl.ds(row, c), :]
        v_bf = v.astype(BF)
        b = sums[0:c]
        e_b = jnp.exp(b)
        e_suf = jnp.exp(b[c - 1:c] - b)

        att = jnp.zeros((c, HG_HEADS * c), F32)
        for lv in range(HG_LEVELS + 1):
            if lv < HG_LEVELS:
                e = level_decay(b, sums, lv)
                z, w = qf * e, kf * e
            else:
                z, w = qf, kf
            ws = (jnp.concatenate([w] * HG_HEADS, axis=0) * hmask).astype(BF)
            att = att + mstk_ref[lv] * _dot_nt(z.astype(BF), ws)
        o = _dot(att.astype(BF), (jnp.concatenate([v] * HG_HEADS, axis=0) * hmask).astype(BF))

        st = st_scr[...]
        o = o + _dot_nt((qf * e_b).astype(BF), st.astype(BF))
        st_scr[...] = e_b[c - 1:c] * st + bd_ref[...] * _dot_tn(v_bf, (kf * e_suf).astype(BF))

        ms = _dot_exact_rhs(o * o, bdn_ref[...])
        o = o * lax.rsqrt(ms + EPS) * gain_ref[...]
        o_ref[0, pl.ds(row, c), :] = o * _silu(g_ref[0, pl.ds(row, c), :])
        return carry

    def chunk_pair(cp, carry):
        rows = [pl.multiple_of((2 * cp + j) * c, c) for j in range(2)]
        fls = [f_ref[0, pl.ds(r, c), :] for r in rows]
        lfs = [jnp.log(jnp.maximum(lb + (1.0 - lb) * jax.nn.sigmoid(fl), TINY)) for fl in fls]
        sums = _dot_exact_lhs(gall_ref[...], jnp.concatenate(lfs, axis=1))
        for j in range(2):
            carry = chunk(rows[j], fls[j], sums[:, j * HG_W:(j + 1) * HG_W], carry)
        return carry

    lax.fori_loop(0, tt // (2 * c), chunk_pair, 0)


def _hgrn(za, lb, gain, tt=512):
    b, l, _ = za.shape
    assert l % tt == 0
    gall, mstk, hmask, bd = _hgrn_constants()
    col = lambda j: pl.BlockSpec((1, tt, HG_W), lambda bi, i: (bi, i, j))
    consts = [jnp.asarray(gall, BF), jnp.asarray(mstk, F32), jnp.asarray(hmask, F32), jnp.asarray(bd, F32),
              jnp.asarray(bd / HG_D, BF)]
    return pl.pallas_call(
        functools.partial(_hgrn_kernel, tt=tt),
        grid=(b, l // tt),
        in_specs=[col(1), col(2), col(3), col(4), _const_spec((1, HG_W)), _const_spec((1, HG_W))]
                 + [_const_spec(x.shape) for x in consts],
        out_specs=pl.BlockSpec((1, tt, HG_W), lambda bi, i: (bi, i, 0)),
        out_shape=jax.ShapeDtypeStruct((b, l, HG_W), F32),
        scratch_shapes=[pltpu.VMEM((HG_W, HG_W), F32)],
        compiler_params=_cparams("arbitrary", "arbitrary"),
        name="hgrn2",
    )(za, za, za, za, lb.reshape(1, HG_W).astype(F32), jnp.tile(gain.astype(F32), HG_HEADS).reshape(1, HG_W),
      *consts)


def _compress_kernel(k16_ref, v16_ref, kpa_ref, kpb_ref, kw1a_ref, kw1b_ref, kw2_ref,
                     vpa_ref, vpb_ref, vw1a_ref, vw1b_ref, vw2_ref, kc_ref, vct_ref, *, nc):
    def hidden(x_ref, pa, pb, w1a, w1b):
        x16 = jnp.concatenate([x_ref[0, pl.ds(j, nc, stride=CMP_STRIDE), :] for j in range(CMP_STRIDE)], axis=1)
        first = _dot((x16 + pa).astype(BF), w1a)
        second = _dot((x16 + pb).astype(BF), w1b)
        pre = first + pltpu.roll(second, nc - 1, 0)
        rows = lax.broadcasted_iota(jnp.int32, pre.shape, 0)
        pre = jnp.where(rows < nc - 1, pre, 0.0)
        return _gelu_tanh(pre).astype(BF)

    hk = hidden(k16_ref, kpa_ref[...], kpb_ref[...], kw1a_ref[...], kw1b_ref[...])
    kc_ref[0] = _dot(hk, kw2_ref[...]).astype(BF)
    hv = hidden(v16_ref, vpa_ref[...], vpb_ref[...], vw1a_ref[...], vw1b_ref[...])
    vct_ref[0] = _dot_nt(vw2_ref[...], hv).astype(BF)


def _compress_params(pos, w1, w2, transpose_out=False):
    eye = jnp.eye(NSA_G, dtype=F32)
    w1r = w1.astype(F32).reshape(CMP_LEN, NSA_DH, NSA_DH)
    wexp = jnp.einsum('jde,gh->jgdhe', w1r, eye).reshape(CMP_LEN, KV_W, KV_W)
    half = CMP_LEN // 2
    w1a = wexp[:half].reshape(half * KV_W, KV_W).astype(BF)
    w1b = wexp[half:].reshape(half * KV_W, KV_W).astype(BF)
    pt = jnp.broadcast_to(pos.astype(F32)[:, None, :], (CMP_LEN, NSA_G, NSA_DH))
    pa = pt[:half].reshape(1, half * KV_W)
    pb = pt[half:].reshape(1, half * KV_W)
    w2bd = jnp.einsum('de,gh->gdhe', w2.astype(F32), eye).reshape(KV_W, KV_W).astype(BF)
    return pa, pb, w1a, w1b, (w2bd.T if transpose_out else w2bd)


def _compress(kcr, vcr, kparams, vparams):
    b, l, _ = kcr.shape
    nc = l // CMP_STRIDE
    per_b = lambda r, cc: pl.BlockSpec((1, r, cc), lambda bi: (bi, 0, 0))
    params = list(kparams) + list(vparams)
    return pl.pallas_call(
        functools.partial(_compress_kernel, nc=nc),
        grid=(b,),
        in_specs=[per_b(l, KV_W), per_b(l, KV_W)] + [_const_spec(p.shape) for p in params],
        out_specs=[per_b(nc, KV_W), per_b(KV_W, nc)],
        out_shape=[jax.ShapeDtypeStruct((b, nc, KV_W), BF), jax.ShapeDtypeStruct((b, KV_W, nc), BF)],
        compiler_params=_cparams("arbitrary"),
        name="compress",
    )(kcr, vcr, *params)


def _t5_bucket(dist):
    n = jnp.maximum(dist, 0)
    max_exact = REL_BUCKETS // 2
    nf = jnp.maximum(n, max_exact).astype(jnp.float32)
    large = max_exact + (jnp.log(nf / max_exact) / math.log(REL_MAX_DIST / max_exact)
                         * (REL_BUCKETS - max_exact)).astype(jnp.int32)
    large = jnp.minimum(large, REL_BUCKETS - 1)
    return jnp.where(n < max_exact, n, large)


def _bias_tables(rel_bias, l):
    tab = rel_bias.astype(F32) * LOG2E
    by_dist = tab[_t5_bucket(jnp.arange(l))].T
    nc = l // CMP_STRIDE
    w = 2 * K_TILE

    def shifted_rows(v, n_rows, step):
        flat = jnp.tile(v, (1,) * (v.ndim - 1) + (n_rows,))[..., :n_rows * (w - step)]
        return flat.reshape(v.shape[:-1] + (n_rows, w - step))[..., :Q_TILE]

    def group_layout(t, valid):
        _, n, r, q = t.shape
        t = jnp.where(jnp.asarray(valid)[None], t, NEG_INF)
        return t.reshape(NSA_G, NSA_R, n, r, q).transpose(0, 2, 3, 1, 4).reshape(NSA_G, n, r, NSA_R * q)

    tq = np.arange(Q_TILE)[None, None, :]
    key = np.arange(K_TILE)[None, :, None]
    blocks = jnp.pad(by_dist, ((0, 0), (K_TILE, w)), mode='edge').reshape(NSA_H, -1, K_TILE)
    vec = jnp.concatenate([blocks[:, 1:N_TOEPLITZ + 1], blocks[:, 0:N_TOEPLITZ]], axis=-1)
    toep = shifted_rows(vec, K_TILE, 1)
    masked_tile = jnp.full((NSA_G, 1, K_TILE, GR_LANES), NEG_INF, F32)
    d = np.arange(N_TOEPLITZ)[:, None, None] * K_TILE + tq - key
    tz = jnp.concatenate([masked_tile, group_layout(toep, d >= 0)], axis=1)
    d = (N_WIN_TILES - 1 - np.arange(N_WIN_TILES))[:, None, None] * K_TILE + tq - key
    wz = group_layout(toep[:, N_WIN_TILES - 1::-1], (d >= 0) & (d < WIN))
    wz = jnp.concatenate([masked_tile, wz, masked_tile], axis=1)
    na = (2 * nc - SUBLANES) // SUBLANES
    c0 = CMP_STRIDE * (nc - SUBLANES) - (CMP_LEN - 1)
    front = K_TILE * (na + 1)
    off = c0 % K_TILE
    padded = jnp.pad(by_dist, ((0, 0), (front, w)), mode='edge')
    nb = (padded.shape[1] - off) // K_TILE
    blocks = padded[:, off:off + nb * K_TILE].reshape(NSA_H, nb, K_TILE)
    k0 = (front + c0 - off) // K_TILE
    vec = jnp.concatenate([blocks[:, k0 - na + 1:k0 + 1][:, ::-1], blocks[:, k0 - na:k0][:, ::-1]], axis=-1)
    cmp_rows = shifted_rows(vec, SUBLANES, CMP_STRIDE).reshape(NSA_H, 1, na * SUBLANES, Q_TILE)
    u = np.arange(na * SUBLANES)[None, :, None]
    mc = group_layout(cmp_rows, tq - CMP_STRIDE * (u - (nc - SUBLANES)) - (CMP_LEN - 1) >= 0)[:, 0]
    return tz, wz, mc


def _nsa_kernel(qt_ref, gt_ref, kc_ref, vct_ref, ksl_ref, kwn_ref, vslt_ref, vwnt_ref, mc_ref, tz_ref, wz_ref,
                ind_ref, expand_ref, o_ref, imp_scr, y_scr, *group_scr, nc, nsb):
    per_group = len(group_scr) // NSA_G
    SLC, WIN_BRANCH = 0, 1

    def branch_scr(g, branch):
        m_scr, acc_scr, s_a, s_b, max_a, max_b = group_scr[g * per_group + 3 + 6 * branch:
                                                           g * per_group + 9 + 6 * branch]
        return m_scr, acc_scr, (s_a, s_b), (max_a, max_b)

    step = pl.program_id(1)
    tiles = [QT_STEP * step + a for a in range(QT_STEP)]
    t0 = step * STEP_Q
    n_sel = min(N_SEL, nsb)
    cmp_per_q = Q_TILE // CMP_STRIDE
    cmp_per_slc = SLC_LEN // CMP_STRIDE
    pair = 2 * K_TILE
    n_last = tiles[-1] // 2
    ones_rows = (lax.broadcasted_iota(jnp.int32, (SLAB, pair), 0) == 0).astype(BF)

    def lanes(parts):
        return jnp.concatenate(parts, axis=1)

    def per_head_lanes(x):
        return lanes([x[:, a * Q_TILE:(a + 1) * Q_TILE] for a in range(QT_STEP) for _ in range(NSA_R)])

    def group_rows(g):
        return slice(g * NSA_DH, (g + 1) * NSA_DH)

    def fill(branch, kp, buf):
        k0 = pl.multiple_of(jnp.maximum(kp, 0) * pair, pair)
        for g in range(NSA_G):
            qt_scr, slab_scr = group_scr[g * per_group:g * per_group + 2]
            _, _, s_bufs, max_bufs = branch_scr(g, branch)
            if branch == SLC:
                lhs = jnp.concatenate([ksl_ref[0, pl.ds(k0, pair), :], ind_ref[...]], axis=1)
                slab = per_head_lanes(slab_scr[pl.ds(pl.multiple_of(kp * SLAB, SLAB), SLAB), :])
                rhs = jnp.concatenate([qt_scr[...], slab, jnp.zeros((KV_W - SLAB, STEP_LANES), BF)], axis=0)
                bias = [lanes([tz_ref[g, jnp.clip(qi - (2 * kp + j), -1, N_TOEPLITZ - 1) + 1] for qi in tiles])
                        for j in range(2)]
            else:
                lhs, rhs = kwn_ref[0, pl.ds(k0, pair), :], qt_scr[...]
                bias = [lanes([wz_ref[g, jnp.where(kp < 0, 0, jnp.clip(2 * kp + j - (qi - (N_WIN_TILES - 1)) + 1,
                                                                       0, N_WIN_TILES + 1))]
                               for qi in tiles]) for j in range(2)]
            s = _dot(lhs, rhs) + jnp.concatenate(bias, axis=0)
            s_bufs[buf][...] = s
            max_bufs[buf][0:1, :] = jnp.max(s, axis=0, keepdims=True)

    def drain(branch, kp, buf):
        vt_ref = vslt_ref if branch == SLC else vwnt_ref
        k0 = pl.multiple_of(jnp.maximum(kp, 0) * pair, pair)
        for g in range(NSA_G):
            m_scr, acc_scr, s_bufs, max_bufs = branch_scr(g, branch)
            m_prev = m_scr[0:1, :]
            m_new = jnp.maximum(m_prev, max_bufs[buf][0:1, :])
            p = jnp.exp2(s_bufs[buf][...] - m_new).astype(BF)
            vt = jnp.concatenate([vt_ref[0, group_rows(g), pl.ds(k0, pair)], ones_rows], axis=0)
            acc_scr[...] = jnp.exp2(m_prev - m_new) * acc_scr[...] + _dot(vt, p)
            m_scr[0:1, :] = m_new

    def result(g, branch):
        _, acc_scr, _, _ = branch_scr(g, branch)
        return acc_scr[0:NSA_DH] * (1.0 / acc_scr[NSA_DH:NSA_DH + 1])

    lane = lax.broadcasted_iota(jnp.int32, (1, STEP_LANES), 1)
    t_lane = t0 + (lane // GR_LANES) * Q_TILE + lane % Q_TILE
    block_scores = []
    for g in range(NSA_G):
        qt_scr = group_scr[g * per_group]
        ocmp_scr = group_scr[g * per_group + 2]
        zeros = jnp.zeros((NSA_DH, Q_TILE), BF)
        parts = []
        for a in range(QT_STEP):
            for r in range(NSA_R):
                h = g * NSA_R + r
                qh = qt_ref[0, h * NSA_DH:(h + 1) * NSA_DH, a * Q_TILE:(a + 1) * Q_TILE]
                parts.append(jnp.concatenate([qh, zeros] if g == 0 else [zeros, qh], axis=0))
        qt = lanes(parts)
        qt_scr[...] = qt

        bias_c = lanes([mc_ref[g, pl.ds(pl.multiple_of((nc - SUBLANES) - cmp_per_q * qi, SUBLANES), nc), :]
                        for qi in tiles])
        s = _dot(kc_ref[0], qt) + bias_c
        m = jnp.max(s, axis=0, keepdims=True)
        e = jnp.exp2(s - m)
        lsum = jnp.sum(e, axis=0, keepdims=True)
        inv = jnp.where(t_lane >= CMP_LEN - 1, 1.0 / jnp.maximum(lsum, TINY), 0.0)
        p = e * inv
        ocmp_scr[...] = _dot(vct_ref[0, group_rows(g), :], p.astype(BF))
        p_slc = []
        for a in range(QT_STEP):
            first = a * GR_LANES
            imp = p[:, first:first + Q_TILE]
            for r in range(1, NSA_R):
                imp = imp + p[:, first + r * Q_TILE:first + (r + 1) * Q_TILE]
            imp_scr[a, 0:SUBLANES, :] = jnp.zeros((SUBLANES, Q_TILE), F32)
            imp_scr[a, SUBLANES:SUBLANES + nc, :] = imp
            a_k = [imp_scr[a, pl.ds(SUBLANES - 1 + k, nsb, stride=cmp_per_slc), :] for k in range(cmp_per_slc + 1)]
            acc = a_k[0] + a_k[1]
            for k in range(1, cmp_per_slc):
                acc = acc + a_k[k] + a_k[k + 1]
            p_slc.append(acc)

        block_scores.append(lanes(p_slc))
        for branch in (SLC, WIN_BRANCH):
            m_scr, acc_scr, _, _ = branch_scr(g, branch)
            m_scr[...] = jnp.full_like(m_scr, NEG_INF)
            acc_scr[...] = jnp.zeros_like(acc_scr)

    jidx = lax.broadcasted_iota(jnp.int32, (nsb, STEP_Q), 0)
    tq = t0 + lax.broadcasted_iota(jnp.int32, (nsb, STEP_Q), 1)
    valid = jidx * SLC_LEN <= tq

    def store_masks(masks):
        for g in range(NSA_G):
            group_scr[g * per_group + 1][...] = _dot(expand_ref[...], masks[g].astype(BF)).astype(BF)

    n_forced = 3
    few_blocks_steps = (n_sel * SLC_LEN) // STEP_Q

    @pl.when(step < few_blocks_steps)
    def _():
        store_masks([jnp.where(valid, 0.0, NEG_INF)] * NSA_G)

    @pl.when(step >= few_blocks_steps)
    def _():
        cur = tq // SLC_LEN
        forced = (jidx == 0) | (jidx == cur) | (jidx == cur - 1)
        masks = []
        for g in range(NSA_G):
            score = jnp.where(forced, -jnp.inf, jnp.where(valid, block_scores[g], NEG_INF))
            for _ in range(n_sel - n_forced):
                mx = jnp.max(score, axis=0, keepdims=True)
                first = jnp.min(jnp.where(score == mx, jidx, nsb), axis=0, keepdims=True)
                score = jnp.where(jidx == first, -jnp.inf, score)
            masks.append(jnp.where(score == -jnp.inf, 0.0, NEG_INF))
        store_masks(masks)

    n_win_pairs = (QT_STEP + N_WIN_TILES - 1 + 1) // 2
    win_first = n_last - (n_win_pairs - 1)
    fill(SLC, jnp.int32(0), 0)

    def slc_body(j, carry):
        kp = 2 * j
        fill(SLC, kp + 1, 1)
        drain(SLC, kp, 0)
        fill(SLC, kp + 2, 0)
        drain(SLC, kp + 1, 1)
        return carry

    lax.fori_loop(0, n_last // 2, slc_body, 0)

    @pl.when(n_last % 2 == 1)
    def _():
        fill(SLC, n_last, 1)
        fill(WIN_BRANCH, win_first, 0)
        drain(SLC, n_last - 1, 0)
        drain(SLC, n_last, 1)

    @pl.when(n_last % 2 == 0)
    def _():
        fill(WIN_BRANCH, win_first, 0)
        drain(SLC, n_last, 0)

    for i in range(1, n_win_pairs):
        fill(WIN_BRANCH, win_first + i, i % 2)
        drain(WIN_BRANCH, win_first + i - 1, (i - 1) % 2)
    drain(WIN_BRANCH, win_first + n_win_pairs - 1, (n_win_pairs - 1) % 2)

    for g in range(NSA_G):
        def gate(branch):
            return lanes([gt_ref[0, pl.ds((g * NSA_R + r) * 3 + branch, 1), a * Q_TILE:(a + 1) * Q_TILE]
                          for a in range(QT_STEP) for r in range(NSA_R)])

        yt = (gate(0) * group_scr[g * per_group + 2][...] + gate(1) * result(g, SLC)
              + gate(2) * result(g, WIN_BRANCH))
        for a in range(QT_STEP):
            for r in range(NSA_R):
                h = g * NSA_R + r
                first = (a * NSA_R + r) * Q_TILE
                y_scr[h * NSA_DH:(h + 1) * NSA_DH, a * Q_TILE:(a + 1) * Q_TILE] = yt[:, first:first + Q_TILE]

    o_ref[0] = y_scr[...].T


def _nsa(qt, gt, kc, vct, ksw, vt, tz, wz, mc):
    b, _, l = qt.shape
    nc = l // CMP_STRIDE
    nsb = l // SLC_LEN
    per_b = lambda shape, idx: pl.BlockSpec((1,) + shape, lambda bi, i: (bi,) + idx, pipeline_mode=pl.Buffered(1))
    assert l % STEP_Q == 0
    pair = 2 * K_TILE
    blocks_per_pair = pair // SLC_LEN
    n_pairs = l // pair
    ind = (np.arange(pair)[:, None] // SLC_LEN == np.arange(KV_W)[None, :]).astype(np.float32)
    expand = np.zeros((n_pairs * SLAB, nsb), np.float32)
    for k in range(n_pairs):
        for j in range(blocks_per_pair):
            expand[k * SLAB + j, k * blocks_per_pair + j] = 1.0
    ind, expand = jnp.asarray(ind, BF), jnp.asarray(expand, BF)
    return pl.pallas_call(
        functools.partial(_nsa_kernel, nc=nc, nsb=nsb),
        grid=(b, l // STEP_Q),
        in_specs=[pl.BlockSpec((1, NSA_W, STEP_Q), lambda bi, i: (bi, 0, i)),
                  pl.BlockSpec((1, 32, STEP_Q), lambda bi, i: (bi, 0, i)),
                  per_b((nc, KV_W), (0, 0)), per_b((KV_W, nc), (0, 0)),
                  per_b((l, KV_W), (0, 0)), per_b((l, KV_W), (0, 1)),
                  per_b((KV_W, l), (0, 0)), per_b((KV_W, l), (1, 0)),
                  _const_spec(mc.shape), _const_spec(tz.shape), _const_spec(wz.shape),
                  _const_spec(ind.shape), _const_spec(expand.shape)],
        out_specs=pl.BlockSpec((1, STEP_Q, NSA_W), lambda bi, i: (bi, i, 0)),
        out_shape=jax.ShapeDtypeStruct((b, l, NSA_W), F32),
        scratch_shapes=[pltpu.VMEM((QT_STEP, nc + 4 * SUBLANES, Q_TILE), F32), pltpu.VMEM((NSA_W, STEP_Q), F32)]
                       + NSA_G * ([pltpu.VMEM((KV_W, STEP_LANES), BF), pltpu.VMEM((n_pairs * SLAB, STEP_Q), BF),
                                   pltpu.VMEM((NSA_DH, STEP_LANES), F32)]
                                  + 2 * [pltpu.VMEM((SUBLANES, STEP_LANES), F32),
                                         pltpu.VMEM((NSA_DH + SLAB, STEP_LANES), F32),
                                         pltpu.VMEM((pair, STEP_LANES), F32), pltpu.VMEM((pair, STEP_LANES), F32),
                                         pltpu.VMEM((SUBLANES, STEP_LANES), F32),
                                         pltpu.VMEM((SUBLANES, STEP_LANES), F32)]),
        compiler_params=_cparams("arbitrary", "arbitrary"),
        name="nsa",
    )(qt, gt, kc, vct, ksw, ksw, vt, vt, mc, tz, wz, ind, expand)


def _split_w_in(w_in):
    o_q = N_TOK_A
    o_kv = o_q + NSA_W
    o_gate = o_kv + 6 * KV_W
    kv = lambda j: w_in[:, o_kv + j * KV_W:o_kv + (j + 1) * KV_W]
    w_tok = jnp.concatenate([w_in[:, :N_TOK_A], kv(0), kv(1), kv(2), kv(4)], axis=1).astype(BF)
    gates = w_in[:, o_gate:]
    pad = jnp.zeros((w_in.shape[0], 32 - gates.shape[1]), w_in.dtype)
    w_t = jnp.concatenate([w_in[:, o_q:o_kv], kv(3), kv(5), gates, pad], axis=1).T.astype(BF)
    return w_tok, w_t


def kernel(x, ffn1_norm, ffn1_w_gate, ffn1_w_up, ffn1_w_down, mix_norm, w_in, w_out, s5_lambda_re, s5_lambda_im, s5_log_dt, s5_b_re, s5_b_im, s5_c_re, s5_c_im, s5_d, s5_w_glu, hgrn_lb_logits, hgrn_norm, nsa_cmp_pos_k, nsa_cmp_w1_k, nsa_cmp_w2_k, nsa_cmp_pos_v, nsa_cmp_w1_v, nsa_cmp_w2_v, rel_bias, ffn2_norm, ffn2_w_gate, ffn2_w_up, ffn2_w_down, final_norm):
    b, l, d = x.shape
    depth = w_in.shape[0]
    gam = jax.nn.softmax(hgrn_lb_logits.astype(F32), axis=0)
    lower_bounds = jnp.cumsum(gam, axis=0) - gam[0:1]
    tz, wz, mc = _bias_tables(rel_bias, l)
    bf = lambda w: w.astype(BF)

    x2 = x.reshape(b * l, d)
    for i in range(depth):
        x2 = _ffn(x2, ffn1_norm[i], bf(ffn1_w_gate[i]), bf(ffn1_w_up[i]), bf(ffn1_w_down[i]))
        w_tok, w_t = _split_w_in(w_in[i])
        za, kcr, vcr, ksw, qt, vt, gt = _inproj(x2.reshape(b, l, d), mix_norm[i], w_tok, w_t)
        bblk, cblk, coef = _s5_params(s5_lambda_re[i], s5_lambda_im[i], s5_log_dt[i], s5_b_re[i], s5_b_im[i],
                                      s5_c_re[i], s5_c_im[i])
        y_s5 = _s5(za, bblk, cblk, coef, s5_d[i], s5_w_glu[i])
        y_hg = _hgrn(za, lower_bounds[i], hgrn_norm[i])
        kc, vct = _compress(kcr, vcr,
                            _compress_params(nsa_cmp_pos_k[i], nsa_cmp_w1_k[i], nsa_cmp_w2_k[i]),
                            _compress_params(nsa_cmp_pos_v[i], nsa_cmp_w1_v[i], nsa_cmp_w2_v[i], transpose_out=True))
        y_nsa = _nsa(qt, gt, kc, vct, ksw, vt, tz, wz, mc)
        proj = (y_s5.reshape(b * l, S5_W), y_hg.reshape(b * l, HG_W), y_nsa.reshape(b * l, NSA_W), bf(w_out[i]))
        x2 = _ffn(x2, ffn2_norm[i], bf(ffn2_w_gate[i]), bf(ffn2_w_up[i]), bf(ffn2_w_down[i]), proj=proj,
                  final_g=final_norm if i + 1 == depth else None)
    return x2.reshape(b, l, d)
```

```python
import functools
import math

import numpy as np
import jax
import jax.numpy as jnp
from jax import lax
from jax.experimental import pallas as pl
from jax.experimental.pallas import tpu as pltpu

BF = jnp.bfloat16
F32 = jnp.float32

EPS = 1e-6
NEG_INF = -1e30
TINY = 1e-30
LOG2E = math.log2(math.e)

S5_W = 256
S5_G = 16
S5_GROUP = 16
S5_P = 64
S5_N = S5_G * S5_P
HG_W = 256
HG_HEADS = 4
HG_D = 64
NSA_W = 512
NSA_DH = 64
NSA_H = 8
NSA_G = 2
NSA_R = 4
KV_W = NSA_G * NSA_DH
CMP_LEN = 32
CMP_STRIDE = 16
SLC_LEN = 64
N_SEL = 16
WIN = 512
REL_BUCKETS = 32
REL_MAX_DIST = 1024

Q_TILE = 128
K_TILE = 128
GR_LANES = NSA_R * Q_TILE
QT_STEP = 4
STEP_Q = QT_STEP * Q_TILE
STEP_LANES = QT_STEP * GR_LANES
N_TOEPLITZ = REL_MAX_DIST // K_TILE + 2
N_WIN_TILES = WIN // K_TILE + 1
SLAB = 16
CMP_ROWS = 128
HG_CHUNK = 64
HG_GROUP = 2
HG_LEVELS = 6
SUBLANES = 8
HG_FINE_LEVELS = 3
VMEM_LIMIT_BYTES = 56 * 1024 * 1024


def _cparams(*sem):
    return pltpu.CompilerParams(dimension_semantics=sem, vmem_limit_bytes=VMEM_LIMIT_BYTES)


def _const_spec(shape):
    nd = len(shape)
    return pl.BlockSpec(shape, lambda *_: (0,) * nd, pipeline_mode=pl.Buffered(1))


def _rms(x, g_row):
    ms = jnp.mean(x * x, axis=-1, keepdims=True)
    return x * lax.rsqrt(ms + EPS) * g_row


def _silu(x):
    return x * jax.nn.sigmoid(x)


def _gelu_tanh(x):
    return 0.5 * x * (1.0 + jnp.tanh(math.sqrt(2.0 / math.pi) * (x + 0.044715 * (x * x * x))))


def _dot(a, b):
    return jnp.dot(a, b, preferred_element_type=F32)


def _dot_nt(a, b):
    return lax.dot_general(a, b, (((1,), (1,)), ((), ())), preferred_element_type=F32)


def _dot_tn(a, b):
    return lax.dot_general(a, b, (((0,), (0,)), ((), ())), preferred_element_type=F32)


def _dot_exact_lhs(c_bf, x):
    hi = x.astype(BF)
    r1 = x - hi.astype(F32)
    mid = r1.astype(BF)
    lo = (r1 - mid.astype(F32)).astype(BF)
    return _dot(c_bf, hi) + _dot(c_bf, mid) + _dot(c_bf, lo)


def _dot_exact_rhs(x, c_bf):
    hi = x.astype(BF)
    r1 = x - hi.astype(F32)
    mid = r1.astype(BF)
    lo = (r1 - mid.astype(F32)).astype(BF)
    return _dot(hi, c_bf) + _dot(mid, c_bf) + _dot(lo, c_bf)


def _ffn_kernel(*refs, n_chunks, tf, with_proj, with_final):
    it = iter(refs)
    x_ref = next(it)
    if with_proj:
        ys5_ref, yhg_ref, ynsa_ref, wo_ref = next(it), next(it), next(it), next(it)
    g_ref, wg_ref, wu_ref, wd_ref = next(it), next(it), next(it), next(it)
    if with_final:
        fg_ref = next(it)
    o_ref, h_scr, a_scr = next(it), next(it), next(it)

    x = x_ref[...]
    if with_proj:
        x = (x + _dot(ys5_ref[...].astype(BF), wo_ref[0:S5_W, :])
             + _dot(yhg_ref[...].astype(BF), wo_ref[S5_W:S5_W + HG_W, :])
             + _dot(ynsa_ref[...].astype(BF), wo_ref[S5_W + HG_W:, :]))
    h_scr[...] = _rms(x, g_ref[...]).astype(BF)
    for c in range(n_chunks):
        sl = slice(c * tf, (c + 1) * tf)
        h = h_scr[...]
        gate = _dot(h, wg_ref[:, sl])
        up = _dot(h, wu_ref[:, sl])
        a_scr[:, sl] = (_silu(gate) * up).astype(BF)
    x = x + 0.5 * _dot(a_scr[...], wd_ref[...])
    if with_final:
        x = _rms(x, fg_ref[...])
    o_ref[...] = x


def _ffn(x2, g, wg, wu, wd, proj=None, final_g=None, tm=512, tf=256):
    n, d = x2.shape
    dff = wg.shape[1]
    assert n % tm == 0 and dff % tf == 0
    row = lambda i: (i, 0)
    in_specs = [pl.BlockSpec((tm, d), row)]
    args = [x2]
    if proj is not None:
        ys5, yhg, ynsa, wo = proj
        in_specs += [pl.BlockSpec((tm, S5_W), row), pl.BlockSpec((tm, HG_W), row),
                     pl.BlockSpec((tm, NSA_W), row), _const_spec(wo.shape)]
        args += [ys5, yhg, ynsa, wo]
    in_specs += [_const_spec((1, d)), _const_spec(wg.shape), _const_spec(wu.shape), _const_spec(wd.shape)]
    args += [g.reshape(1, d), wg, wu, wd]
    if final_g is not None:
        in_specs.append(_const_spec((1, d)))
        args.append(final_g.reshape(1, d))
    kern = functools.partial(_ffn_kernel, n_chunks=dff // tf, tf=tf,
                             with_proj=proj is not None, with_final=final_g is not None)
    return pl.pallas_call(
        kern,
        grid=(n // tm,),
        in_specs=in_specs,
        out_specs=pl.BlockSpec((tm, d), row),
        out_shape=jax.ShapeDtypeStruct((n, d), F32),
        scratch_shapes=[pltpu.VMEM((tm, d), BF), pltpu.VMEM((tm, dff), BF)],
        compiler_params=_cparams("arbitrary"),
        name="ffn",
    )(*args)


N_TOK_A = S5_W + 4 * HG_W
N_T_ROWS = NSA_W + 2 * KV_W + 32


def _inproj_kernel(x_ref, g_ref, wtok_ref, wt_ref, za_ref, kcr_ref, vcr_ref, ksw_ref, qt_ref, vt_ref, gt_ref):
    h = _rms(x_ref[0], g_ref[...]).astype(BF)
    for c in range(N_TOK_A // 256):
        sl = slice(c * 256, (c + 1) * 256)
        za_ref[0, :, sl] = _dot(h, wtok_ref[:, sl])
    kcr_ref[0] = _dot(h, wtok_ref[:, N_TOK_A:N_TOK_A + KV_W])
    vcr_ref[0] = _dot(h, wtok_ref[:, N_TOK_A + KV_W:N_TOK_A + 2 * KV_W])
    ksw_ref[0] = _dot(h, wtok_ref[:, N_TOK_A + 2 * KV_W:N_TOK_A + 4 * KV_W]).astype(BF)
    qt_ref[0] = (_dot_nt(wt_ref[0:NSA_W, :], h) * (NSA_DH ** -0.5 * LOG2E)).astype(BF)
    vt_ref[0] = _dot_nt(wt_ref[NSA_W:NSA_W + 2 * KV_W, :], h).astype(BF)
    gt_ref[0] = jax.nn.sigmoid(_dot_nt(wt_ref[NSA_W + 2 * KV_W:, :], h))


def _inproj(x3, g, w_tok, w_t, tm=512):
    b, l, d = x3.shape
    assert l % tm == 0
    tok = lambda w: pl.BlockSpec((1, tm, w), lambda bi, i: (bi, i, 0))
    tr = lambda r: pl.BlockSpec((1, r, tm), lambda bi, i: (bi, 0, i))
    return pl.pallas_call(
        _inproj_kernel,
        grid=(b, l // tm),
        in_specs=[tok(d), _const_spec((1, d)), _const_spec(w_tok.shape), _const_spec(w_t.shape)],
        out_specs=[tok(N_TOK_A), tok(KV_W), tok(KV_W), tok(2 * KV_W), tr(NSA_W), tr(2 * KV_W), tr(32)],
        out_shape=[jax.ShapeDtypeStruct((b, l, N_TOK_A), F32),
                   jax.ShapeDtypeStruct((b, l, KV_W), F32),
                   jax.ShapeDtypeStruct((b, l, KV_W), F32),
                   jax.ShapeDtypeStruct((b, l, 2 * KV_W), BF),
                   jax.ShapeDtypeStruct((b, NSA_W, l), BF),
                   jax.ShapeDtypeStruct((b, 2 * KV_W, l), BF),
                   jax.ShapeDtypeStruct((b, 32, l), F32)],
        compiler_params=_cparams("arbitrary", "arbitrary"),
        name="inproj",
    )(x3, g.reshape(1, d), w_tok, w_t)


def _s5_kernel(u_ref, bblk_ref, cblk_ref, coef_ref, d_ref, wglu_ref, y_ref, xs_scr, carry_scr, *, tc):
    @pl.when(pl.program_id(1) == 0)
    def _():
        carry_scr[...] = jnp.zeros_like(carry_scr)

    u = u_ref[0]
    xs_scr[...] = _dot(u.astype(BF), bblk_ref[...])

    def body(r, carry):
        cre, cim = carry
        row = pl.multiple_of(r * SUBLANES, SUBLANES)
        xre = xs_scr[pl.ds(row, SUBLANES), 0:S5_N]
        xim = xs_scr[pl.ds(row, SUBLANES), S5_N:2 * S5_N]
        for idx, k in enumerate((1, 2, 4)):
            are, aim = coef_ref[idx, 0], coef_ref[idx, 1]
            sre, sim = pltpu.roll(xre, k, 0), pltpu.roll(xim, k, 0)
            xre, xim = xre + (are * sre - aim * sim), xim + (are * sim + aim * sre)
        pre, pim = coef_ref[3, 0], coef_ref[3, 1]
        xre, xim = xre + (pre * cre - pim * cim), xim + (pre * cim + pim * cre)
        xs_scr[pl.ds(row, SUBLANES), 0:S5_N] = xre
        xs_scr[pl.ds(row, SUBLANES), S5_N:2 * S5_N] = xim
        return xre[SUBLANES - 1:SUBLANES], xim[SUBLANES - 1:SUBLANES]

    cre, cim = lax.fori_loop(0, tc // SUBLANES, body, (carry_scr[0:1], carry_scr[1:2]))
    carry_scr[0:1] = cre
    carry_scr[1:2] = cim

    y = _dot(xs_scr[...].astype(BF), cblk_ref[...]) + d_ref[...] * u
    y = _gelu_tanh(y)
    y_ref[0] = y * jax.nn.sigmoid(_dot(y.astype(BF), wglu_ref[...]))


def _s5_params(lam_re, lam_im, log_dt, b_re, b_im, c_re, c_im):
    lr, li = lam_re.astype(F32), lam_im.astype(F32)
    dt = jnp.exp(log_dt.astype(F32))[:, None]
    mag = jnp.exp(lr * dt)
    ab_re, ab_im = mag * jnp.cos(li * dt), mag * jnp.sin(li * dt)
    den = lr * lr + li * li
    nr, ni = ab_re - 1.0, ab_im
    g_re = (nr * lr + ni * li) / den
    g_im = (ni * lr - nr * li) / den
    br, bi = b_re.astype(F32), b_im.astype(F32)
    bb_re = g_re[..., None] * br - g_im[..., None] * bi
    bb_im = g_re[..., None] * bi + g_im[..., None] * br
    eye = jnp.eye(S5_G, dtype=F32)
    blk = lambda w: jnp.einsum('gph,gk->ghkp', w, eye).reshape(S5_W, S5_N)
    bblk = jnp.concatenate([blk(bb_re), blk(bb_im)], axis=1).astype(BF)
    cblk_f = lambda w: jnp.einsum('ghp,gk->gpkh', w, eye).reshape(S5_N, S5_W)
    cblk = jnp.concatenate([cblk_f(c_re.astype(F32)), -cblk_f(c_im.astype(F32))], axis=0).astype(BF)
    are, aim = ab_re.reshape(1, S5_N), ab_im.reshape(1, S5_N)
    pw = [(are, aim)]
    for _ in range(SUBLANES - 1):
        pr, pi = pw[-1]
        pw.append((pr * are - pi * aim, pr * aim + pi * are))
    rows = np.arange(SUBLANES)[:, None]
    coef = []
    for k in (1, 2, 4):
        m = jnp.asarray((rows >= k).astype(np.float32))
        coef.append(jnp.stack([m * pw[k - 1][0], m * pw[k - 1][1]]))
    coef.append(jnp.stack([jnp.concatenate([p[0] for p in pw], axis=0), jnp.concatenate([p[1] for p in pw], axis=0)]))
    return bblk, cblk, jnp.stack(coef)


def _s5(za, bblk, cblk, coef, d, w_glu, tc=512):
    b, l, _ = za.shape
    assert l % tc == 0
    return pl.pallas_call(
        functools.partial(_s5_kernel, tc=tc),
        grid=(b, l // tc),
        in_specs=[pl.BlockSpec((1, tc, S5_W), lambda bi, i: (bi, i, 0)),
                  _const_spec(bblk.shape), _const_spec(cblk.shape), _const_spec(coef.shape),
                  _const_spec((1, S5_W)), _const_spec(w_glu.shape)],
        out_specs=pl.BlockSpec((1, tc, S5_W), lambda bi, i: (bi, i, 0)),
        out_shape=jax.ShapeDtypeStruct((b, l, S5_W), F32),
        scratch_shapes=[pltpu.VMEM((tc, 2 * S5_N), F32), pltpu.VMEM((2, S5_N), F32)],
        compiler_params=_cparams("arbitrary", "arbitrary"),
        name="s5",
    )(za, bblk, cblk, coef, d.reshape(1, S5_W).astype(F32), w_glu.astype(BF))


def _hgrn_constants():
    c = HG_CHUNK
    t = np.arange(c)[:, None]
    u = np.arange(c)[None, :]
    mats = [(u <= t)]
    masks = []
    for lv in range(HG_LEVELS):
        n = c >> lv
        half = n // 2
        ref = (t // n) * n + half - 1
        lower = (t % n) >= half
        if half < SUBLANES:
            mats.append(np.where(lower, (u > ref) & (u <= t), (u > t) & (u <= ref)))
        same = (t // n) == (u // n)
        masks.append(same & lower & ((u % n) < half))
    masks.append(t == u)
    gall = np.concatenate(mats, axis=0).astype(np.float32)
    mstk = np.stack([np.tile(m, (1, HG_HEADS)) for m in masks]).astype(np.float32)
    lane_head = np.arange(HG_W)[None, :] // HG_D
    hmask = (np.repeat(np.arange(HG_HEADS), c)[:, None] == lane_head).astype(np.float32)
    bd = (np.arange(HG_W)[:, None] // HG_D == lane_head).astype(np.float32)
    return gall, mstk, hmask, bd


def _hgrn_kernel(q_ref, f_ref, i_ref, g_ref, lb_ref, gain_ref, gall_ref, mstk_ref, hmask_ref, bd_ref, bdn_ref,
                 o_ref, st_scr, *, tt):
    c = HG_CHUNK

    @pl.when(pl.program_id(1) == 0)
    def _():
        st_scr[...] = jnp.zeros_like(st_scr)

    lb = lb_ref[...]
    hmask = hmask_ref[...]

    def level_decay(b, sums, lv):
        n = c >> lv
        half = n // 2
        if half < SUBLANES:
            fine = lv - (HG_LEVELS - HG_FINE_LEVELS)
            return jnp.exp(sums[(1 + fine) * c:(2 + fine) * c])
        pieces = []
        for blk in range(c // n):
            ref = blk * n + half - 1
            pieces.append(b[ref:ref + 1] - b[blk * n:blk * n + half])
            pieces.append(b[blk * n + half:(blk + 1) * n] - b[ref:ref + 1])
        return jnp.exp(jnp.concatenate(pieces, axis=0))

    def chunk(row, fl, sums, carry):
        qf = _silu(q_ref[0, pl.ds(row, c), :])
        kf = (1.0 - lb) * jax.nn.sigmoid(-fl)
        v = i_ref[0, pl.ds(row, c), :]
        v_bf = v.astype(BF)
        b = sums[0:c]
        e_b = jnp.exp(b)
        e_suf = jnp.exp(b[c - 1:c] - b)

        att = jnp.zeros((c, HG_HEADS * c), F32)
        for lv in range(HG_LEVELS + 1):
            if lv < HG_LEVELS:
                e = level_decay(b, sums, lv)
                z, w = qf * e, kf * e
            else:
                z, w = qf, kf
            ws = (jnp.concatenate([w] * HG_HEADS, axis=0) * hmask).astype(BF)
            att = att + mstk_ref[lv] * _dot_nt(z.astype(BF), ws)
        o = _dot(att.astype(BF), (jnp.concatenate([v] * HG_HEADS, axis=0) * hmask).astype(BF))

        st = st_scr[...]
        o = o + _dot_nt((qf * e_b).astype(BF), st.astype(BF))
        st_scr[...] = e_b[c - 1:c] * st + bd_ref[...] * _dot_tn(v_bf, (kf * e_suf).astype(BF))

        ms = _dot_exact_rhs(o * o, bdn_ref[...])
        o = o * lax.rsqrt(ms + EPS) * gain_ref[...]
        o_ref[0, pl.ds(row, c), :] = o * _silu(g_ref[0, pl.ds(row, c), :])
        return carry

    def chunk_group(cg, carry):
        rows = [pl.multiple_of((HG_GROUP * cg + j) * c, c) for j in range(HG_GROUP)]
        fls = [f_ref[0, pl.ds(r, c), :] for r in rows]
        lfs = [jnp.log(jnp.maximum(lb + (1.0 - lb) * jax.nn.sigmoid(fl), TINY)) for fl in fls]
        sums = _dot_exact_lhs(gall_ref[...], jnp.concatenate(lfs, axis=1))
        for j in range(HG_GROUP):
            carry = chunk(rows[j], fls[j], sums[:, j * HG_W:(j + 1) * HG_W], carry)
        return carry

    lax.fori_loop(0, tt // (HG_GROUP * c), chunk_group, 0)


def _hgrn(za, lb, gain, tt=512):
    b, l, _ = za.shape
    assert l % tt == 0
    gall, mstk, hmask, bd = _hgrn_constants()
    col = lambda j: pl.BlockSpec((1, tt, HG_W), lambda bi, i: (bi, i, j))
    consts = [jnp.asarray(gall, BF), jnp.asarray(mstk, F32), jnp.asarray(hmask, F32), jnp.asarray(bd, F32),
              jnp.asarray(bd / HG_D, BF)]
    return pl.pallas_call(
        functools.partial(_hgrn_kernel, tt=tt),
        grid=(b, l // tt),
        in_specs=[col(1), col(2), col(3), col(4), _const_spec((1, HG_W)), _const_spec((1, HG_W))]
                 + [_const_spec(x.shape) for x in consts],
        out_specs=pl.BlockSpec((1, tt, HG_W), lambda bi, i: (bi, i, 0)),
        out_shape=jax.ShapeDtypeStruct((b, l, HG_W), F32),
        scratch_shapes=[pltpu.VMEM((HG_W, HG_W), F32)],
        compiler_params=_cparams("arbitrary", "arbitrary"),
        name="hgrn2",
    )(za, za, za, za, lb.reshape(1, HG_W).astype(F32), jnp.tile(gain.astype(F32), HG_HEADS).reshape(1, HG_W),
      *consts)


def _compress_kernel(k16_ref, v16_ref, kpa_ref, kpb_ref, kw1a_ref, kw1b_ref, kw2_ref,
                     vpa_ref, vpb_ref, vw1a_ref, vw1b_ref, vw2_ref, kc_ref, vct_ref, *, nc):
    def hidden(x_ref, pa, pb, w1a, w1b):
        x16 = jnp.concatenate([x_ref[0, pl.ds(j, nc, stride=CMP_STRIDE), :] for j in range(CMP_STRIDE)], axis=1)
        first = _dot((x16 + pa).astype(BF), w1a)
        second = _dot((x16 + pb).astype(BF), w1b)
        pre = first + pltpu.roll(second, nc - 1, 0)
        rows = lax.broadcasted_iota(jnp.int32, pre.shape, 0)
        pre = jnp.where(rows < nc - 1, pre, 0.0)
        return _gelu_tanh(pre).astype(BF)

    hk = hidden(k16_ref, kpa_ref[...], kpb_ref[...], kw1a_ref[...], kw1b_ref[...])
    kc_ref[0] = _dot(hk, kw2_ref[...]).astype(BF)
    hv = hidden(v16_ref, vpa_ref[...], vpb_ref[...], vw1a_ref[...], vw1b_ref[...])
    vct_ref[0] = _dot_nt(vw2_ref[...], hv).astype(BF)


def _compress_params(pos, w1, w2, transpose_out=False):
    eye = jnp.eye(NSA_G, dtype=F32)
    w1r = w1.astype(F32).reshape(CMP_LEN, NSA_DH, NSA_DH)
    wexp = jnp.einsum('jde,gh->jgdhe', w1r, eye).reshape(CMP_LEN, KV_W, KV_W)
    half = CMP_LEN // 2
    w1a = wexp[:half].reshape(half * KV_W, KV_W).astype(BF)
    w1b = wexp[half:].reshape(half * KV_W, KV_W).astype(BF)
    pt = jnp.broadcast_to(pos.astype(F32)[:, None, :], (CMP_LEN, NSA_G, NSA_DH))
    pa = pt[:half].reshape(1, half * KV_W)
    pb = pt[half:].reshape(1, half * KV_W)
    w2bd = jnp.einsum('de,gh->gdhe', w2.astype(F32), eye).reshape(KV_W, KV_W).astype(BF)
    return pa, pb, w1a, w1b, (w2bd.T if transpose_out else w2bd)


def _compress(kcr, vcr, kparams, vparams):
    b, l, _ = kcr.shape
    nc = l // CMP_STRIDE
    per_b = lambda r, cc: pl.BlockSpec((1, r, cc), lambda bi: (bi, 0, 0))
    params = list(kparams) + list(vparams)
    return pl.pallas_call(
        functools.partial(_compress_kernel, nc=nc),
        grid=(b,),
        in_specs=[per_b(l, KV_W), per_b(l, KV_W)] + [_const_spec(p.shape) for p in params],
        out_specs=[per_b(nc, KV_W), per_b(KV_W, nc)],
        out_shape=[jax.ShapeDtypeStruct((b, nc, KV_W), BF), jax.ShapeDtypeStruct((b, KV_W, nc), BF)],
        compiler_params=_cparams("arbitrary"),
        name="compress",
    )(kcr, vcr, *params)


def _t5_bucket(dist):
    n = jnp.maximum(dist, 0)
    max_exact = REL_BUCKETS // 2
    nf = jnp.maximum(n, max_exact).astype(jnp.float32)
    large = max_exact + (jnp.log(nf / max_exact) / math.log(REL_MAX_DIST / max_exact)
                         * (REL_BUCKETS - max_exact)).astype(jnp.int32)
    large = jnp.minimum(large, REL_BUCKETS - 1)
    return jnp.where(n < max_exact, n, large)


def _bias_tables(rel_bias, l):
    tab = rel_bias.astype(F32) * LOG2E
    by_dist = tab[_t5_bucket(jnp.arange(l))].T
    nc = l // CMP_STRIDE
    w = 2 * K_TILE

    def shifted_rows(v, n_rows, step):
        flat = jnp.tile(v, (1,) * (v.ndim - 1) + (n_rows,))[..., :n_rows * (w - step)]
        return flat.reshape(v.shape[:-1] + (n_rows, w - step))[..., :Q_TILE]

    def group_layout(t, valid):
        _, n, r, q = t.shape
        t = jnp.where(jnp.asarray(valid)[None], t, NEG_INF)
        return t.reshape(NSA_G, NSA_R, n, r, q).transpose(0, 2, 3, 1, 4).reshape(NSA_G, n, r, NSA_R * q)

    tq = np.arange(Q_TILE)[None, None, :]
    key = np.arange(K_TILE)[None, :, None]
    blocks = jnp.pad(by_dist, ((0, 0), (K_TILE, w)), mode='edge').reshape(NSA_H, -1, K_TILE)
    vec = jnp.concatenate([blocks[:, 1:N_TOEPLITZ + 1], blocks[:, 0:N_TOEPLITZ]], axis=-1)
    toep = shifted_rows(vec, K_TILE, 1)
    masked_tile = jnp.full((NSA_G, 1, K_TILE, GR_LANES), NEG_INF, F32)
    d = np.arange(N_TOEPLITZ)[:, None, None] * K_TILE + tq - key
    tz = jnp.concatenate([masked_tile, group_layout(toep, d >= 0)], axis=1)
    d = (N_WIN_TILES - 1 - np.arange(N_WIN_TILES))[:, None, None] * K_TILE + tq - key
    wz = group_layout(toep[:, N_WIN_TILES - 1::-1], (d >= 0) & (d < WIN))
    wz = jnp.concatenate([masked_tile, wz, masked_tile], axis=1)
    na = (2 * nc - SUBLANES) // SUBLANES
    c0 = CMP_STRIDE * (nc - SUBLANES) - (CMP_LEN - 1)
    front = K_TILE * (na + 1)
    off = c0 % K_TILE
    padded = jnp.pad(by_dist, ((0, 0), (front, w)), mode='edge')
    nb = (padded.shape[1] - off) // K_TILE
    blocks = padded[:, off:off + nb * K_TILE].reshape(NSA_H, nb, K_TILE)
    k0 = (front + c0 - off) // K_TILE
    vec = jnp.concatenate([blocks[:, k0 - na + 1:k0 + 1][:, ::-1], blocks[:, k0 - na:k0][:, ::-1]], axis=-1)
    cmp_rows = shifted_rows(vec, SUBLANES, CMP_STRIDE).reshape(NSA_H, 1, na * SUBLANES, Q_TILE)
    u = np.arange(na * SUBLANES)[None, :, None]
    mc = group_layout(cmp_rows, tq - CMP_STRIDE * (u - (nc - SUBLANES)) - (CMP_LEN - 1) >= 0)[:, 0]
    return tz, wz, mc


def _nsa_kernel(qt_ref, gt_ref, kc_ref, vct_ref, ksl_ref, kwn_ref, vslt_ref, vwnt_ref, mc_ref, tz_ref, wz_ref,
                ind_ref, expand_ref, o_ref, imp_scr, y_scr, *group_scr, nc, nsb):
    per_group = len(group_scr) // NSA_G
    SLC, WIN_BRANCH = 0, 1

    def branch_scr(g, branch):
        m_scr, acc_scr, s_a, s_b, max_a, max_b = group_scr[g * per_group + 3 + 6 * branch:
                                                           g * per_group + 9 + 6 * branch]
        return m_scr, acc_scr, (s_a, s_b), (max_a, max_b)

    step = pl.program_id(1)
    tiles = [QT_STEP * step + a for a in range(QT_STEP)]
    t0 = step * STEP_Q
    n_sel = min(N_SEL, nsb)
    cmp_per_q = Q_TILE // CMP_STRIDE
    cmp_per_slc = SLC_LEN // CMP_STRIDE
    pair = 2 * K_TILE
    n_last = tiles[-1] // 2
    ones_rows = (lax.broadcasted_iota(jnp.int32, (SLAB, pair), 0) == 0).astype(BF)

    def lanes(parts):
        return jnp.concatenate(parts, axis=1)

    def per_head_lanes(x):
        return lanes([x[:, a * Q_TILE:(a + 1) * Q_TILE] for a in range(QT_STEP) for _ in range(NSA_R)])

    def group_rows(g):
        return slice(g * NSA_DH, (g + 1) * NSA_DH)

    def fill(branch, kp, buf):
        k0 = pl.multiple_of(jnp.maximum(kp, 0) * pair, pair)
        for g in range(NSA_G):
            qt_scr, slab_scr = group_scr[g * per_group:g * per_group + 2]
            _, _, s_bufs, max_bufs = branch_scr(g, branch)
            if branch == SLC:
                lhs = jnp.concatenate([ksl_ref[0, pl.ds(k0, pair), :], ind_ref[...]], axis=1)
                slab = per_head_lanes(slab_scr[pl.ds(pl.multiple_of(kp * SLAB, SLAB), SLAB), :])
                rhs = jnp.concatenate([qt_scr[...], slab, jnp.zeros((KV_W - SLAB, STEP_LANES), BF)], axis=0)
                bias = [lanes([tz_ref[g, jnp.clip(qi - (2 * kp + j), -1, N_TOEPLITZ - 1) + 1] for qi in tiles])
                        for j in range(2)]
            else:
                lhs, rhs = kwn_ref[0, pl.ds(k0, pair), :], qt_scr[...]
                bias = [lanes([wz_ref[g, jnp.where(kp < 0, 0, jnp.clip(2 * kp + j - (qi - (N_WIN_TILES - 1)) + 1,
                                                                       0, N_WIN_TILES + 1))]
                               for qi in tiles]) for j in range(2)]
            s = _dot(lhs, rhs) + jnp.concatenate(bias, axis=0)
            s_bufs[buf][...] = s
            max_bufs[buf][0:1, :] = jnp.max(s, axis=0, keepdims=True)

    def drain(branch, kp, buf):
        vt_ref = vslt_ref if branch == SLC else vwnt_ref
        k0 = pl.multiple_of(jnp.maximum(kp, 0) * pair, pair)
        for g in range(NSA_G):
            m_scr, acc_scr, s_bufs, max_bufs = branch_scr(g, branch)
            m_prev = m_scr[0:1, :]
            m_new = jnp.maximum(m_prev, max_bufs[buf][0:1, :])
            p = jnp.exp2(s_bufs[buf][...] - m_new).astype(BF)
            vt = jnp.concatenate([vt_ref[0, group_rows(g), pl.ds(k0, pair)], ones_rows], axis=0)
            acc_scr[...] = jnp.exp2(m_prev - m_new) * acc_scr[...] + _dot(vt, p)
            m_scr[0:1, :] = m_new

    def result(g, branch):
        _, acc_scr, _, _ = branch_scr(g, branch)
        return acc_scr[0:NSA_DH] * (1.0 / acc_scr[NSA_DH:NSA_DH + 1])

    for g in range(NSA_G):
        qt_scr = group_scr[g * per_group]
        zeros = jnp.zeros((NSA_DH, Q_TILE), BF)
        parts = []
        for a in range(QT_STEP):
            for r in range(NSA_R):
                h = g * NSA_R + r
                qh = qt_ref[0, h * NSA_DH:(h + 1) * NSA_DH, a * Q_TILE:(a + 1) * Q_TILE]
                parts.append(jnp.concatenate([qh, zeros] if g == 0 else [zeros, qh], axis=0))
        qt_scr[...] = lanes(parts)
        for branch in (SLC, WIN_BRANCH):
            m_scr, acc_scr, _, _ = branch_scr(g, branch)
            m_scr[...] = jnp.full_like(m_scr, NEG_INF)
            acc_scr[...] = jnp.zeros_like(acc_scr)

    lane = lax.broadcasted_iota(jnp.int32, (1, STEP_LANES), 1)
    t_lane = t0 + (lane // GR_LANES) * Q_TILE + lane % Q_TILE

    def compressed(rows):
        for g in range(NSA_G):
            qt = group_scr[g * per_group][...]
            bias_c = lanes([mc_ref[g, pl.ds(pl.multiple_of((nc - SUBLANES) - cmp_per_q * qi, SUBLANES), rows), :]
                            for qi in tiles])
            s = _dot(kc_ref[0, 0:rows, :], qt) + bias_c
            m = jnp.max(s, axis=0, keepdims=True)
            e = jnp.exp2(s - m)
            lsum = jnp.sum(e, axis=0, keepdims=True)
            inv = jnp.where(t_lane >= CMP_LEN - 1, 1.0 / jnp.maximum(lsum, TINY), 0.0)
            p = e * inv
            group_scr[g * per_group + 2][...] = _dot(vct_ref[0, group_rows(g), 0:rows], p.astype(BF))
            for a in range(QT_STEP):
                first = a * GR_LANES
                imp = p[:, first:first + Q_TILE]
                for r in range(1, NSA_R):
                    imp = imp + p[:, first + r * Q_TILE:first + (r + 1) * Q_TILE]
                imp_scr[g, a, 0:SUBLANES, :] = jnp.zeros((SUBLANES, Q_TILE), F32)
                imp_scr[g, a, SUBLANES:SUBLANES + rows, :] = imp
                if rows < nc:
                    imp_scr[g, a, SUBLANES + rows:SUBLANES + nc, :] = jnp.zeros((nc - rows, Q_TILE), F32)

    n_variants = max(nc // CMP_ROWS, 1)
    visible = cmp_per_q * QT_STEP * (step + 1)
    variant = jnp.minimum((visible + CMP_ROWS - 1) // CMP_ROWS, n_variants)
    for v in range(1, n_variants + 1):
        pl.when(variant == v)(functools.partial(compressed, min(v * CMP_ROWS, nc)))

    block_scores = []
    for g in range(NSA_G):
        p_slc = []
        for a in range(QT_STEP):
            a_k = [imp_scr[g, a, pl.ds(SUBLANES - 1 + k, nsb, stride=cmp_per_slc), :]
                   for k in range(cmp_per_slc + 1)]
            acc = a_k[0] + a_k[1]
            for k in range(1, cmp_per_slc):
                acc = acc + a_k[k] + a_k[k + 1]
            p_slc.append(acc)
        block_scores.append(lanes(p_slc))

    jidx = lax.broadcasted_iota(jnp.int32, (nsb, STEP_Q), 0)
    tq = t0 + lax.broadcasted_iota(jnp.int32, (nsb, STEP_Q), 1)
    valid = jidx * SLC_LEN <= tq

    def store_masks(masks):
        for g in range(NSA_G):
            group_scr[g * per_group + 1][...] = _dot(expand_ref[...], masks[g].astype(BF)).astype(BF)

    n_forced = 3
    few_blocks_steps = (n_sel * SLC_LEN) // STEP_Q

    @pl.when(step < few_blocks_steps)
    def _():
        store_masks([jnp.where(valid, 0.0, NEG_INF)] * NSA_G)

    @pl.when(step >= few_blocks_steps)
    def _():
        cur = tq // SLC_LEN
        forced = (jidx == 0) | (jidx == cur) | (jidx == cur - 1)
        masks = []
        for g in range(NSA_G):
            score = jnp.where(forced, -jnp.inf, jnp.where(valid, block_scores[g], NEG_INF))
            for _ in range(n_sel - n_forced):
                mx = jnp.max(score, axis=0, keepdims=True)
                first = jnp.min(jnp.where(score == mx, jidx, nsb), axis=0, keepdims=True)
                score = jnp.where(jidx == first, -jnp.inf, score)
            masks.append(jnp.where(score == -jnp.inf, 0.0, NEG_INF))
        store_masks(masks)

    n_win_pairs = (QT_STEP + N_WIN_TILES - 1 + 1) // 2
    win_first = n_last - (n_win_pairs - 1)
    fill(SLC, jnp.int32(0), 0)

    def slc_body(j, carry):
        kp = 2 * j
        fill(SLC, kp + 1, 1)
        drain(SLC, kp, 0)
        fill(SLC, kp + 2, 0)
        drain(SLC, kp + 1, 1)
        return carry

    lax.fori_loop(0, n_last // 2, slc_body, 0)

    @pl.when(n_last % 2 == 1)
    def _():
        fill(SLC, n_last, 1)
        fill(WIN_BRANCH, win_first, 0)
        drain(SLC, n_last - 1, 0)
        drain(SLC, n_last, 1)

    @pl.when(n_last % 2 == 0)
    def _():
        fill(WIN_BRANCH, win_first, 0)
        drain(SLC, n_last, 0)

    for i in range(1, n_win_pairs):
        fill(WIN_BRANCH, win_first + i, i % 2)
        drain(WIN_BRANCH, win_first + i - 1, (i - 1) % 2)
    drain(WIN_BRANCH, win_first + n_win_pairs - 1, (n_win_pairs - 1) % 2)

    for g in range(NSA_G):
        def gate(branch):
            return lanes([gt_ref[0, pl.ds((g * NSA_R + r) * 3 + branch, 1), a * Q_TILE:(a + 1) * Q_TILE]
                          for a in range(QT_STEP) for r in range(NSA_R)])

        yt = (gate(0) * group_scr[g * per_group + 2][...] + gate(1) * result(g, SLC)
              + gate(2) * result(g, WIN_BRANCH))
        for a in range(QT_STEP):
            for r in range(NSA_R):
                h = g * NSA_R + r
                first = (a * NSA_R + r) * Q_TILE
                y_scr[h * NSA_DH:(h + 1) * NSA_DH, a * Q_TILE:(a + 1) * Q_TILE] = yt[:, first:first + Q_TILE]

    o_ref[0] = y_scr[...].T


def _nsa(qt, gt, kc, vct, ksw, vt, tz, wz, mc):
    b, _, l = qt.shape
    nc = l // CMP_STRIDE
    nsb = l // SLC_LEN
    per_b = lambda shape, idx: pl.BlockSpec((1,) + shape, lambda bi, i: (bi,) + idx, pipeline_mode=pl.Buffered(1))
    assert l % STEP_Q == 0
    pair = 2 * K_TILE
    blocks_per_pair = pair // SLC_LEN
    n_pairs = l // pair
    ind = (np.arange(pair)[:, None] // SLC_LEN == np.arange(KV_W)[None, :]).astype(np.float32)
    expand = np.zeros((n_pairs * SLAB, nsb), np.float32)
    for k in range(n_pairs):
        for j in range(blocks_per_pair):
            expand[k * SLAB + j, k * blocks_per_pair + j] = 1.0
    ind, expand = jnp.asarray(ind, BF), jnp.asarray(expand, BF)
    return pl.pallas_call(
        functools.partial(_nsa_kernel, nc=nc, nsb=nsb),
        grid=(b, l // STEP_Q),
        in_specs=[pl.BlockSpec((1, NSA_W, STEP_Q), lambda bi, i: (bi, 0, i)),
                  pl.BlockSpec((1, 32, STEP_Q), lambda bi, i: (bi, 0, i)),
                  per_b((nc, KV_W), (0, 0)), per_b((KV_W, nc), (0, 0)),
                  per_b((l, KV_W), (0, 0)), per_b((l, KV_W), (0, 1)),
                  per_b((KV_W, l), (0, 0)), per_b((KV_W, l), (1, 0)),
                  _const_spec(mc.shape), _const_spec(tz.shape), _const_spec(wz.shape),
                  _const_spec(ind.shape), _const_spec(expand.shape)],
        out_specs=pl.BlockSpec((1, STEP_Q, NSA_W), lambda bi, i: (bi, i, 0)),
        out_shape=jax.ShapeDtypeStruct((b, l, NSA_W), F32),
        scratch_shapes=[pltpu.VMEM((NSA_G, QT_STEP, nc + 4 * SUBLANES, Q_TILE), F32),
                        pltpu.VMEM((NSA_W, STEP_Q), F32)]
                       + NSA_G * ([pltpu.VMEM((KV_W, STEP_LANES), BF), pltpu.VMEM((n_pairs * SLAB, STEP_Q), BF),
                                   pltpu.VMEM((NSA_DH, STEP_LANES), F32)]
                                  + 2 * [pltpu.VMEM((SUBLANES, STEP_LANES), F32),
                                         pltpu.VMEM((NSA_DH + SLAB, STEP_LANES), F32),
                                         pltpu.VMEM((pair, STEP_LANES), F32), pltpu.VMEM((pair, STEP_LANES), F32),
                                         pltpu.VMEM((SUBLANES, STEP_LANES), F32),
                                         pltpu.VMEM((SUBLANES, STEP_LANES), F32)]),
        compiler_params=_cparams("arbitrary", "arbitrary"),
        name="nsa",
    )(qt, gt, kc, vct, ksw, ksw, vt, vt, mc, tz, wz, ind, expand)


def _split_w_in(w_in):
    w_in = w_in.astype(BF)
    o_q = N_TOK_A
    o_kv = o_q + NSA_W
    o_gate = o_kv + 6 * KV_W
    kv = lambda j: w_in[:, o_kv + j * KV_W:o_kv + (j + 1) * KV_W]
    w_tok = jnp.concatenate([w_in[:, :N_TOK_A], kv(0), kv(1), kv(2), kv(4)], axis=1).astype(BF)
    gates = w_in[:, o_gate:]
    pad = jnp.zeros((w_in.shape[0], 32 - gates.shape[1]), w_in.dtype)
    w_t = jnp.concatenate([w_in[:, o_q:o_kv], kv(3), kv(5), gates, pad], axis=1).T.astype(BF)
    return w_tok, w_t


def kernel(x, ffn1_norm, ffn1_w_gate, ffn1_w_up, ffn1_w_down, mix_norm, w_in, w_out, s5_lambda_re, s5_lambda_im, s5_log_dt, s5_b_re, s5_b_im, s5_c_re, s5_c_im, s5_d, s5_w_glu, hgrn_lb_logits, hgrn_norm, nsa_cmp_pos_k, nsa_cmp_w1_k, nsa_cmp_w2_k, nsa_cmp_pos_v, nsa_cmp_w1_v, nsa_cmp_w2_v, rel_bias, ffn2_norm, ffn2_w_gate, ffn2_w_up, ffn2_w_down, final_norm):
    b, l, d = x.shape
    depth = w_in.shape[0]
    gam = jax.nn.softmax(hgrn_lb_logits.astype(F32), axis=0)
    lower_bounds = jnp.cumsum(gam, axis=0) - gam[0:1]
    tz, wz, mc = _bias_tables(rel_bias, l)
    bf = lambda w: w.astype(BF)

    x2 = x.reshape(b * l, d)
    for i in range(depth):
        x2 = _ffn(x2, ffn1_norm[i], bf(ffn1_w_gate[i]), bf(ffn1_w_up[i]), bf(ffn1_w_down[i]))
        w_tok, w_t = _split_w_in(w_in[i])
        za, kcr, vcr, ksw, qt, vt, gt = _inproj(x2.reshape(b, l, d), mix_norm[i], w_tok, w_t)
        bblk, cblk, coef = _s5_params(s5_lambda_re[i], s5_lambda_im[i], s5_log_dt[i], s5_b_re[i], s5_b_im[i],
                                      s5_c_re[i], s5_c_im[i])
        y_s5 = _s5(za, bblk, cblk, coef, s5_d[i], s5_w_glu[i])
        y_hg = _hgrn(za, lower_bounds[i], hgrn_norm[i])
        kc, vct = _compress(kcr, vcr,
                            _compress_params(nsa_cmp_pos_k[i], nsa_cmp_w1_k[i], nsa_cmp_w2_k[i]),
                            _compress_params(nsa_cmp_pos_v[i], nsa_cmp_w1_v[i], nsa_cmp_w2_v[i], transpose_out=True))
        y_nsa = _nsa(qt, gt, kc, vct, ksw, vt, tz, wz, mc)
        proj = (y_s5.reshape(b * l, S5_W), y_hg.reshape(b * l, HG_W), y_nsa.reshape(b * l, NSA_W), bf(w_out[i]))
        x2 = _ffn(x2, ffn2_norm[i], bf(ffn2_w_gate[i]), bf(ffn2_w_up[i]), bf(ffn2_w_down[i]), proj=proj,
                  final_g=final_norm if i + 1 == depth else None)
    return x2.reshape(b, l, d)
```

```python
import functools
import math

import numpy as np
import jax
import jax.numpy as jnp
from jax import lax
from jax.experimental import pallas as pl
from jax.experimental.pallas import tpu as pltpu

BF = jnp.bfloat16
F32 = jnp.float32

EPS = 1e-6
NEG_INF = -1e30
TINY = 1e-30
LOG2E = math.log2(math.e)

S5_W = 256
S5_G = 16
S5_GROUP = 16
S5_P = 64
S5_N = S5_G * S5_P
HG_W = 256
HG_HEADS = 4
HG_D = 64
NSA_W = 512
NSA_DH = 64
NSA_H = 8
NSA_G = 2
NSA_R = 4
KV_W = NSA_G * NSA_DH
CMP_LEN = 32
CMP_STRIDE = 16
SLC_LEN = 64
N_SEL = 16
WIN = 512
REL_BUCKETS = 32
REL_MAX_DIST = 1024

Q_TILE = 128
K_TILE = 128
GR_LANES = NSA_R * Q_TILE
QT_STEP = 4
STEP_Q = QT_STEP * Q_TILE
STEP_LANES = QT_STEP * GR_LANES
N_TOEPLITZ = REL_MAX_DIST // K_TILE + 2
N_WIN_TILES = WIN // K_TILE + 1
SLAB = 16
CMP_ROWS = 128
PICK_ROWS = 32
HG_CHUNK = 64
HG_GROUP = 2
HG_LEVELS = 6
SUBLANES = 8
HG_FINE_LEVELS = 3
VMEM_LIMIT_BYTES = 56 * 1024 * 1024


def _cparams(*sem):
    return pltpu.CompilerParams(dimension_semantics=sem, vmem_limit_bytes=VMEM_LIMIT_BYTES)


def _const_spec(shape):
    nd = len(shape)
    return pl.BlockSpec(shape, lambda *_: (0,) * nd, pipeline_mode=pl.Buffered(1))


def _rms(x, g_row):
    ms = jnp.mean(x * x, axis=-1, keepdims=True)
    return x * lax.rsqrt(ms + EPS) * g_row


def _silu(x):
    return x * jax.nn.sigmoid(x)


def _gelu_tanh(x):
    return 0.5 * x * (1.0 + jnp.tanh(math.sqrt(2.0 / math.pi) * (x + 0.044715 * (x * x * x))))


def _dot(a, b):
    return jnp.dot(a, b, preferred_element_type=F32)


def _dot_nt(a, b):
    return lax.dot_general(a, b, (((1,), (1,)), ((), ())), preferred_element_type=F32)


def _dot_tn(a, b):
    return lax.dot_general(a, b, (((0,), (0,)), ((), ())), preferred_element_type=F32)


def _dot_exact_lhs(c_bf, x):
    hi = x.astype(BF)
    r1 = x - hi.astype(F32)
    mid = r1.astype(BF)
    lo = (r1 - mid.astype(F32)).astype(BF)
    return _dot(c_bf, hi) + _dot(c_bf, mid) + _dot(c_bf, lo)


def _dot_exact_rhs(x, c_bf):
    hi = x.astype(BF)
    r1 = x - hi.astype(F32)
    mid = r1.astype(BF)
    lo = (r1 - mid.astype(F32)).astype(BF)
    return _dot(hi, c_bf) + _dot(mid, c_bf) + _dot(lo, c_bf)


def _ffn_kernel(*refs, n_chunks, tf, with_proj, with_final):
    it = iter(refs)
    x_ref = next(it)
    if with_proj:
        ys5_ref, yhg_ref, ynsa_ref, wo_ref = next(it), next(it), next(it), next(it)
    g_ref, wg_ref, wu_ref, wd_ref = next(it), next(it), next(it), next(it)
    if with_final:
        fg_ref = next(it)
    o_ref, h_scr, a_scr = next(it), next(it), next(it)

    x = x_ref[...]
    if with_proj:
        x = (x + _dot(ys5_ref[...].astype(BF), wo_ref[0:S5_W, :])
             + _dot(yhg_ref[...].astype(BF), wo_ref[S5_W:S5_W + HG_W, :])
             + _dot(ynsa_ref[...].astype(BF), wo_ref[S5_W + HG_W:, :]))
    h_scr[...] = _rms(x, g_ref[...]).astype(BF)
    for c in range(n_chunks):
        sl = slice(c * tf, (c + 1) * tf)
        h = h_scr[...]
        gate = _dot(h, wg_ref[:, sl])
        up = _dot(h, wu_ref[:, sl])
        a_scr[:, sl] = (_silu(gate) * up).astype(BF)
    x = x + 0.5 * _dot(a_scr[...], wd_ref[...])
    if with_final:
        x = _rms(x, fg_ref[...])
    o_ref[...] = x


def _ffn(x2, g, wg, wu, wd, proj=None, final_g=None, tm=512, tf=256):
    n, d = x2.shape
    dff = wg.shape[1]
    assert n % tm == 0 and dff % tf == 0
    row = lambda i: (i, 0)
    in_specs = [pl.BlockSpec((tm, d), row)]
    args = [x2]
    if proj is not None:
        ys5, yhg, ynsa, wo = proj
        in_specs += [pl.BlockSpec((tm, S5_W), row), pl.BlockSpec((tm, HG_W), row),
                     pl.BlockSpec((tm, NSA_W), row), _const_spec(wo.shape)]
        args += [ys5, yhg, ynsa, wo]
    in_specs += [_const_spec((1, d)), _const_spec(wg.shape), _const_spec(wu.shape), _const_spec(wd.shape)]
    args += [g.reshape(1, d), wg, wu, wd]
    if final_g is not None:
        in_specs.append(_const_spec((1, d)))
        args.append(final_g.reshape(1, d))
    kern = functools.partial(_ffn_kernel, n_chunks=dff // tf, tf=tf,
                             with_proj=proj is not None, with_final=final_g is not None)
    return pl.pallas_call(
        kern,
        grid=(n // tm,),
        in_specs=in_specs,
        out_specs=pl.BlockSpec((tm, d), row),
        out_shape=jax.ShapeDtypeStruct((n, d), F32),
        scratch_shapes=[pltpu.VMEM((tm, d), BF), pltpu.VMEM((tm, dff), BF)],
        compiler_params=_cparams("arbitrary"),
        name="ffn",
    )(*args)


N_TOK_A = S5_W + 4 * HG_W
N_T_ROWS = NSA_W + 2 * KV_W + 32


def _inproj_kernel(x_ref, g_ref, wtok_ref, wt_ref, za_ref, kcr_ref, vcr_ref, ksw_ref, qt_ref, vt_ref, gt_ref):
    h = _rms(x_ref[0], g_ref[...]).astype(BF)
    for c in range(N_TOK_A // 256):
        sl = slice(c * 256, (c + 1) * 256)
        za_ref[0, :, sl] = _dot(h, wtok_ref[:, sl])
    kcr_ref[0] = _dot(h, wtok_ref[:, N_TOK_A:N_TOK_A + KV_W])
    vcr_ref[0] = _dot(h, wtok_ref[:, N_TOK_A + KV_W:N_TOK_A + 2 * KV_W])
    ksw_ref[0] = _dot(h, wtok_ref[:, N_TOK_A + 2 * KV_W:N_TOK_A + 4 * KV_W]).astype(BF)
    qt_ref[0] = (_dot_nt(wt_ref[0:NSA_W, :], h) * (NSA_DH ** -0.5 * LOG2E)).astype(BF)
    vt_ref[0] = _dot_nt(wt_ref[NSA_W:NSA_W + 2 * KV_W, :], h).astype(BF)
    gt_ref[0] = jax.nn.sigmoid(_dot_nt(wt_ref[NSA_W + 2 * KV_W:, :], h))


def _inproj(x3, g, w_tok, w_t, tm=512):
    b, l, d = x3.shape
    assert l % tm == 0
    tok = lambda w: pl.BlockSpec((1, tm, w), lambda bi, i: (bi, i, 0))
    tr = lambda r: pl.BlockSpec((1, r, tm), lambda bi, i: (bi, 0, i))
    return pl.pallas_call(
        _inproj_kernel,
        grid=(b, l // tm),
        in_specs=[tok(d), _const_spec((1, d)), _const_spec(w_tok.shape), _const_spec(w_t.shape)],
        out_specs=[tok(N_TOK_A), tok(KV_W), tok(KV_W), tok(2 * KV_W), tr(NSA_W), tr(2 * KV_W), tr(32)],
        out_shape=[jax.ShapeDtypeStruct((b, l, N_TOK_A), F32),
                   jax.ShapeDtypeStruct((b, l, KV_W), F32),
                   jax.ShapeDtypeStruct((b, l, KV_W), F32),
                   jax.ShapeDtypeStruct((b, l, 2 * KV_W), BF),
                   jax.ShapeDtypeStruct((b, NSA_W, l), BF),
                   jax.ShapeDtypeStruct((b, 2 * KV_W, l), BF),
                   jax.ShapeDtypeStruct((b, 32, l), F32)],
        compiler_params=_cparams("arbitrary", "arbitrary"),
        name="inproj",
    )(x3, g.reshape(1, d), w_tok, w_t)


def _s5_kernel(u_ref, bblk_ref, cblk_ref, coef_ref, d_ref, wglu_ref, y_ref, xs_scr, carry_scr, *, tc):
    @pl.when(pl.program_id(1) == 0)
    def _():
        carry_scr[...] = jnp.zeros_like(carry_scr)

    u = u_ref[0]
    xs_scr[...] = _dot(u.astype(BF), bblk_ref[...])

    def body(r, carry):
        cre, cim = carry
        row = pl.multiple_of(r * SUBLANES, SUBLANES)
        xre = xs_scr[pl.ds(row, SUBLANES), 0:S5_N]
        xim = xs_scr[pl.ds(row, SUBLANES), S5_N:2 * S5_N]
        for idx, k in enumerate((1, 2, 4)):
            are, aim = coef_ref[idx, 0], coef_ref[idx, 1]
            sre, sim = pltpu.roll(xre, k, 0), pltpu.roll(xim, k, 0)
            xre, xim = xre + (are * sre - aim * sim), xim + (are * sim + aim * sre)
        pre, pim = coef_ref[3, 0], coef_ref[3, 1]
        xre, xim = xre + (pre * cre - pim * cim), xim + (pre * cim + pim * cre)
        xs_scr[pl.ds(row, SUBLANES), 0:S5_N] = xre
        xs_scr[pl.ds(row, SUBLANES), S5_N:2 * S5_N] = xim
        return xre[SUBLANES - 1:SUBLANES], xim[SUBLANES - 1:SUBLANES]

    cre, cim = lax.fori_loop(0, tc // SUBLANES, body, (carry_scr[0:1], carry_scr[1:2]))
    carry_scr[0:1] = cre
    carry_scr[1:2] = cim

    y = _dot(xs_scr[...].astype(BF), cblk_ref[...]) + d_ref[...] * u
    y = _gelu_tanh(y)
    y_ref[0] = y * jax.nn.sigmoid(_dot(y.astype(BF), wglu_ref[...]))


def _s5_params(lam_re, lam_im, log_dt, b_re, b_im, c_re, c_im):
    lr, li = lam_re.astype(F32), lam_im.astype(F32)
    dt = jnp.exp(log_dt.astype(F32))[:, None]
    mag = jnp.exp(lr * dt)
    ab_re, ab_im = mag * jnp.cos(li * dt), mag * jnp.sin(li * dt)
    den = lr * lr + li * li
    nr, ni = ab_re - 1.0, ab_im
    g_re = (nr * lr + ni * li) / den
    g_im = (ni * lr - nr * li) / den
    br, bi = b_re.astype(F32), b_im.astype(F32)
    bb_re = g_re[..., None] * br - g_im[..., None] * bi
    bb_im = g_re[..., None] * bi + g_im[..., None] * br
    eye = jnp.eye(S5_G, dtype=F32)
    blk = lambda w: jnp.einsum('gph,gk->ghkp', w, eye).reshape(S5_W, S5_N)
    bblk = jnp.concatenate([blk(bb_re), blk(bb_im)], axis=1).astype(BF)
    cblk_f = lambda w: jnp.einsum('ghp,gk->gpkh', w, eye).reshape(S5_N, S5_W)
    cblk = jnp.concatenate([cblk_f(c_re.astype(F32)), -cblk_f(c_im.astype(F32))], axis=0).astype(BF)
    are, aim = ab_re.reshape(1, S5_N), ab_im.reshape(1, S5_N)
    pw = [(are, aim)]
    for _ in range(SUBLANES - 1):
        pr, pi = pw[-1]
        pw.append((pr * are - pi * aim, pr * aim + pi * are))
    rows = np.arange(SUBLANES)[:, None]
    coef = []
    for k in (1, 2, 4):
        m = jnp.asarray((rows >= k).astype(np.float32))
        coef.append(jnp.stack([m * pw[k - 1][0], m * pw[k - 1][1]]))
    coef.append(jnp.stack([jnp.concatenate([p[0] for p in pw], axis=0), jnp.concatenate([p[1] for p in pw], axis=0)]))
    return bblk, cblk, jnp.stack(coef)


def _s5(za, bblk, cblk, coef, d, w_glu, tc=512):
    b, l, _ = za.shape
    assert l % tc == 0
    return pl.pallas_call(
        functools.partial(_s5_kernel, tc=tc),
        grid=(b, l // tc),
        in_specs=[pl.BlockSpec((1, tc, S5_W), lambda bi, i: (bi, i, 0)),
                  _const_spec(bblk.shape), _const_spec(cblk.shape), _const_spec(coef.shape),
                  _const_spec((1, S5_W)), _const_spec(w_glu.shape)],
        out_specs=pl.BlockSpec((1, tc, S5_W), lambda bi, i: (bi, i, 0)),
        out_shape=jax.ShapeDtypeStruct((b, l, S5_W), F32),
        scratch_shapes=[pltpu.VMEM((tc, 2 * S5_N), F32), pltpu.VMEM((2, S5_N), F32)],
        compiler_params=_cparams("arbitrary", "arbitrary"),
        name="s5",
    )(za, bblk, cblk, coef, d.reshape(1, S5_W).astype(F32), w_glu.astype(BF))


def _hgrn_constants():
    c = HG_CHUNK
    t = np.arange(c)[:, None]
    u = np.arange(c)[None, :]
    mats = [(u <= t)]
    masks = []
    for lv in range(HG_LEVELS):
        n = c >> lv
        half = n // 2
        ref = (t // n) * n + half - 1
        lower = (t % n) >= half
        if half < SUBLANES:
            mats.append(np.where(lower, (u > ref) & (u <= t), (u > t) & (u <= ref)))
        same = (t // n) == (u // n)
        masks.append(same & lower & ((u % n) < half))
    masks.append(t == u)
    gall = np.concatenate(mats, axis=0).astype(np.float32)
    mstk = np.stack([np.tile(m, (1, HG_HEADS)) for m in masks]).astype(np.float32)
    lane_head = np.arange(HG_W)[None, :] // HG_D
    hmask = (np.repeat(np.arange(HG_HEADS), c)[:, None] == lane_head).astype(np.float32)
    bd = (np.arange(HG_W)[:, None] // HG_D == lane_head).astype(np.float32)
    return gall, mstk, hmask, bd


def _hgrn_kernel(q_ref, f_ref, i_ref, g_ref, lb_ref, gain_ref, gall_ref, mstk_ref, hmask_ref, bd_ref, bdn_ref,
                 o_ref, st_scr, *, tt):
    c = HG_CHUNK

    @pl.when(pl.program_id(1) == 0)
    def _():
        st_scr[...] = jnp.zeros_like(st_scr)

    lb = lb_ref[...]
    hmask = hmask_ref[...]

    def level_decay(b, sums, lv):
        n = c >> lv
        half = n // 2
        if half < SUBLANES:
            fine = lv - (HG_LEVELS - HG_FINE_LEVELS)
            return jnp.exp(sums[(1 + fine) * c:(2 + fine) * c])
        pieces = []
        for blk in range(c // n):
            ref = blk * n + half - 1
            pieces.append(b[ref:ref + 1] - b[blk * n:blk * n + half])
            pieces.append(b[blk * n + half:(blk + 1) * n] - b[ref:ref + 1])
        return jnp.exp(jnp.concatenate(pieces, axis=0))

    def chunk(row, fl, sums, carry):
        qf = _silu(q_ref[0, pl.ds(row, c), :])
        kf = (1.0 - lb) * jax.nn.sigmoid(-fl)
        v = i_ref[0, pl.ds(row, c), :]
        v_bf = v.astype(BF)
        b = sums[0:c]
        e_b = jnp.exp(b)
        e_suf = jnp.exp(b[c - 1:c] - b)

        att = jnp.zeros((c, HG_HEADS * c), F32)
        for lv in range(HG_LEVELS + 1):
            if lv < HG_LEVELS:
                e = level_decay(b, sums, lv)
                z, w = qf * e, kf * e
            else:
                z, w = qf, kf
            ws = (jnp.concatenate([w] * HG_HEADS, axis=0) * hmask).astype(BF)
            att = att + mstk_ref[lv] * _dot_nt(z.astype(BF), ws)
        o = _dot(att.astype(BF), (jnp.concatenate([v] * HG_HEADS, axis=0) * hmask).astype(BF))

        st = st_scr[...]
        o = o + _dot_nt((qf * e_b).astype(BF), st.astype(BF))
        st_scr[...] = e_b[c - 1:c] * st + bd_ref[...] * _dot_tn(v_bf, (kf * e_suf).astype(BF))

        ms = _dot_exact_rhs(o * o, bdn_ref[...])
        o = o * lax.rsqrt(ms + EPS) * gain_ref[...]
        o_ref[0, pl.ds(row, c), :] = o * _silu(g_ref[0, pl.ds(row, c), :])
        return carry

    def chunk_group(cg, carry):
        rows = [pl.multiple_of((HG_GROUP * cg + j) * c, c) for j in range(HG_GROUP)]
        fls = [f_ref[0, pl.ds(r, c), :] for r in rows]
        lfs = [jnp.log(jnp.maximum(lb + (1.0 - lb) * jax.nn.sigmoid(fl), TINY)) for fl in fls]
        sums = _dot_exact_lhs(gall_ref[...], jnp.concatenate(lfs, axis=1))
        for j in range(HG_GROUP):
            carry = chunk(rows[j], fls[j], sums[:, j * HG_W:(j + 1) * HG_W], carry)
        return carry

    lax.fori_loop(0, tt // (HG_GROUP * c), chunk_group, 0)


def _hgrn(za, lb, gain, tt=512):
    b, l, _ = za.shape
    assert l % tt == 0
    gall, mstk, hmask, bd = _hgrn_constants()
    col = lambda j: pl.BlockSpec((1, tt, HG_W), lambda bi, i: (bi, i, j))
    consts = [jnp.asarray(gall, BF), jnp.asarray(mstk, F32), jnp.asarray(hmask, F32), jnp.asarray(bd, F32),
              jnp.asarray(bd / HG_D, BF)]
    return pl.pallas_call(
        functools.partial(_hgrn_kernel, tt=tt),
        grid=(b, l // tt),
        in_specs=[col(1), col(2), col(3), col(4), _const_spec((1, HG_W)), _const_spec((1, HG_W))]
                 + [_const_spec(x.shape) for x in consts],
        out_specs=pl.BlockSpec((1, tt, HG_W), lambda bi, i: (bi, i, 0)),
        out_shape=jax.ShapeDtypeStruct((b, l, HG_W), F32),
        scratch_shapes=[pltpu.VMEM((HG_W, HG_W), F32)],
        compiler_params=_cparams("arbitrary", "arbitrary"),
        name="hgrn2",
    )(za, za, za, za, lb.reshape(1, HG_W).astype(F32), jnp.tile(gain.astype(F32), HG_HEADS).reshape(1, HG_W),
      *consts)


def _compress_kernel(k16_ref, v16_ref, kpa_ref, kpb_ref, kw1a_ref, kw1b_ref, kw2_ref,
                     vpa_ref, vpb_ref, vw1a_ref, vw1b_ref, vw2_ref, kc_ref, vct_ref, *, nc):
    def hidden(x_ref, pa, pb, w1a, w1b):
        x16 = jnp.concatenate([x_ref[0, pl.ds(j, nc, stride=CMP_STRIDE), :] for j in range(CMP_STRIDE)], axis=1)
        first = _dot((x16 + pa).astype(BF), w1a)
        second = _dot((x16 + pb).astype(BF), w1b)
        pre = first + pltpu.roll(second, nc - 1, 0)
        rows = lax.broadcasted_iota(jnp.int32, pre.shape, 0)
        pre = jnp.where(rows < nc - 1, pre, 0.0)
        return _gelu_tanh(pre).astype(BF)

    hk = hidden(k16_ref, kpa_ref[...], kpb_ref[...], kw1a_ref[...], kw1b_ref[...])
    kc_ref[0] = _dot(hk, kw2_ref[...]).astype(BF)
    hv = hidden(v16_ref, vpa_ref[...], vpb_ref[...], vw1a_ref[...], vw1b_ref[...])
    vct_ref[0] = _dot_nt(vw2_ref[...], hv).astype(BF)


def _compress_params(pos, w1, w2, transpose_out=False):
    eye = jnp.eye(NSA_G, dtype=F32)
    w1r = w1.astype(F32).reshape(CMP_LEN, NSA_DH, NSA_DH)
    wexp = jnp.einsum('jde,gh->jgdhe', w1r, eye).reshape(CMP_LEN, KV_W, KV_W)
    half = CMP_LEN // 2
    w1a = wexp[:half].reshape(half * KV_W, KV_W).astype(BF)
    w1b = wexp[half:].reshape(half * KV_W, KV_W).astype(BF)
    pt = jnp.broadcast_to(pos.astype(F32)[:, None, :], (CMP_LEN, NSA_G, NSA_DH))
    pa = pt[:half].reshape(1, half * KV_W)
    pb = pt[half:].reshape(1, half * KV_W)
    w2bd = jnp.einsum('de,gh->gdhe', w2.astype(F32), eye).reshape(KV_W, KV_W).astype(BF)
    return pa, pb, w1a, w1b, (w2bd.T if transpose_out else w2bd)


def _compress(kcr, vcr, kparams, vparams):
    b, l, _ = kcr.shape
    nc = l // CMP_STRIDE
    per_b = lambda r, cc: pl.BlockSpec((1, r, cc), lambda bi: (bi, 0, 0))
    params = list(kparams) + list(vparams)
    return pl.pallas_call(
        functools.partial(_compress_kernel, nc=nc),
        grid=(b,),
        in_specs=[per_b(l, KV_W), per_b(l, KV_W)] + [_const_spec(p.shape) for p in params],
        out_specs=[per_b(nc, KV_W), per_b(KV_W, nc)],
        out_shape=[jax.ShapeDtypeStruct((b, nc, KV_W), BF), jax.ShapeDtypeStruct((b, KV_W, nc), BF)],
        compiler_params=_cparams("arbitrary"),
        name="compress",
    )(kcr, vcr, *params)


def _t5_bucket(dist):
    n = jnp.maximum(dist, 0)
    max_exact = REL_BUCKETS // 2
    nf = jnp.maximum(n, max_exact).astype(jnp.float32)
    large = max_exact + (jnp.log(nf / max_exact) / math.log(REL_MAX_DIST / max_exact)
                         * (REL_BUCKETS - max_exact)).astype(jnp.int32)
    large = jnp.minimum(large, REL_BUCKETS - 1)
    return jnp.where(n < max_exact, n, large)


def _bias_tables(rel_bias, l):
    tab = rel_bias.astype(F32) * LOG2E
    by_dist = tab[_t5_bucket(jnp.arange(l))].T
    nc = l // CMP_STRIDE
    w = 2 * K_TILE

    def shifted_rows(v, n_rows, step):
        flat = jnp.tile(v, (1,) * (v.ndim - 1) + (n_rows,))[..., :n_rows * (w - step)]
        return flat.reshape(v.shape[:-1] + (n_rows, w - step))[..., :Q_TILE]

    def group_layout(t, valid):
        _, n, r, q = t.shape
        t = jnp.where(jnp.asarray(valid)[None], t, NEG_INF)
        return t.reshape(NSA_G, NSA_R, n, r, q).transpose(0, 2, 3, 1, 4).reshape(NSA_G, n, r, NSA_R * q)

    tq = np.arange(Q_TILE)[None, None, :]
    key = np.arange(K_TILE)[None, :, None]
    blocks = jnp.pad(by_dist, ((0, 0), (K_TILE, w)), mode='edge').reshape(NSA_H, -1, K_TILE)
    vec = jnp.concatenate([blocks[:, 1:N_TOEPLITZ + 1], blocks[:, 0:N_TOEPLITZ]], axis=-1)
    toep = shifted_rows(vec, K_TILE, 1)
    masked_tile = jnp.full((NSA_G, 1, K_TILE, GR_LANES), NEG_INF, F32)
    d = np.arange(N_TOEPLITZ)[:, None, None] * K_TILE + tq - key
    tz = jnp.concatenate([masked_tile, group_layout(toep, d >= 0)], axis=1)
    d = (N_WIN_TILES - 1 - np.arange(N_WIN_TILES))[:, None, None] * K_TILE + tq - key
    wz = group_layout(toep[:, N_WIN_TILES - 1::-1], (d >= 0) & (d < WIN))
    wz = jnp.concatenate([masked_tile, wz, masked_tile], axis=1)
    na = (2 * nc - SUBLANES) // SUBLANES
    c0 = CMP_STRIDE * (nc - SUBLANES) - (CMP_LEN - 1)
    front = K_TILE * (na + 1)
    off = c0 % K_TILE
    padded = jnp.pad(by_dist, ((0, 0), (front, w)), mode='edge')
    nb = (padded.shape[1] - off) // K_TILE
    blocks = padded[:, off:off + nb * K_TILE].reshape(NSA_H, nb, K_TILE)
    k0 = (front + c0 - off) // K_TILE
    vec = jnp.concatenate([blocks[:, k0 - na + 1:k0 + 1][:, ::-1], blocks[:, k0 - na:k0][:, ::-1]], axis=-1)
    cmp_rows = shifted_rows(vec, SUBLANES, CMP_STRIDE).reshape(NSA_H, 1, na * SUBLANES, Q_TILE)
    u = np.arange(na * SUBLANES)[None, :, None]
    mc = group_layout(cmp_rows, tq - CMP_STRIDE * (u - (nc - SUBLANES)) - (CMP_LEN - 1) >= 0)[:, 0]
    return tz, wz, mc


def _nsa_kernel(qt_ref, gt_ref, kc_ref, vct_ref, ksl_ref, kwn_ref, vslt_ref, vwnt_ref, mc_ref, tz_ref, wz_ref,
                ind_ref, expand_ref, o_ref, imp_scr, y_scr, *group_scr, nc, nsb):
    per_group = len(group_scr) // NSA_G
    SLC, WIN_BRANCH = 0, 1

    def branch_scr(g, branch):
        m_scr, acc_scr, s_a, s_b, max_a, max_b = group_scr[g * per_group + 3 + 6 * branch:
                                                           g * per_group + 9 + 6 * branch]
        return m_scr, acc_scr, (s_a, s_b), (max_a, max_b)

    step = pl.program_id(1)
    tiles = [QT_STEP * step + a for a in range(QT_STEP)]
    t0 = step * STEP_Q
    n_sel = min(N_SEL, nsb)
    cmp_per_q = Q_TILE // CMP_STRIDE
    cmp_per_slc = SLC_LEN // CMP_STRIDE
    pair = 2 * K_TILE
    n_last = tiles[-1] // 2
    ones_rows = (lax.broadcasted_iota(jnp.int32, (SLAB, pair), 0) == 0).astype(BF)

    def lanes(parts):
        return jnp.concatenate(parts, axis=1)

    def per_head_lanes(x):
        return lanes([x[:, a * Q_TILE:(a + 1) * Q_TILE] for a in range(QT_STEP) for _ in range(NSA_R)])

    def group_rows(g):
        return slice(g * NSA_DH, (g + 1) * NSA_DH)

    def fill(branch, kp, buf):
        k0 = pl.multiple_of(jnp.maximum(kp, 0) * pair, pair)
        for g in range(NSA_G):
            qt_scr, slab_scr = group_scr[g * per_group:g * per_group + 2]
            _, _, s_bufs, max_bufs = branch_scr(g, branch)
            if branch == SLC:
                lhs = jnp.concatenate([ksl_ref[0, pl.ds(k0, pair), :], ind_ref[...]], axis=1)
                slab = per_head_lanes(slab_scr[pl.ds(pl.multiple_of(kp * SLAB, SLAB), SLAB), :])
                rhs = jnp.concatenate([qt_scr[...], slab, jnp.zeros((KV_W - SLAB, STEP_LANES), BF)], axis=0)
                bias = [lanes([tz_ref[g, jnp.clip(qi - (2 * kp + j), -1, N_TOEPLITZ - 1) + 1] for qi in tiles])
                        for j in range(2)]
            else:
                lhs, rhs = kwn_ref[0, pl.ds(k0, pair), :], qt_scr[...]
                bias = [lanes([wz_ref[g, jnp.where(kp < 0, 0, jnp.clip(2 * kp + j - (qi - (N_WIN_TILES - 1)) + 1,
                                                                       0, N_WIN_TILES + 1))]
                               for qi in tiles]) for j in range(2)]
            s = _dot(lhs, rhs) + jnp.concatenate(bias, axis=0)
            s_bufs[buf][...] = s
            max_bufs[buf][0:1, :] = jnp.max(s, axis=0, keepdims=True)

    def drain(branch, kp, buf):
        vt_ref = vslt_ref if branch == SLC else vwnt_ref
        k0 = pl.multiple_of(jnp.maximum(kp, 0) * pair, pair)
        for g in range(NSA_G):
            m_scr, acc_scr, s_bufs, max_bufs = branch_scr(g, branch)
            m_prev = m_scr[0:1, :]
            m_new = jnp.maximum(m_prev, max_bufs[buf][0:1, :])
            p = jnp.exp2(s_bufs[buf][...] - m_new).astype(BF)
            vt = jnp.concatenate([vt_ref[0, group_rows(g), pl.ds(k0, pair)], ones_rows], axis=0)
            acc_scr[...] = jnp.exp2(m_prev - m_new) * acc_scr[...] + _dot(vt, p)
            m_scr[0:1, :] = m_new

    def result(g, branch):
        _, acc_scr, _, _ = branch_scr(g, branch)
        return acc_scr[0:NSA_DH] * (1.0 / acc_scr[NSA_DH:NSA_DH + 1])

    for g in range(NSA_G):
        qt_scr = group_scr[g * per_group]
        zeros = jnp.zeros((NSA_DH, Q_TILE), BF)
        parts = []
        for a in range(QT_STEP):
            for r in range(NSA_R):
                h = g * NSA_R + r
                qh = qt_ref[0, h * NSA_DH:(h + 1) * NSA_DH, a * Q_TILE:(a + 1) * Q_TILE]
                parts.append(jnp.concatenate([qh, zeros] if g == 0 else [zeros, qh], axis=0))
        qt_scr[...] = lanes(parts)
        for branch in (SLC, WIN_BRANCH):
            m_scr, acc_scr, _, _ = branch_scr(g, branch)
            m_scr[...] = jnp.full_like(m_scr, NEG_INF)
            acc_scr[...] = jnp.zeros_like(acc_scr)

    lane = lax.broadcasted_iota(jnp.int32, (1, STEP_LANES), 1)
    t_lane = t0 + (lane // GR_LANES) * Q_TILE + lane % Q_TILE

    def compressed(rows):
        for g in range(NSA_G):
            qt = group_scr[g * per_group][...]
            bias_c = lanes([mc_ref[g, pl.ds(pl.multiple_of((nc - SUBLANES) - cmp_per_q * qi, SUBLANES), rows), :]
                            for qi in tiles])
            s = _dot(kc_ref[0, 0:rows, :], qt) + bias_c
            m = jnp.max(s, axis=0, keepdims=True)
            e = jnp.exp2(s - m)
            lsum = jnp.sum(e, axis=0, keepdims=True)
            inv = jnp.where(t_lane >= CMP_LEN - 1, 1.0 / jnp.maximum(lsum, TINY), 0.0)
            p = e * inv
            group_scr[g * per_group + 2][...] = _dot(vct_ref[0, group_rows(g), 0:rows], p.astype(BF))
            for a in range(QT_STEP):
                first = a * GR_LANES
                imp = p[:, first:first + Q_TILE]
                for r in range(1, NSA_R):
                    imp = imp + p[:, first + r * Q_TILE:first + (r + 1) * Q_TILE]
                imp_scr[g, a, 0:SUBLANES, :] = jnp.zeros((SUBLANES, Q_TILE), F32)
                imp_scr[g, a, SUBLANES:SUBLANES + rows, :] = imp
                if rows < nc:
                    imp_scr[g, a, SUBLANES + rows:SUBLANES + nc, :] = jnp.zeros((nc - rows, Q_TILE), F32)

    n_variants = max(nc // CMP_ROWS, 1)
    visible = cmp_per_q * QT_STEP * (step + 1)
    variant = jnp.minimum((visible + CMP_ROWS - 1) // CMP_ROWS, n_variants)
    for v in range(1, n_variants + 1):
        pl.when(variant == v)(functools.partial(compressed, min(v * CMP_ROWS, nc)))

    block_scores = []
    for g in range(NSA_G):
        p_slc = []
        for a in range(QT_STEP):
            a_k = [imp_scr[g, a, pl.ds(SUBLANES - 1 + k, nsb, stride=cmp_per_slc), :]
                   for k in range(cmp_per_slc + 1)]
            acc = a_k[0] + a_k[1]
            for k in range(1, cmp_per_slc):
                acc = acc + a_k[k] + a_k[k + 1]
            p_slc.append(acc)
        block_scores.append(lanes(p_slc))

    jidx = lax.broadcasted_iota(jnp.int32, (nsb, STEP_Q), 0)
    tq = t0 + lax.broadcasted_iota(jnp.int32, (nsb, STEP_Q), 1)
    valid = jidx * SLC_LEN <= tq

    def store_masks(masks):
        for g in range(NSA_G):
            group_scr[g * per_group + 1][...] = _dot(expand_ref[...], masks[g].astype(BF)).astype(BF)

    n_forced = 3
    few_blocks_steps = (n_sel * SLC_LEN) // STEP_Q

    @pl.when(step < few_blocks_steps)
    def _():
        store_masks([jnp.where(valid, 0.0, NEG_INF)] * NSA_G)

    def pick(rows):
        j = lax.broadcasted_iota(jnp.int32, (rows, STEP_Q), 0)
        tq_r = t0 + lax.broadcasted_iota(jnp.int32, (rows, STEP_Q), 1)
        valid_r = j * SLC_LEN <= tq_r
        cur = tq_r // SLC_LEN
        forced = (j == 0) | (j == cur) | (j == cur - 1)
        masks = []
        for g in range(NSA_G):
            score = jnp.where(forced, -jnp.inf, jnp.where(valid_r, block_scores[g][0:rows], NEG_INF))
            for _ in range(n_sel - n_forced):
                mx = jnp.max(score, axis=0, keepdims=True)
                first = jnp.min(jnp.where(score == mx, j, nsb), axis=0, keepdims=True)
                score = jnp.where(j == first, -jnp.inf, score)
            mask = jnp.where(score == -jnp.inf, 0.0, NEG_INF)
            if rows < nsb:
                mask = jnp.concatenate([mask, jnp.full((nsb - rows, STEP_Q), NEG_INF, F32)], axis=0)
            masks.append(mask)
        store_masks(masks)

    n_pick_variants = max(nsb // PICK_ROWS, 1)
    blocks_started = (STEP_Q // SLC_LEN) * (step + 1)
    pick_variant = jnp.minimum((blocks_started + PICK_ROWS - 1) // PICK_ROWS, n_pick_variants)
    for v in range(1, n_pick_variants + 1):
        pl.when((step >= few_blocks_steps) & (pick_variant == v))(functools.partial(pick, min(v * PICK_ROWS, nsb)))

    n_win_pairs = (QT_STEP + N_WIN_TILES - 1 + 1) // 2
    win_first = n_last - (n_win_pairs - 1)
    fill(SLC, jnp.int32(0), 0)

    def slc_body(j, carry):
        kp = 2 * j
        fill(SLC, kp + 1, 1)
        drain(SLC, kp, 0)
        fill(SLC, kp + 2, 0)
        drain(SLC, kp + 1, 1)
        return carry

    lax.fori_loop(0, n_last // 2, slc_body, 0)

    @pl.when(n_last % 2 == 1)
    def _():
        fill(SLC, n_last, 1)
        fill(WIN_BRANCH, win_first, 0)
        drain(SLC, n_last - 1, 0)
        drain(SLC, n_last, 1)

    @pl.when(n_last % 2 == 0)
    def _():
        fill(WIN_BRANCH, win_first, 0)
        drain(SLC, n_last, 0)

    for i in range(1, n_win_pairs):
        fill(WIN_BRANCH, win_first + i, i % 2)
        drain(WIN_BRANCH, win_first + i - 1, (i - 1) % 2)
    drain(WIN_BRANCH, win_first + n_win_pairs - 1, (n_win_pairs - 1) % 2)

    for g in range(NSA_G):
        def gate(branch):
            return lanes([gt_ref[0, pl.ds((g * NSA_R + r) * 3 + branch, 1), a * Q_TILE:(a + 1) * Q_TILE]
                          for a in range(QT_STEP) for r in range(NSA_R)])

        yt = (gate(0) * group_scr[g * per_group + 2][...] + gate(1) * result(g, SLC)
              + gate(2) * result(g, WIN_BRANCH))
        for a in range(QT_STEP):
            for r in range(NSA_R):
                h = g * NSA_R + r
                first = (a * NSA_R + r) * Q_TILE
                y_scr[h * NSA_DH:(h + 1) * NSA_DH, a * Q_TILE:(a + 1) * Q_TILE] = yt[:, first:first + Q_TILE]

    o_ref[0] = y_scr[...].T


def _nsa(qt, gt, kc, vct, ksw, vt, tz, wz, mc):
    b, _, l = qt.shape
    nc = l // CMP_STRIDE
    nsb = l // SLC_LEN
    per_b = lambda shape, idx: pl.BlockSpec((1,) + shape, lambda bi, i: (bi,) + idx, pipeline_mode=pl.Buffered(1))
    assert l % STEP_Q == 0
    pair = 2 * K_TILE
    blocks_per_pair = pair // SLC_LEN
    n_pairs = l // pair
    ind = (np.arange(pair)[:, None] // SLC_LEN == np.arange(KV_W)[None, :]).astype(np.float32)
    expand = np.zeros((n_pairs * SLAB, nsb), np.float32)
    for k in range(n_pairs):
        for j in range(blocks_per_pair):
            expand[k * SLAB + j, k * blocks_per_pair + j] = 1.0
    ind, expand = jnp.asarray(ind, BF), jnp.asarray(expand, BF)
    return pl.pallas_call(
        functools.partial(_nsa_kernel, nc=nc, nsb=nsb),
        grid=(b, l // STEP_Q),
        in_specs=[pl.BlockSpec((1, NSA_W, STEP_Q), lambda bi, i: (bi, 0, i)),
                  pl.BlockSpec((1, 32, STEP_Q), lambda bi, i: (bi, 0, i)),
                  per_b((nc, KV_W), (0, 0)), per_b((KV_W, nc), (0, 0)),
                  per_b((l, KV_W), (0, 0)), per_b((l, KV_W), (0, 1)),
                  per_b((KV_W, l), (0, 0)), per_b((KV_W, l), (1, 0)),
                  _const_spec(mc.shape), _const_spec(tz.shape), _const_spec(wz.shape),
                  _const_spec(ind.shape), _const_spec(expand.shape)],
        out_specs=pl.BlockSpec((1, STEP_Q, NSA_W), lambda bi, i: (bi, i, 0)),
        out_shape=jax.ShapeDtypeStruct((b, l, NSA_W), F32),
        scratch_shapes=[pltpu.VMEM((NSA_G, QT_STEP, nc + 4 * SUBLANES, Q_TILE), F32),
                        pltpu.VMEM((NSA_W, STEP_Q), F32)]
                       + NSA_G * ([pltpu.VMEM((KV_W, STEP_LANES), BF), pltpu.VMEM((n_pairs * SLAB, STEP_Q), BF),
                                   pltpu.VMEM((NSA_DH, STEP_LANES), F32)]
                                  + 2 * [pltpu.VMEM((SUBLANES, STEP_LANES), F32),
                                         pltpu.VMEM((NSA_DH + SLAB, STEP_LANES), F32),
                                         pltpu.VMEM((pair, STEP_LANES), F32), pltpu.VMEM((pair, STEP_LANES), F32),
                                         pltpu.VMEM((SUBLANES, STEP_LANES), F32),
                                         pltpu.VMEM((SUBLANES, STEP_LANES), F32)]),
        compiler_params=_cparams("arbitrary", "arbitrary"),
        name="nsa",
    )(qt, gt, kc, vct, ksw, ksw, vt, vt, mc, tz, wz, ind, expand)


def _split_w_in(w_in):
    w_in = w_in.astype(BF)
    o_q = N_TOK_A
    o_kv = o_q + NSA_W
    o_gate = o_kv + 6 * KV_W
    kv = lambda j: w_in[:, o_kv + j * KV_W:o_kv + (j + 1) * KV_W]
    w_tok = jnp.concatenate([w_in[:, :N_TOK_A], kv(0), kv(1), kv(2), kv(4)], axis=1).astype(BF)
    gates = w_in[:, o_gate:]
    pad = jnp.zeros((w_in.shape[0], 32 - gates.shape[1]), w_in.dtype)
    w_t = jnp.concatenate([w_in[:, o_q:o_kv], kv(3), kv(5), gates, pad], axis=1).T.astype(BF)
    return w_tok, w_t


def kernel(x, ffn1_norm, ffn1_w_gate, ffn1_w_up, ffn1_w_down, mix_norm, w_in, w_out, s5_lambda_re, s5_lambda_im, s5_log_dt, s5_b_re, s5_b_im, s5_c_re, s5_c_im, s5_d, s5_w_glu, hgrn_lb_logits, hgrn_norm, nsa_cmp_pos_k, nsa_cmp_w1_k, nsa_cmp_w2_k, nsa_cmp_pos_v, nsa_cmp_w1_v, nsa_cmp_w2_v, rel_bias, ffn2_norm, ffn2_w_gate, ffn2_w_up, ffn2_w_down, final_norm):
    b, l, d = x.shape
    depth = w_in.shape[0]
    gam = jax.nn.softmax(hgrn_lb_logits.astype(F32), axis=0)
    lower_bounds = jnp.cumsum(gam, axis=0) - gam[0:1]
    tz, wz, mc = _bias_tables(rel_bias, l)
    bf = lambda w: w.astype(BF)

    x2 = x.reshape(b * l, d)
    for i in range(depth):
        x2 = _ffn(x2, ffn1_norm[i], bf(ffn1_w_gate[i]), bf(ffn1_w_up[i]), bf(ffn1_w_down[i]))
        w_tok, w_t = _split_w_in(w_in[i])
        za, kcr, vcr, ksw, qt, vt, gt = _inproj(x2.reshape(b, l, d), mix_norm[i], w_tok, w_t)
        bblk, cblk, coef = _s5_params(s5_lambda_re[i], s5_lambda_im[i], s5_log_dt[i], s5_b_re[i], s5_b_im[i],
                                      s5_c_re[i], s5_c_im[i])
        y_s5 = _s5(za, bblk, cblk, coef, s5_d[i], s5_w_glu[i])
        y_hg = _hgrn(za, lower_bounds[i], hgrn_norm[i])
        kc, vct = _compress(kcr, vcr,
                            _compress_params(nsa_cmp_pos_k[i], nsa_cmp_w1_k[i], nsa_cmp_w2_k[i]),
                            _compress_params(nsa_cmp_pos_v[i], nsa_cmp_w1_v[i], nsa_cmp_w2_v[i], transpose_out=True))
        y_nsa = _nsa(qt, gt, kc, vct, ksw, vt, tz, wz, mc)
        proj = (y_s5.reshape(b * l, S5_W), y_hg.reshape(b * l, HG_W), y_nsa.reshape(b * l, NSA_W), bf(w_out[i]))
        x2 = _ffn(x2, ffn2_norm[i], bf(ffn2_w_gate[i]), bf(ffn2_w_up[i]), bf(ffn2_w_down[i]), proj=proj,
                  final_g=final_norm if i + 1 == depth else None)
    return x2.reshape(b, l, d)
```

```python
import functools
import math

import numpy as np
import jax
import jax.numpy as jnp
from jax import lax
from jax.experimental import pallas as pl
from jax.experimental.pallas import tpu as pltpu

BF = jnp.bfloat16
F32 = jnp.float32

EPS = 1e-6
NEG_INF = -1e30
TINY = 1e-30
LOG2E = math.log2(math.e)

S5_W = 256
S5_G = 16
S5_GROUP = 16
S5_P = 64
S5_N = S5_G * S5_P
HG_W = 256
HG_HEADS = 4
HG_D = 64
NSA_W = 512
NSA_DH = 64
NSA_H = 8
NSA_G = 2
NSA_R = 4
KV_W = NSA_G * NSA_DH
CMP_LEN = 32
CMP_STRIDE = 16
SLC_LEN = 64
N_SEL = 16
WIN = 512
REL_BUCKETS = 32
REL_MAX_DIST = 1024

Q_TILE = 128
K_TILE = 128
GR_LANES = NSA_R * Q_TILE
QT_STEP = 4
STEP_Q = QT_STEP * Q_TILE
STEP_LANES = QT_STEP * GR_LANES
N_TOEPLITZ = REL_MAX_DIST // K_TILE + 2
N_WIN_TILES = WIN // K_TILE + 1
SLAB = 16
CMP_ROWS = 128
PICK_ROWS = 32
HG_CHUNK = 64
HG_GROUP = 4
HG_LEVELS = 6
SUBLANES = 8
HG_FINE_LEVELS = 3
VMEM_LIMIT_BYTES = 56 * 1024 * 1024


def _cparams(*sem):
    return pltpu.CompilerParams(dimension_semantics=sem, vmem_limit_bytes=VMEM_LIMIT_BYTES)


def _const_spec(shape):
    nd = len(shape)
    return pl.BlockSpec(shape, lambda *_: (0,) * nd, pipeline_mode=pl.Buffered(1))


def _rms(x, g_row):
    ms = jnp.mean(x * x, axis=-1, keepdims=True)
    return x * lax.rsqrt(ms + EPS) * g_row


def _silu(x):
    return x * jax.nn.sigmoid(x)


def _gelu_tanh(x):
    return 0.5 * x * (1.0 + jnp.tanh(math.sqrt(2.0 / math.pi) * (x + 0.044715 * (x * x * x))))


def _dot(a, b):
    return jnp.dot(a, b, preferred_element_type=F32)


def _dot_nt(a, b):
    return lax.dot_general(a, b, (((1,), (1,)), ((), ())), preferred_element_type=F32)


def _dot_tn(a, b):
    return lax.dot_general(a, b, (((0,), (0,)), ((), ())), preferred_element_type=F32)


def _dot_exact_lhs(c_bf, x):
    hi = x.astype(BF)
    r1 = x - hi.astype(F32)
    mid = r1.astype(BF)
    lo = (r1 - mid.astype(F32)).astype(BF)
    return _dot(c_bf, hi) + _dot(c_bf, mid) + _dot(c_bf, lo)


def _dot_exact_rhs(x, c_bf):
    hi = x.astype(BF)
    r1 = x - hi.astype(F32)
    mid = r1.astype(BF)
    lo = (r1 - mid.astype(F32)).astype(BF)
    return _dot(hi, c_bf) + _dot(mid, c_bf) + _dot(lo, c_bf)


def _ffn_kernel(*refs, n_chunks, tf, with_proj, with_final):
    it = iter(refs)
    x_ref = next(it)
    if with_proj:
        ys5_ref, yhg_ref, ynsa_ref, wo_ref = next(it), next(it), next(it), next(it)
    g_ref, wg_ref, wu_ref, wd_ref = next(it), next(it), next(it), next(it)
    if with_final:
        fg_ref = next(it)
    o_ref, h_scr, a_scr = next(it), next(it), next(it)

    x = x_ref[...]
    if with_proj:
        x = (x + _dot(ys5_ref[...].astype(BF), wo_ref[0:S5_W, :])
             + _dot(yhg_ref[...].astype(BF), wo_ref[S5_W:S5_W + HG_W, :])
             + _dot(ynsa_ref[...].astype(BF), wo_ref[S5_W + HG_W:, :]))
    h_scr[...] = _rms(x, g_ref[...]).astype(BF)
    for c in range(n_chunks):
        sl = slice(c * tf, (c + 1) * tf)
        h = h_scr[...]
        gate = _dot(h, wg_ref[:, sl])
        up = _dot(h, wu_ref[:, sl])
        a_scr[:, sl] = (_silu(gate) * up).astype(BF)
    x = x + 0.5 * _dot(a_scr[...], wd_ref[...])
    if with_final:
        x = _rms(x, fg_ref[...])
    o_ref[...] = x


def _ffn(x2, g, wg, wu, wd, proj=None, final_g=None, tm=512, tf=256):
    n, d = x2.shape
    dff = wg.shape[1]
    assert n % tm == 0 and dff % tf == 0
    row = lambda i: (i, 0)
    in_specs = [pl.BlockSpec((tm, d), row)]
    args = [x2]
    if proj is not None:
        ys5, yhg, ynsa, wo = proj
        in_specs += [pl.BlockSpec((tm, S5_W), row), pl.BlockSpec((tm, HG_W), row),
                     pl.BlockSpec((tm, NSA_W), row), _const_spec(wo.shape)]
        args += [ys5, yhg, ynsa, wo]
    in_specs += [_const_spec((1, d)), _const_spec(wg.shape), _const_spec(wu.shape), _const_spec(wd.shape)]
    args += [g.reshape(1, d), wg, wu, wd]
    if final_g is not None:
        in_specs.append(_const_spec((1, d)))
        args.append(final_g.reshape(1, d))
    kern = functools.partial(_ffn_kernel, n_chunks=dff // tf, tf=tf,
                             with_proj=proj is not None, with_final=final_g is not None)
    return pl.pallas_call(
        kern,
        grid=(n // tm,),
        in_specs=in_specs,
        out_specs=pl.BlockSpec((tm, d), row),
        out_shape=jax.ShapeDtypeStruct((n, d), F32),
        scratch_shapes=[pltpu.VMEM((tm, d), BF), pltpu.VMEM((tm, dff), BF)],
        compiler_params=_cparams("arbitrary"),
        name="ffn",
    )(*args)


N_TOK_A = S5_W + 4 * HG_W
N_T_ROWS = NSA_W + 2 * KV_W + 32


def _inproj_kernel(x_ref, g_ref, wtok_ref, wt_ref, za_ref, kcr_ref, vcr_ref, ksw_ref, qt_ref, vt_ref, gt_ref):
    h = _rms(x_ref[0], g_ref[...]).astype(BF)
    for c in range(N_TOK_A // 256):
        sl = slice(c * 256, (c + 1) * 256)
        za_ref[0, :, sl] = _dot(h, wtok_ref[:, sl])
    kcr_ref[0] = _dot(h, wtok_ref[:, N_TOK_A:N_TOK_A + KV_W])
    vcr_ref[0] = _dot(h, wtok_ref[:, N_TOK_A + KV_W:N_TOK_A + 2 * KV_W])
    ksw_ref[0] = _dot(h, wtok_ref[:, N_TOK_A + 2 * KV_W:N_TOK_A + 4 * KV_W]).astype(BF)
    qt_ref[0] = (_dot_nt(wt_ref[0:NSA_W, :], h) * (NSA_DH ** -0.5 * LOG2E)).astype(BF)
    vt_ref[0] = _dot_nt(wt_ref[NSA_W:NSA_W + 2 * KV_W, :], h).astype(BF)
    gt_ref[0] = jax.nn.sigmoid(_dot_nt(wt_ref[NSA_W + 2 * KV_W:, :], h))


def _inproj(x3, g, w_tok, w_t, tm=512):
    b, l, d = x3.shape
    assert l % tm == 0
    tok = lambda w: pl.BlockSpec((1, tm, w), lambda bi, i: (bi, i, 0))
    tr = lambda r: pl.BlockSpec((1, r, tm), lambda bi, i: (bi, 0, i))
    return pl.pallas_call(
        _inproj_kernel,
        grid=(b, l // tm),
        in_specs=[tok(d), _const_spec((1, d)), _const_spec(w_tok.shape), _const_spec(w_t.shape)],
        out_specs=[tok(N_TOK_A), tok(KV_W), tok(KV_W), tok(2 * KV_W), tr(NSA_W), tr(2 * KV_W), tr(32)],
        out_shape=[jax.ShapeDtypeStruct((b, l, N_TOK_A), F32),
                   jax.ShapeDtypeStruct((b, l, KV_W), F32),
                   jax.ShapeDtypeStruct((b, l, KV_W), F32),
                   jax.ShapeDtypeStruct((b, l, 2 * KV_W), BF),
                   jax.ShapeDtypeStruct((b, NSA_W, l), BF),
                   jax.ShapeDtypeStruct((b, 2 * KV_W, l), BF),
                   jax.ShapeDtypeStruct((b, 32, l), F32)],
        compiler_params=_cparams("arbitrary", "arbitrary"),
        name="inproj",
    )(x3, g.reshape(1, d), w_tok, w_t)


def _s5_kernel(u_ref, bblk_ref, cblk_ref, coef_ref, d_ref, wglu_ref, y_ref, xs_scr, carry_scr, *, tc):
    @pl.when(pl.program_id(1) == 0)
    def _():
        carry_scr[...] = jnp.zeros_like(carry_scr)

    u = u_ref[0]
    xs_scr[...] = _dot(u.astype(BF), bblk_ref[...])

    def body(r, carry):
        cre, cim = carry
        row = pl.multiple_of(r * SUBLANES, SUBLANES)
        xre = xs_scr[pl.ds(row, SUBLANES), 0:S5_N]
        xim = xs_scr[pl.ds(row, SUBLANES), S5_N:2 * S5_N]
        for idx, k in enumerate((1, 2, 4)):
            are, aim = coef_ref[idx, 0], coef_ref[idx, 1]
            sre, sim = pltpu.roll(xre, k, 0), pltpu.roll(xim, k, 0)
            xre, xim = xre + (are * sre - aim * sim), xim + (are * sim + aim * sre)
        pre, pim = coef_ref[3, 0], coef_ref[3, 1]
        xre, xim = xre + (pre * cre - pim * cim), xim + (pre * cim + pim * cre)
        xs_scr[pl.ds(row, SUBLANES), 0:S5_N] = xre
        xs_scr[pl.ds(row, SUBLANES), S5_N:2 * S5_N] = xim
        return xre[SUBLANES - 1:SUBLANES], xim[SUBLANES - 1:SUBLANES]

    cre, cim = lax.fori_loop(0, tc // SUBLANES, body, (carry_scr[0:1], carry_scr[1:2]))
    carry_scr[0:1] = cre
    carry_scr[1:2] = cim

    y = _dot(xs_scr[...].astype(BF), cblk_ref[...]) + d_ref[...] * u
    y = _gelu_tanh(y)
    y_ref[0] = y * jax.nn.sigmoid(_dot(y.astype(BF), wglu_ref[...]))


def _s5_params(lam_re, lam_im, log_dt, b_re, b_im, c_re, c_im):
    lr, li = lam_re.astype(F32), lam_im.astype(F32)
    dt = jnp.exp(log_dt.astype(F32))[:, None]
    mag = jnp.exp(lr * dt)
    ab_re, ab_im = mag * jnp.cos(li * dt), mag * jnp.sin(li * dt)
    den = lr * lr + li * li
    nr, ni = ab_re - 1.0, ab_im
    g_re = (nr * lr + ni * li) / den
    g_im = (ni * lr - nr * li) / den
    br, bi = b_re.astype(F32), b_im.astype(F32)
    bb_re = g_re[..., None] * br - g_im[..., None] * bi
    bb_im = g_re[..., None] * bi + g_im[..., None] * br
    eye = jnp.eye(S5_G, dtype=F32)
    blk = lambda w: jnp.einsum('gph,gk->ghkp', w, eye).reshape(S5_W, S5_N)
    bblk = jnp.concatenate([blk(bb_re), blk(bb_im)], axis=1).astype(BF)
    cblk_f = lambda w: jnp.einsum('ghp,gk->gpkh', w, eye).reshape(S5_N, S5_W)
    cblk = jnp.concatenate([cblk_f(c_re.astype(F32)), -cblk_f(c_im.astype(F32))], axis=0).astype(BF)
    are, aim = ab_re.reshape(1, S5_N), ab_im.reshape(1, S5_N)
    pw = [(are, aim)]
    for _ in range(SUBLANES - 1):
        pr, pi = pw[-1]
        pw.append((pr * are - pi * aim, pr * aim + pi * are))
    rows = np.arange(SUBLANES)[:, None]
    coef = []
    for k in (1, 2, 4):
        m = jnp.asarray((rows >= k).astype(np.float32))
        coef.append(jnp.stack([m * pw[k - 1][0], m * pw[k - 1][1]]))
    coef.append(jnp.stack([jnp.concatenate([p[0] for p in pw], axis=0), jnp.concatenate([p[1] for p in pw], axis=0)]))
    return bblk, cblk, jnp.stack(coef)


def _s5(za, bblk, cblk, coef, d, w_glu, tc=512):
    b, l, _ = za.shape
    assert l % tc == 0
    return pl.pallas_call(
        functools.partial(_s5_kernel, tc=tc),
        grid=(b, l // tc),
        in_specs=[pl.BlockSpec((1, tc, S5_W), lambda bi, i: (bi, i, 0)),
                  _const_spec(bblk.shape), _const_spec(cblk.shape), _const_spec(coef.shape),
                  _const_spec((1, S5_W)), _const_spec(w_glu.shape)],
        out_specs=pl.BlockSpec((1, tc, S5_W), lambda bi, i: (bi, i, 0)),
        out_shape=jax.ShapeDtypeStruct((b, l, S5_W), F32),
        scratch_shapes=[pltpu.VMEM((tc, 2 * S5_N), F32), pltpu.VMEM((2, S5_N), F32)],
        compiler_params=_cparams("arbitrary", "arbitrary"),
        name="s5",
    )(za, bblk, cblk, coef, d.reshape(1, S5_W).astype(F32), w_glu.astype(BF))


def _hgrn_constants():
    c = HG_CHUNK
    t = np.arange(c)[:, None]
    u = np.arange(c)[None, :]
    mats = [(u <= t)]
    masks = []
    for lv in range(HG_LEVELS):
        n = c >> lv
        half = n // 2
        ref = (t // n) * n + half - 1
        lower = (t % n) >= half
        if half < SUBLANES:
            mats.append(np.where(lower, (u > ref) & (u <= t), (u > t) & (u <= ref)))
        same = (t // n) == (u // n)
        masks.append(same & lower & ((u % n) < half))
    masks.append(t == u)
    gall = np.concatenate(mats, axis=0).astype(np.float32)
    mstk = np.stack([np.tile(m, (1, HG_HEADS)) for m in masks]).astype(np.float32)
    lane_head = np.arange(HG_W)[None, :] // HG_D
    hmask = (np.repeat(np.arange(HG_HEADS), c)[:, None] == lane_head).astype(np.float32)
    bd = (np.arange(HG_W)[:, None] // HG_D == lane_head).astype(np.float32)
    return gall, mstk, hmask, bd


def _hgrn_kernel(q_ref, f_ref, i_ref, g_ref, lb_ref, gain_ref, gall_ref, mstk_ref, hmask_ref, bd_ref, bdn_ref,
                 o_ref, st_scr, *, tt):
    c = HG_CHUNK

    @pl.when(pl.program_id(1) == 0)
    def _():
        st_scr[...] = jnp.zeros_like(st_scr)

    lb = lb_ref[...]
    hmask = hmask_ref[...]

    def level_decay(b, sums, lv):
        n = c >> lv
        half = n // 2
        if half < SUBLANES:
            fine = lv - (HG_LEVELS - HG_FINE_LEVELS)
            return jnp.exp(sums[(1 + fine) * c:(2 + fine) * c])
        pieces = []
        for blk in range(c // n):
            ref = blk * n + half - 1
            pieces.append(b[ref:ref + 1] - b[blk * n:blk * n + half])
            pieces.append(b[blk * n + half:(blk + 1) * n] - b[ref:ref + 1])
        return jnp.exp(jnp.concatenate(pieces, axis=0))

    def chunk(row, fl, sums, carry):
        qf = _silu(q_ref[0, pl.ds(row, c), :])
        kf = (1.0 - lb) * jax.nn.sigmoid(-fl)
        v = i_ref[0, pl.ds(row, c), :]
        v_bf = v.astype(BF)
        b = sums[0:c]
        e_b = jnp.exp(b)
        e_suf = jnp.exp(b[c - 1:c] - b)

        att = jnp.zeros((c, HG_HEADS * c), F32)
        for lv in range(HG_LEVELS + 1):
            if lv < HG_LEVELS:
                e = level_decay(b, sums, lv)
                z, w = qf * e, kf * e
            else:
                z, w = qf, kf
            ws = (jnp.concatenate([w] * HG_HEADS, axis=0) * hmask).astype(BF)
            att = att + mstk_ref[lv] * _dot_nt(z.astype(BF), ws)
        o = _dot(att.astype(BF), (jnp.concatenate([v] * HG_HEADS, axis=0) * hmask).astype(BF))

        st = st_scr[...]
        o = o + _dot_nt((qf * e_b).astype(BF), st.astype(BF))
        st_scr[...] = e_b[c - 1:c] * st + bd_ref[...] * _dot_tn(v_bf, (kf * e_suf).astype(BF))

        ms = _dot_exact_rhs(o * o, bdn_ref[...])
        o = o * lax.rsqrt(ms + EPS) * gain_ref[...]
        o_ref[0, pl.ds(row, c), :] = o * _silu(g_ref[0, pl.ds(row, c), :])
        return carry

    def chunk_group(cg, carry):
        rows = [pl.multiple_of((HG_GROUP * cg + j) * c, c) for j in range(HG_GROUP)]
        fls = [f_ref[0, pl.ds(r, c), :] for r in rows]
        lfs = [jnp.log(jnp.maximum(lb + (1.0 - lb) * jax.nn.sigmoid(fl), TINY)) for fl in fls]
        sums = _dot_exact_lhs(gall_ref[...], jnp.concatenate(lfs, axis=1))
        for j in range(HG_GROUP):
            carry = chunk(rows[j], fls[j], sums[:, j * HG_W:(j + 1) * HG_W], carry)
        return carry

    lax.fori_loop(0, tt // (HG_GROUP * c), chunk_group, 0)


def _hgrn(za, lb, gain, tt=512):
    b, l, _ = za.shape
    assert l % tt == 0
    gall, mstk, hmask, bd = _hgrn_constants()
    col = lambda j: pl.BlockSpec((1, tt, HG_W), lambda bi, i: (bi, i, j))
    consts = [jnp.asarray(gall, BF), jnp.asarray(mstk, F32), jnp.asarray(hmask, F32), jnp.asarray(bd, F32),
              jnp.asarray(bd / HG_D, BF)]
    return pl.pallas_call(
        functools.partial(_hgrn_kernel, tt=tt),
        grid=(b, l // tt),
        in_specs=[col(1), col(2), col(3), col(4), _const_spec((1, HG_W)), _const_spec((1, HG_W))]
                 + [_const_spec(x.shape) for x in consts],
        out_specs=pl.BlockSpec((1, tt, HG_W), lambda bi, i: (bi, i, 0)),
        out_shape=jax.ShapeDtypeStruct((b, l, HG_W), F32),
        scratch_shapes=[pltpu.VMEM((HG_W, HG_W), F32)],
        compiler_params=_cparams("arbitrary", "arbitrary"),
        name="hgrn2",
    )(za, za, za, za, lb.reshape(1, HG_W).astype(F32), jnp.tile(gain.astype(F32), HG_HEADS).reshape(1, HG_W),
      *consts)


def _compress_kernel(k16_ref, v16_ref, kpa_ref, kpb_ref, kw1a_ref, kw1b_ref, kw2_ref,
                     vpa_ref, vpb_ref, vw1a_ref, vw1b_ref, vw2_ref, kc_ref, vct_ref, *, nc):
    def hidden(x_ref, pa, pb, w1a, w1b):
        x16 = jnp.concatenate([x_ref[0, pl.ds(j, nc, stride=CMP_STRIDE), :] for j in range(CMP_STRIDE)], axis=1)
        first = _dot((x16 + pa).astype(BF), w1a)
        second = _dot((x16 + pb).astype(BF), w1b)
        pre = first + pltpu.roll(second, nc - 1, 0)
        rows = lax.broadcasted_iota(jnp.int32, pre.shape, 0)
        pre = jnp.where(rows < nc - 1, pre, 0.0)
        return _gelu_tanh(pre).astype(BF)

    hk = hidden(k16_ref, kpa_ref[...], kpb_ref[...], kw1a_ref[...], kw1b_ref[...])
    kc_ref[0] = _dot(hk, kw2_ref[...]).astype(BF)
    hv = hidden(v16_ref, vpa_ref[...], vpb_ref[...], vw1a_ref[...], vw1b_ref[...])
    vct_ref[0] = _dot_nt(vw2_ref[...], hv).astype(BF)


def _compress_params(pos, w1, w2, transpose_out=False):
    eye = jnp.eye(NSA_G, dtype=F32)
    w1r = w1.astype(F32).reshape(CMP_LEN, NSA_DH, NSA_DH)
    wexp = jnp.einsum('jde,gh->jgdhe', w1r, eye).reshape(CMP_LEN, KV_W, KV_W)
    half = CMP_LEN // 2
    w1a = wexp[:half].reshape(half * KV_W, KV_W).astype(BF)
    w1b = wexp[half:].reshape(half * KV_W, KV_W).astype(BF)
    pt = jnp.broadcast_to(pos.astype(F32)[:, None, :], (CMP_LEN, NSA_G, NSA_DH))
    pa = pt[:half].reshape(1, half * KV_W)
    pb = pt[half:].reshape(1, half * KV_W)
    w2bd = jnp.einsum('de,gh->gdhe', w2.astype(F32), eye).reshape(KV_W, KV_W).astype(BF)
    return pa, pb, w1a, w1b, (w2bd.T if transpose_out else w2bd)


def _compress(kcr, vcr, kparams, vparams):
    b, l, _ = kcr.shape
    nc = l // CMP_STRIDE
    per_b = lambda r, cc: pl.BlockSpec((1, r, cc), lambda bi: (bi, 0, 0))
    params = list(kparams) + list(vparams)
    return pl.pallas_call(
        functools.partial(_compress_kernel, nc=nc),
        grid=(b,),
        in_specs=[per_b(l, KV_W), per_b(l, KV_W)] + [_const_spec(p.shape) for p in params],
        out_specs=[per_b(nc, KV_W), per_b(KV_W, nc)],
        out_shape=[jax.ShapeDtypeStruct((b, nc, KV_W), BF), jax.ShapeDtypeStruct((b, KV_W, nc), BF)],
        compiler_params=_cparams("arbitrary"),
        name="compress",
    )(kcr, vcr, *params)


def _t5_bucket(dist):
    n = jnp.maximum(dist, 0)
    max_exact = REL_BUCKETS // 2
    nf = jnp.maximum(n, max_exact).astype(jnp.float32)
    large = max_exact + (jnp.log(nf / max_exact) / math.log(REL_MAX_DIST / max_exact)
                         * (REL_BUCKETS - max_exact)).astype(jnp.int32)
    large = jnp.minimum(large, REL_BUCKETS - 1)
    return jnp.where(n < max_exact, n, large)


def _bias_tables(rel_bias, l):
    tab = rel_bias.astype(F32) * LOG2E
    by_dist = tab[_t5_bucket(jnp.arange(l))].T
    nc = l // CMP_STRIDE
    w = 2 * K_TILE

    def shifted_rows(v, n_rows, step):
        flat = jnp.tile(v, (1,) * (v.ndim - 1) + (n_rows,))[..., :n_rows * (w - step)]
        return flat.reshape(v.shape[:-1] + (n_rows, w - step))[..., :Q_TILE]

    def group_layout(t, valid):
        _, n, r, q = t.shape
        t = jnp.where(jnp.asarray(valid)[None], t, NEG_INF)
        return t.reshape(NSA_G, NSA_R, n, r, q).transpose(0, 2, 3, 1, 4).reshape(NSA_G, n, r, NSA_R * q)

    tq = np.arange(Q_TILE)[None, None, :]
    key = np.arange(K_TILE)[None, :, None]
    blocks = jnp.pad(by_dist, ((0, 0), (K_TILE, w)), mode='edge').reshape(NSA_H, -1, K_TILE)
    vec = jnp.concatenate([blocks[:, 1:N_TOEPLITZ + 1], blocks[:, 0:N_TOEPLITZ]], axis=-1)
    toep = shifted_rows(vec, K_TILE, 1)
    masked_tile = jnp.full((NSA_G, 1, K_TILE, GR_LANES), NEG_INF, F32)
    d = np.arange(N_TOEPLITZ)[:, None, None] * K_TILE + tq - key
    tz = jnp.concatenate([masked_tile, group_layout(toep, d >= 0)], axis=1)
    d = (N_WIN_TILES - 1 - np.arange(N_WIN_TILES))[:, None, None] * K_TILE + tq - key
    wz = group_layout(toep[:, N_WIN_TILES - 1::-1], (d >= 0) & (d < WIN))
    wz = jnp.concatenate([masked_tile, wz, masked_tile], axis=1)
    na = (2 * nc - SUBLANES) // SUBLANES
    c0 = CMP_STRIDE * (nc - SUBLANES) - (CMP_LEN - 1)
    front = K_TILE * (na + 1)
    off = c0 % K_TILE
    padded = jnp.pad(by_dist, ((0, 0), (front, w)), mode='edge')
    nb = (padded.shape[1] - off) // K_TILE
    blocks = padded[:, off:off + nb * K_TILE].reshape(NSA_H, nb, K_TILE)
    k0 = (front + c0 - off) // K_TILE
    vec = jnp.concatenate([blocks[:, k0 - na + 1:k0 + 1][:, ::-1], blocks[:, k0 - na:k0][:, ::-1]], axis=-1)
    cmp_rows = shifted_rows(vec, SUBLANES, CMP_STRIDE).reshape(NSA_H, 1, na * SUBLANES, Q_TILE)
    u = np.arange(na * SUBLANES)[None, :, None]
    mc = group_layout(cmp_rows, tq - CMP_STRIDE * (u - (nc - SUBLANES)) - (CMP_LEN - 1) >= 0)[:, 0]
    return tz, wz, mc


def _nsa_kernel(qt_ref, gt_ref, kc_ref, vct_ref, ksl_ref, kwn_ref, vslt_ref, vwnt_ref, mc_ref, tz_ref, wz_ref,
                ind_ref, expand_ref, o_ref, imp_scr, y_scr, *group_scr, nc, nsb):
    per_group = len(group_scr) // NSA_G
    SLC, WIN_BRANCH = 0, 1

    def branch_scr(g, branch):
        m_scr, acc_scr, s_a, s_b, max_a, max_b = group_scr[g * per_group + 3 + 6 * branch:
                                                           g * per_group + 9 + 6 * branch]
        return m_scr, acc_scr, (s_a, s_b), (max_a, max_b)

    step = pl.program_id(1)
    tiles = [QT_STEP * step + a for a in range(QT_STEP)]
    t0 = step * STEP_Q
    n_sel = min(N_SEL, nsb)
    cmp_per_q = Q_TILE // CMP_STRIDE
    cmp_per_slc = SLC_LEN // CMP_STRIDE
    pair = 2 * K_TILE
    n_last = tiles[-1] // 2
    ones_rows = (lax.broadcasted_iota(jnp.int32, (SLAB, pair), 0) == 0).astype(BF)

    def lanes(parts):
        return jnp.concatenate(parts, axis=1)

    def per_head_lanes(x):
        return lanes([x[:, a * Q_TILE:(a + 1) * Q_TILE] for a in range(QT_STEP) for _ in range(NSA_R)])

    def group_rows(g):
        return slice(g * NSA_DH, (g + 1) * NSA_DH)

    def fill(branch, kp, buf):
        k0 = pl.multiple_of(jnp.maximum(kp, 0) * pair, pair)
        for g in range(NSA_G):
            qt_scr, slab_scr = group_scr[g * per_group:g * per_group + 2]
            _, _, s_bufs, max_bufs = branch_scr(g, branch)
            if branch == SLC:
                lhs = jnp.concatenate([ksl_ref[0, pl.ds(k0, pair), :], ind_ref[...]], axis=1)
                slab = per_head_lanes(slab_scr[pl.ds(pl.multiple_of(kp * SLAB, SLAB), SLAB), :])
                rhs = jnp.concatenate([qt_scr[...], slab, jnp.zeros((KV_W - SLAB, STEP_LANES), BF)], axis=0)
                bias = [lanes([tz_ref[g, jnp.clip(qi - (2 * kp + j), -1, N_TOEPLITZ - 1) + 1] for qi in tiles])
                        for j in range(2)]
            else:
                lhs, rhs = kwn_ref[0, pl.ds(k0, pair), :], qt_scr[...]
                bias = [lanes([wz_ref[g, jnp.where(kp < 0, 0, jnp.clip(2 * kp + j - (qi - (N_WIN_TILES - 1)) + 1,
                                                                       0, N_WIN_TILES + 1))]
                               for qi in tiles]) for j in range(2)]
            s = _dot(lhs, rhs) + jnp.concatenate(bias, axis=0)
            s_bufs[buf][...] = s
            max_bufs[buf][0:1, :] = jnp.max(s, axis=0, keepdims=True)

    def drain(branch, kp, buf):
        vt_ref = vslt_ref if branch == SLC else vwnt_ref
        k0 = pl.multiple_of(jnp.maximum(kp, 0) * pair, pair)
        for g in range(NSA_G):
            m_scr, acc_scr, s_bufs, max_bufs = branch_scr(g, branch)
            m_prev = m_scr[0:1, :]
            m_new = jnp.maximum(m_prev, max_bufs[buf][0:1, :])
            p = jnp.exp2(s_bufs[buf][...] - m_new).astype(BF)
            vt = jnp.concatenate([vt_ref[0, group_rows(g), pl.ds(k0, pair)], ones_rows], axis=0)
            acc_scr[...] = jnp.exp2(m_prev - m_new) * acc_scr[...] + _dot(vt, p)
            m_scr[0:1, :] = m_new

    def result(g, branch):
        _, acc_scr, _, _ = branch_scr(g, branch)
        return acc_scr[0:NSA_DH] * (1.0 / acc_scr[NSA_DH:NSA_DH + 1])

    for g in range(NSA_G):
        qt_scr = group_scr[g * per_group]
        zeros = jnp.zeros((NSA_DH, Q_TILE), BF)
        parts = []
        for a in range(QT_STEP):
            for r in range(NSA_R):
                h = g * NSA_R + r
                qh = qt_ref[0, h * NSA_DH:(h + 1) * NSA_DH, a * Q_TILE:(a + 1) * Q_TILE]
                parts.append(jnp.concatenate([qh, zeros] if g == 0 else [zeros, qh], axis=0))
        qt_scr[...] = lanes(parts)
        for branch in (SLC, WIN_BRANCH):
            m_scr, acc_scr, _, _ = branch_scr(g, branch)
            m_scr[...] = jnp.full_like(m_scr, NEG_INF)
            acc_scr[...] = jnp.zeros_like(acc_scr)

    lane = lax.broadcasted_iota(jnp.int32, (1, STEP_LANES), 1)
    t_lane = t0 + (lane // GR_LANES) * Q_TILE + lane % Q_TILE

    def compressed(rows):
        for g in range(NSA_G):
            qt = group_scr[g * per_group][...]
            bias_c = lanes([mc_ref[g, pl.ds(pl.multiple_of((nc - SUBLANES) - cmp_per_q * qi, SUBLANES), rows), :]
                            for qi in tiles])
            s = _dot(kc_ref[0, 0:rows, :], qt) + bias_c
            m = jnp.max(s, axis=0, keepdims=True)
            e = jnp.exp2(s - m)
            lsum = jnp.sum(e, axis=0, keepdims=True)
            inv = jnp.where(t_lane >= CMP_LEN - 1, 1.0 / jnp.maximum(lsum, TINY), 0.0)
            p = e * inv
            group_scr[g * per_group + 2][...] = _dot(vct_ref[0, group_rows(g), 0:rows], p.astype(BF))
            for a in range(QT_STEP):
                first = a * GR_LANES
                imp = p[:, first:first + Q_TILE]
                for r in range(1, NSA_R):
                    imp = imp + p[:, first + r * Q_TILE:first + (r + 1) * Q_TILE]
                imp_scr[g, a, 0:SUBLANES, :] = jnp.zeros((SUBLANES, Q_TILE), F32)
                imp_scr[g, a, SUBLANES:SUBLANES + rows, :] = imp
                if rows < nc:
                    imp_scr[g, a, SUBLANES + rows:SUBLANES + nc, :] = jnp.zeros((nc - rows, Q_TILE), F32)

    n_variants = max(nc // CMP_ROWS, 1)
    visible = cmp_per_q * QT_STEP * (step + 1)
    variant = jnp.minimum((visible + CMP_ROWS - 1) // CMP_ROWS, n_variants)
    for v in range(1, n_variants + 1):
        pl.when(variant == v)(functools.partial(compressed, min(v * CMP_ROWS, nc)))

    block_scores = []
    for g in range(NSA_G):
        p_slc = []
        for a in range(QT_STEP):
            a_k = [imp_scr[g, a, pl.ds(SUBLANES - 1 + k, nsb, stride=cmp_per_slc), :]
                   for k in range(cmp_per_slc + 1)]
            acc = a_k[0] + a_k[1]
            for k in range(1, cmp_per_slc):
                acc = acc + a_k[k] + a_k[k + 1]
            p_slc.append(acc)
        block_scores.append(lanes(p_slc))

    jidx = lax.broadcasted_iota(jnp.int32, (nsb, STEP_Q), 0)
    tq = t0 + lax.broadcasted_iota(jnp.int32, (nsb, STEP_Q), 1)
    valid = jidx * SLC_LEN <= tq

    def store_masks(masks):
        for g in range(NSA_G):
            group_scr[g * per_group + 1][...] = _dot(expand_ref[...], masks[g].astype(BF)).astype(BF)

    n_forced = 3
    few_blocks_steps = (n_sel * SLC_LEN) // STEP_Q

    @pl.when(step < few_blocks_steps)
    def _():
        store_masks([jnp.where(valid, 0.0, NEG_INF)] * NSA_G)

    def pick(rows):
        j = lax.broadcasted_iota(jnp.int32, (rows, STEP_Q), 0)
        tq_r = t0 + lax.broadcasted_iota(jnp.int32, (rows, STEP_Q), 1)
        valid_r = j * SLC_LEN <= tq_r
        cur = tq_r // SLC_LEN
        forced = (j == 0) | (j == cur) | (j == cur - 1)
        masks = []
        for g in range(NSA_G):
            score = jnp.where(forced, -jnp.inf, jnp.where(valid_r, block_scores[g][0:rows], NEG_INF))
            for _ in range(n_sel - n_forced):
                mx = jnp.max(score, axis=0, keepdims=True)
                first = jnp.min(jnp.where(score == mx, j, nsb), axis=0, keepdims=True)
                score = jnp.where(j == first, -jnp.inf, score)
            mask = jnp.where(score == -jnp.inf, 0.0, NEG_INF)
            if rows < nsb:
                mask = jnp.concatenate([mask, jnp.full((nsb - rows, STEP_Q), NEG_INF, F32)], axis=0)
            masks.append(mask)
        store_masks(masks)

    n_pick_variants = max(nsb // PICK_ROWS, 1)
    blocks_started = (STEP_Q // SLC_LEN) * (step + 1)
    pick_variant = jnp.minimum((blocks_started + PICK_ROWS - 1) // PICK_ROWS, n_pick_variants)
    for v in range(1, n_pick_variants + 1):
        pl.when((step >= few_blocks_steps) & (pick_variant == v))(functools.partial(pick, min(v * PICK_ROWS, nsb)))

    n_win_pairs = (QT_STEP + N_WIN_TILES - 1 + 1) // 2
    win_first = n_last - (n_win_pairs - 1)
    fill(SLC, jnp.int32(0), 0)

    def slc_body(j, carry):
        kp = 2 * j
        fill(SLC, kp + 1, 1)
        drain(SLC, kp, 0)
        fill(SLC, kp + 2, 0)
        drain(SLC, kp + 1, 1)
        return carry

    lax.fori_loop(0, n_last // 2, slc_body, 0)

    @pl.when(n_last % 2 == 1)
    def _():
        fill(SLC, n_last, 1)
        fill(WIN_BRANCH, win_first, 0)
        drain(SLC, n_last - 1, 0)
        drain(SLC, n_last, 1)

    @pl.when(n_last % 2 == 0)
    def _():
        fill(WIN_BRANCH, win_first, 0)
        drain(SLC, n_last, 0)

    for i in range(1, n_win_pairs):
        fill(WIN_BRANCH, win_first + i, i % 2)
        drain(WIN_BRANCH, win_first + i - 1, (i - 1) % 2)
    drain(WIN_BRANCH, win_first + n_win_pairs - 1, (n_win_pairs - 1) % 2)

    for g in range(NSA_G):
        def gate(branch):
            return lanes([gt_ref[0, pl.ds((g * NSA_R + r) * 3 + branch, 1), a * Q_TILE:(a + 1) * Q_TILE]
                          for a in range(QT_STEP) for r in range(NSA_R)])

        yt = (gate(0) * group_scr[g * per_group + 2][...] + gate(1) * result(g, SLC)
              + gate(2) * result(g, WIN_BRANCH))
        for a in range(QT_STEP):
            for r in range(NSA_R):
                h = g * NSA_R + r
                first = (a * NSA_R + r) * Q_TILE
                y_scr[h * NSA_DH:(h + 1) * NSA_DH, a * Q_TILE:(a + 1) * Q_TILE] = yt[:, first:first + Q_TILE]

    o_ref[0] = y_scr[...].T


def _nsa(qt, gt, kc, vct, ksw, vt, tz, wz, mc):
    b, _, l = qt.shape
    nc = l // CMP_STRIDE
    nsb = l // SLC_LEN
    per_b = lambda shape, idx: pl.BlockSpec((1,) + shape, lambda bi, i: (bi,) + idx, pipeline_mode=pl.Buffered(1))
    assert l % STEP_Q == 0
    pair = 2 * K_TILE
    blocks_per_pair = pair // SLC_LEN
    n_pairs = l // pair
    ind = (np.arange(pair)[:, None] // SLC_LEN == np.arange(KV_W)[None, :]).astype(np.float32)
    expand = np.zeros((n_pairs * SLAB, nsb), np.float32)
    for k in range(n_pairs):
        for j in range(blocks_per_pair):
            expand[k * SLAB + j, k * blocks_per_pair + j] = 1.0
    ind, expand = jnp.asarray(ind, BF), jnp.asarray(expand, BF)
    return pl.pallas_call(
        functools.partial(_nsa_kernel, nc=nc, nsb=nsb),
        grid=(b, l // STEP_Q),
        in_specs=[pl.BlockSpec((1, NSA_W, STEP_Q), lambda bi, i: (bi, 0, i)),
                  pl.BlockSpec((1, 32, STEP_Q), lambda bi, i: (bi, 0, i)),
                  per_b((nc, KV_W), (0, 0)), per_b((KV_W, nc), (0, 0)),
                  per_b((l, KV_W), (0, 0)), per_b((l, KV_W), (0, 1)),
                  per_b((KV_W, l), (0, 0)), per_b((KV_W, l), (1, 0)),
                  _const_spec(mc.shape), _const_spec(tz.shape), _const_spec(wz.shape),
                  _const_spec(ind.shape), _const_spec(expand.shape)],
        out_specs=pl.BlockSpec((1, STEP_Q, NSA_W), lambda bi, i: (bi, i, 0)),
        out_shape=jax.ShapeDtypeStruct((b, l, NSA_W), F32),
        scratch_shapes=[pltpu.VMEM((NSA_G, QT_STEP, nc + 4 * SUBLANES, Q_TILE), F32),
                        pltpu.VMEM((NSA_W, STEP_Q), F32)]
                       + NSA_G * ([pltpu.VMEM((KV_W, STEP_LANES), BF), pltpu.VMEM((n_pairs * SLAB, STEP_Q), BF),
                                   pltpu.VMEM((NSA_DH, STEP_LANES), F32)]
                                  + 2 * [pltpu.VMEM((SUBLANES, STEP_LANES), F32),
                                         pltpu.VMEM((NSA_DH + SLAB, STEP_LANES), F32),
                                         pltpu.VMEM((pair, STEP_LANES), F32), pltpu.VMEM((pair, STEP_LANES), F32),
                                         pltpu.VMEM((SUBLANES, STEP_LANES), F32),
                                         pltpu.VMEM((SUBLANES, STEP_LANES), F32)]),
        compiler_params=_cparams("arbitrary", "arbitrary"),
        name="nsa",
    )(qt, gt, kc, vct, ksw, ksw, vt, vt, mc, tz, wz, ind, expand)


def _split_w_in(w_in):
    w_in = w_in.astype(BF)
    o_q = N_TOK_A
    o_kv = o_q + NSA_W
    o_gate = o_kv + 6 * KV_W
    kv = lambda j: w_in[:, o_kv + j * KV_W:o_kv + (j + 1) * KV_W]
    w_tok = jnp.concatenate([w_in[:, :N_TOK_A], kv(0), kv(1), kv(2), kv(4)], axis=1).astype(BF)
    gates = w_in[:, o_gate:]
    pad = jnp.zeros((w_in.shape[0], 32 - gates.shape[1]), w_in.dtype)
    w_t = jnp.concatenate([w_in[:, o_q:o_kv], kv(3), kv(5), gates, pad], axis=1).T.astype(BF)
    return w_tok, w_t


def kernel(x, ffn1_norm, ffn1_w_gate, ffn1_w_up, ffn1_w_down, mix_norm, w_in, w_out, s5_lambda_re, s5_lambda_im, s5_log_dt, s5_b_re, s5_b_im, s5_c_re, s5_c_im, s5_d, s5_w_glu, hgrn_lb_logits, hgrn_norm, nsa_cmp_pos_k, nsa_cmp_w1_k, nsa_cmp_w2_k, nsa_cmp_pos_v, nsa_cmp_w1_v, nsa_cmp_w2_v, rel_bias, ffn2_norm, ffn2_w_gate, ffn2_w_up, ffn2_w_down, final_norm):
    b, l, d = x.shape
    depth = w_in.shape[0]
    gam = jax.nn.softmax(hgrn_lb_logits.astype(F32), axis=0)
    lower_bounds = jnp.cumsum(gam, axis=0) - gam[0:1]
    tz, wz, mc = _bias_tables(rel_bias, l)
    bf = lambda w: w.astype(BF)

    x2 = x.reshape(b * l, d)
    for i in range(depth):
        x2 = _ffn(x2, ffn1_norm[i], bf(ffn1_w_gate[i]), bf(ffn1_w_up[i]), bf(ffn1_w_down[i]))
        w_tok, w_t = _split_w_in(w_in[i])
        za, kcr, vcr, ksw, qt, vt, gt = _inproj(x2.reshape(b, l, d), mix_norm[i], w_tok, w_t)
        bblk, cblk, coef = _s5_params(s5_lambda_re[i], s5_lambda_im[i], s5_log_dt[i], s5_b_re[i], s5_b_im[i],
                                      s5_c_re[i], s5_c_im[i])
        y_s5 = _s5(za, bblk, cblk, coef, s5_d[i], s5_w_glu[i])
        y_hg = _hgrn(za, lower_bounds[i], hgrn_norm[i])
        kc, vct = _compress(kcr, vcr,
                            _compress_params(nsa_cmp_pos_k[i], nsa_cmp_w1_k[i], nsa_cmp_w2_k[i]),
                            _compress_params(nsa_cmp_pos_v[i], nsa_cmp_w1_v[i], nsa_cmp_w2_v[i], transpose_out=True))
        y_nsa = _nsa(qt, gt, kc, vct, ksw, vt, tz, wz, mc)
        proj = (y_s5.reshape(b * l, S5_W), y_hg.reshape(b * l, HG_W), y_nsa.reshape(b * l, NSA_W), bf(w_out[i]))
        x2 = _ffn(x2, ffn2_norm[i], bf(ffn2_w_gate[i]), bf(ffn2_w_up[i]), bf(ffn2_w_down[i]), proj=proj,
                  final_g=final_norm if i + 1 == depth else None)
    return x2.reshape(b, l, d)
```

```python
import functools
import math

import numpy as np
import jax
import jax.numpy as jnp
from jax import lax
from jax.experimental import pallas as pl
from jax.experimental.pallas import tpu as pltpu

BF = jnp.bfloat16
F32 = jnp.float32

EPS = 1e-6
NEG_INF = -1e30
TINY = 1e-30
LOG2E = math.log2(math.e)

S5_W = 256
S5_G = 16
S5_GROUP = 16
S5_P = 64
S5_N = S5_G * S5_P
HG_W = 256
HG_HEADS = 4
HG_D = 64
NSA_W = 512
NSA_DH = 64
NSA_H = 8
NSA_G = 2
NSA_R = 4
KV_W = NSA_G * NSA_DH
CMP_LEN = 32
CMP_STRIDE = 16
SLC_LEN = 64
N_SEL = 16
WIN = 512
REL_BUCKETS = 32
REL_MAX_DIST = 1024

Q_TILE = 128
K_TILE = 128
GR_LANES = NSA_R * Q_TILE
QT_STEP = 4
STEP_Q = QT_STEP * Q_TILE
STEP_LANES = QT_STEP * GR_LANES
N_TOEPLITZ = REL_MAX_DIST // K_TILE + 2
N_WIN_TILES = WIN // K_TILE + 1
SLAB = 16
CMP_ROWS = 128
PICK_ROWS = 32
HG_CHUNK = 64
HG_GROUP = 4
HG_LEVELS = 6
SUBLANES = 8
HG_FINE_LEVELS = 3
VMEM_LIMIT_BYTES = 56 * 1024 * 1024


def _cparams(*sem):
    return pltpu.CompilerParams(dimension_semantics=sem, vmem_limit_bytes=VMEM_LIMIT_BYTES)


def _const_spec(shape):
    nd = len(shape)
    return pl.BlockSpec(shape, lambda *_: (0,) * nd, pipeline_mode=pl.Buffered(1))


def _rms(x, g_row):
    ms = jnp.mean(x * x, axis=-1, keepdims=True)
    return x * lax.rsqrt(ms + EPS) * g_row


def _silu(x):
    return x * jax.nn.sigmoid(x)


def _gelu_tanh(x):
    return 0.5 * x * (1.0 + jnp.tanh(math.sqrt(2.0 / math.pi) * (x + 0.044715 * (x * x * x))))


def _dot(a, b):
    return jnp.dot(a, b, preferred_element_type=F32)


def _dot_nt(a, b):
    return lax.dot_general(a, b, (((1,), (1,)), ((), ())), preferred_element_type=F32)


def _dot_tn(a, b):
    return lax.dot_general(a, b, (((0,), (0,)), ((), ())), preferred_element_type=F32)


def _dot_exact_lhs(c_bf, x):
    hi = x.astype(BF)
    r1 = x - hi.astype(F32)
    mid = r1.astype(BF)
    lo = (r1 - mid.astype(F32)).astype(BF)
    return _dot(c_bf, hi) + _dot(c_bf, mid) + _dot(c_bf, lo)


def _dot_exact_rhs(x, c_bf):
    hi = x.astype(BF)
    r1 = x - hi.astype(F32)
    mid = r1.astype(BF)
    lo = (r1 - mid.astype(F32)).astype(BF)
    return _dot(hi, c_bf) + _dot(mid, c_bf) + _dot(lo, c_bf)


def _ffn_kernel(*refs, n_chunks, tf, with_proj, with_final):
    it = iter(refs)
    x_ref = next(it)
    if with_proj:
        ys5_ref, yhg_ref, ynsa_ref, wo_ref = next(it), next(it), next(it), next(it)
    g_ref, wg_ref, wu_ref, wd_ref = next(it), next(it), next(it), next(it)
    if with_final:
        fg_ref = next(it)
    o_ref, h_scr, a_scr = next(it), next(it), next(it)

    x = x_ref[...]
    if with_proj:
        x = (x + _dot(ys5_ref[...].astype(BF), wo_ref[0:S5_W, :])
             + _dot(yhg_ref[...].astype(BF), wo_ref[S5_W:S5_W + HG_W, :])
             + _dot(ynsa_ref[...].astype(BF), wo_ref[S5_W + HG_W:, :]))
    h_scr[...] = _rms(x, g_ref[...]).astype(BF)
    for c in range(n_chunks):
        sl = slice(c * tf, (c + 1) * tf)
        h = h_scr[...]
        gate = _dot(h, wg_ref[:, sl])
        up = _dot(h, wu_ref[:, sl])
        a_scr[:, sl] = (_silu(gate) * up).astype(BF)
    x = x + 0.5 * _dot(a_scr[...], wd_ref[...])
    if with_final:
        x = _rms(x, fg_ref[...])
    o_ref[...] = x


def _ffn(x2, g, wg, wu, wd, proj=None, final_g=None, tm=512, tf=256):
    n, d = x2.shape
    dff = wg.shape[1]
    assert n % tm == 0 and dff % tf == 0
    row = lambda i: (i, 0)
    in_specs = [pl.BlockSpec((tm, d), row)]
    args = [x2]
    if proj is not None:
        ys5, yhg, ynsa, wo = proj
        in_specs += [pl.BlockSpec((tm, S5_W), row), pl.BlockSpec((tm, HG_W), row),
                     pl.BlockSpec((tm, NSA_W), row), _const_spec(wo.shape)]
        args += [ys5, yhg, ynsa, wo]
    in_specs += [_const_spec((1, d)), _const_spec(wg.shape), _const_spec(wu.shape), _const_spec(wd.shape)]
    args += [g.reshape(1, d), wg, wu, wd]
    if final_g is not None:
        in_specs.append(_const_spec((1, d)))
        args.append(final_g.reshape(1, d))
    kern = functools.partial(_ffn_kernel, n_chunks=dff // tf, tf=tf,
                             with_proj=proj is not None, with_final=final_g is not None)
    return pl.pallas_call(
        kern,
        grid=(n // tm,),
        in_specs=in_specs,
        out_specs=pl.BlockSpec((tm, d), row),
        out_shape=jax.ShapeDtypeStruct((n, d), F32),
        scratch_shapes=[pltpu.VMEM((tm, d), BF), pltpu.VMEM((tm, dff), BF)],
        compiler_params=_cparams("arbitrary"),
        name="ffn",
    )(*args)


N_TOK_A = S5_W + 4 * HG_W
N_T_ROWS = NSA_W + 2 * KV_W + 32


def _inproj_kernel(x_ref, g_ref, wtok_ref, wt_ref, za_ref, kcr_ref, vcr_ref, ksw_ref, qt_ref, vt_ref, gt_ref):
    h = _rms(x_ref[0], g_ref[...]).astype(BF)
    for c in range(N_TOK_A // 256):
        sl = slice(c * 256, (c + 1) * 256)
        za_ref[0, :, sl] = _dot(h, wtok_ref[:, sl])
    kcr_ref[0] = _dot(h, wtok_ref[:, N_TOK_A:N_TOK_A + KV_W])
    vcr_ref[0] = _dot(h, wtok_ref[:, N_TOK_A + KV_W:N_TOK_A + 2 * KV_W])
    ksw_ref[0] = _dot(h, wtok_ref[:, N_TOK_A + 2 * KV_W:N_TOK_A + 4 * KV_W]).astype(BF)
    qt_ref[0] = (_dot_nt(wt_ref[0:NSA_W, :], h) * (NSA_DH ** -0.5 * LOG2E)).astype(BF)
    vt_ref[0] = _dot_nt(wt_ref[NSA_W:NSA_W + 2 * KV_W, :], h).astype(BF)
    gt_ref[0] = jax.nn.sigmoid(_dot_nt(wt_ref[NSA_W + 2 * KV_W:, :], h))


def _inproj(x3, g, w_tok, w_t, tm=512):
    b, l, d = x3.shape
    assert l % tm == 0
    tok = lambda w: pl.BlockSpec((1, tm, w), lambda bi, i: (bi, i, 0))
    tr = lambda r: pl.BlockSpec((1, r, tm), lambda bi, i: (bi, 0, i))
    return pl.pallas_call(
        _inproj_kernel,
        grid=(b, l // tm),
        in_specs=[tok(d), _const_spec((1, d)), _const_spec(w_tok.shape), _const_spec(w_t.shape)],
        out_specs=[tok(N_TOK_A), tok(KV_W), tok(KV_W), tok(2 * KV_W), tr(NSA_W), tr(2 * KV_W), tr(32)],
        out_shape=[jax.ShapeDtypeStruct((b, l, N_TOK_A), F32),
                   jax.ShapeDtypeStruct((b, l, KV_W), F32),
                   jax.ShapeDtypeStruct((b, l, KV_W), F32),
                   jax.ShapeDtypeStruct((b, l, 2 * KV_W), BF),
                   jax.ShapeDtypeStruct((b, NSA_W, l), BF),
                   jax.ShapeDtypeStruct((b, 2 * KV_W, l), BF),
                   jax.ShapeDtypeStruct((b, 32, l), F32)],
        compiler_params=_cparams("arbitrary", "arbitrary"),
        name="inproj",
    )(x3, g.reshape(1, d), w_tok, w_t)


def _s5_kernel(u_ref, bblk_ref, cblk_ref, coef_ref, d_ref, wglu_ref, y_ref, xs_scr, carry_scr, *, tc):
    @pl.when(pl.program_id(1) == 0)
    def _():
        carry_scr[...] = jnp.zeros_like(carry_scr)

    u = u_ref[0]
    xs_scr[...] = _dot(u.astype(BF), bblk_ref[...])

    def body(r, carry):
        cre, cim = carry
        row = pl.multiple_of(r * SUBLANES, SUBLANES)
        xre = xs_scr[pl.ds(row, SUBLANES), 0:S5_N]
        xim = xs_scr[pl.ds(row, SUBLANES), S5_N:2 * S5_N]
        for idx, k in enumerate((1, 2, 4)):
            are, aim = coef_ref[idx, 0], coef_ref[idx, 1]
            sre, sim = pltpu.roll(xre, k, 0), pltpu.roll(xim, k, 0)
            xre, xim = xre + (are * sre - aim * sim), xim + (are * sim + aim * sre)
        pre, pim = coef_ref[3, 0], coef_ref[3, 1]
        xre, xim = xre + (pre * cre - pim * cim), xim + (pre * cim + pim * cre)
        xs_scr[pl.ds(row, SUBLANES), 0:S5_N] = xre
        xs_scr[pl.ds(row, SUBLANES), S5_N:2 * S5_N] = xim
        return xre[SUBLANES - 1:SUBLANES], xim[SUBLANES - 1:SUBLANES]

    cre, cim = lax.fori_loop(0, tc // SUBLANES, body, (carry_scr[0:1], carry_scr[1:2]))
    carry_scr[0:1] = cre
    carry_scr[1:2] = cim

    y = _dot(xs_scr[...].astype(BF), cblk_ref[...]) + d_ref[...] * u
    y = _gelu_tanh(y)
    y_ref[0] = y * jax.nn.sigmoid(_dot(y.astype(BF), wglu_ref[...]))


def _s5_params(lam_re, lam_im, log_dt, b_re, b_im, c_re, c_im):
    lr, li = lam_re.astype(F32), lam_im.astype(F32)
    dt = jnp.exp(log_dt.astype(F32))[:, None]
    mag = jnp.exp(lr * dt)
    ab_re, ab_im = mag * jnp.cos(li * dt), mag * jnp.sin(li * dt)
    den = lr * lr + li * li
    nr, ni = ab_re - 1.0, ab_im
    g_re = (nr * lr + ni * li) / den
    g_im = (ni * lr - nr * li) / den
    br, bi = b_re.astype(F32), b_im.astype(F32)
    bb_re = g_re[..., None] * br - g_im[..., None] * bi
    bb_im = g_re[..., None] * bi + g_im[..., None] * br
    eye = jnp.eye(S5_G, dtype=F32)
    blk = lambda w: jnp.einsum('gph,gk->ghkp', w, eye).reshape(S5_W, S5_N)
    bblk = jnp.concatenate([blk(bb_re), blk(bb_im)], axis=1).astype(BF)
    cblk_f = lambda w: jnp.einsum('ghp,gk->gpkh', w, eye).reshape(S5_N, S5_W)
    cblk = jnp.concatenate([cblk_f(c_re.astype(F32)), -cblk_f(c_im.astype(F32))], axis=0).astype(BF)
    are, aim = ab_re.reshape(1, S5_N), ab_im.reshape(1, S5_N)
    pw = [(are, aim)]
    for _ in range(SUBLANES - 1):
        pr, pi = pw[-1]
        pw.append((pr * are - pi * aim, pr * aim + pi * are))
    rows = np.arange(SUBLANES)[:, None]
    coef = []
    for k in (1, 2, 4):
        m = jnp.asarray((rows >= k).astype(np.float32))
        coef.append(jnp.stack([m * pw[k - 1][0], m * pw[k - 1][1]]))
    coef.append(jnp.stack([jnp.concatenate([p[0] for p in pw], axis=0), jnp.concatenate([p[1] for p in pw], axis=0)]))
    return bblk, cblk, jnp.stack(coef)


def _s5(za, bblk, cblk, coef, d, w_glu, tc=1024):
    b, l, _ = za.shape
    assert l % tc == 0
    return pl.pallas_call(
        functools.partial(_s5_kernel, tc=tc),
        grid=(b, l // tc),
        in_specs=[pl.BlockSpec((1, tc, S5_W), lambda bi, i: (bi, i, 0)),
                  _const_spec(bblk.shape), _const_spec(cblk.shape), _const_spec(coef.shape),
                  _const_spec((1, S5_W)), _const_spec(w_glu.shape)],
        out_specs=pl.BlockSpec((1, tc, S5_W), lambda bi, i: (bi, i, 0)),
        out_shape=jax.ShapeDtypeStruct((b, l, S5_W), F32),
        scratch_shapes=[pltpu.VMEM((tc, 2 * S5_N), F32), pltpu.VMEM((2, S5_N), F32)],
        compiler_params=_cparams("arbitrary", "arbitrary"),
        name="s5",
    )(za, bblk, cblk, coef, d.reshape(1, S5_W).astype(F32), w_glu.astype(BF))


def _hgrn_constants():
    c = HG_CHUNK
    t = np.arange(c)[:, None]
    u = np.arange(c)[None, :]
    mats = [(u <= t)]
    masks = []
    for lv in range(HG_LEVELS):
        n = c >> lv
        half = n // 2
        ref = (t // n) * n + half - 1
        lower = (t % n) >= half
        if half < SUBLANES:
            mats.append(np.where(lower, (u > ref) & (u <= t), (u > t) & (u <= ref)))
        same = (t // n) == (u // n)
        masks.append(same & lower & ((u % n) < half))
    masks.append(t == u)
    gall = np.concatenate(mats, axis=0).astype(np.float32)
    mstk = np.stack([np.tile(m, (1, HG_HEADS)) for m in masks]).astype(np.float32)
    lane_head = np.arange(HG_W)[None, :] // HG_D
    hmask = (np.repeat(np.arange(HG_HEADS), c)[:, None] == lane_head).astype(np.float32)
    bd = (np.arange(HG_W)[:, None] // HG_D == lane_head).astype(np.float32)
    return gall, mstk, hmask, bd


def _hgrn_kernel(q_ref, f_ref, i_ref, g_ref, lb_ref, gain_ref, gall_ref, mstk_ref, hmask_ref, bd_ref, bdn_ref,
                 o_ref, st_scr, *, tt):
    c = HG_CHUNK

    @pl.when(pl.program_id(1) == 0)
    def _():
        st_scr[...] = jnp.zeros_like(st_scr)

    lb = lb_ref[...]
    hmask = hmask_ref[...]

    def level_decay(b, sums, lv):
        n = c >> lv
        half = n // 2
        if half < SUBLANES:
            fine = lv - (HG_LEVELS - HG_FINE_LEVELS)
            return jnp.exp(sums[(1 + fine) * c:(2 + fine) * c])
        pieces = []
        for blk in range(c // n):
            ref = blk * n + half - 1
            pieces.append(b[ref:ref + 1] - b[blk * n:blk * n + half])
            pieces.append(b[blk * n + half:(blk + 1) * n] - b[ref:ref + 1])
        return jnp.exp(jnp.concatenate(pieces, axis=0))

    def chunk(row, fl, sums, carry):
        qf = _silu(q_ref[0, pl.ds(row, c), :])
        kf = (1.0 - lb) * jax.nn.sigmoid(-fl)
        v = i_ref[0, pl.ds(row, c), :]
        v_bf = v.astype(BF)
        b = sums[0:c]
        e_b = jnp.exp(b)
        e_suf = jnp.exp(b[c - 1:c] - b)

        att = jnp.zeros((c, HG_HEADS * c), F32)
        for lv in range(HG_LEVELS + 1):
            if lv < HG_LEVELS:
                e = level_decay(b, sums, lv)
                z, w = qf * e, kf * e
            else:
                z, w = qf, kf
            ws = (jnp.concatenate([w] * HG_HEADS, axis=0) * hmask).astype(BF)
            att = att + mstk_ref[lv] * _dot_nt(z.astype(BF), ws)
        o = _dot(att.astype(BF), (jnp.concatenate([v] * HG_HEADS, axis=0) * hmask).astype(BF))

        st = st_scr[...]
        o = o + _dot_nt((qf * e_b).astype(BF), st.astype(BF))
        st_scr[...] = e_b[c - 1:c] * st + bd_ref[...] * _dot_tn(v_bf, (kf * e_suf).astype(BF))

        ms = _dot_exact_rhs(o * o, bdn_ref[...])
        o = o * lax.rsqrt(ms + EPS) * gain_ref[...]
        o_ref[0, pl.ds(row, c), :] = o * _silu(g_ref[0, pl.ds(row, c), :])
        return carry

    def chunk_group(cg, carry):
        rows = [pl.multiple_of((HG_GROUP * cg + j) * c, c) for j in range(HG_GROUP)]
        fls = [f_ref[0, pl.ds(r, c), :] for r in rows]
        lfs = [jnp.log(jnp.maximum(lb + (1.0 - lb) * jax.nn.sigmoid(fl), TINY)) for fl in fls]
        sums = _dot_exact_lhs(gall_ref[...], jnp.concatenate(lfs, axis=1))
        for j in range(HG_GROUP):
            carry = chunk(rows[j], fls[j], sums[:, j * HG_W:(j + 1) * HG_W], carry)
        return carry

    lax.fori_loop(0, tt // (HG_GROUP * c), chunk_group, 0)


def _hgrn(za, lb, gain, tt=512):
    b, l, _ = za.shape
    assert l % tt == 0
    gall, mstk, hmask, bd = _hgrn_constants()
    col = lambda j: pl.BlockSpec((1, tt, HG_W), lambda bi, i: (bi, i, j))
    consts = [jnp.asarray(gall, BF), jnp.asarray(mstk, F32), jnp.asarray(hmask, F32), jnp.asarray(bd, F32),
              jnp.asarray(bd / HG_D, BF)]
    return pl.pallas_call(
        functools.partial(_hgrn_kernel, tt=tt),
        grid=(b, l // tt),
        in_specs=[col(1), col(2), col(3), col(4), _const_spec((1, HG_W)), _const_spec((1, HG_W))]
                 + [_const_spec(x.shape) for x in consts],
        out_specs=pl.BlockSpec((1, tt, HG_W), lambda bi, i: (bi, i, 0)),
        out_shape=jax.ShapeDtypeStruct((b, l, HG_W), F32),
        scratch_shapes=[pltpu.VMEM((HG_W, HG_W), F32)],
        compiler_params=_cparams("arbitrary", "arbitrary"),
        name="hgrn2",
    )(za, za, za, za, lb.reshape(1, HG_W).astype(F32), jnp.tile(gain.astype(F32), HG_HEADS).reshape(1, HG_W),
      *consts)


def _compress_kernel(k16_ref, v16_ref, kpa_ref, kpb_ref, kw1a_ref, kw1b_ref, kw2_ref,
                     vpa_ref, vpb_ref, vw1a_ref, vw1b_ref, vw2_ref, kc_ref, vct_ref, *, nc):
    def hidden(x_ref, pa, pb, w1a, w1b):
        x16 = jnp.concatenate([x_ref[0, pl.ds(j, nc, stride=CMP_STRIDE), :] for j in range(CMP_STRIDE)], axis=1)
        first = _dot((x16 + pa).astype(BF), w1a)
        second = _dot((x16 + pb).astype(BF), w1b)
        pre = first + pltpu.roll(second, nc - 1, 0)
        rows = lax.broadcasted_iota(jnp.int32, pre.shape, 0)
        pre = jnp.where(rows < nc - 1, pre, 0.0)
        return _gelu_tanh(pre).astype(BF)

    hk = hidden(k16_ref, kpa_ref[...], kpb_ref[...], kw1a_ref[...], kw1b_ref[...])
    kc_ref[0] = _dot(hk, kw2_ref[...]).astype(BF)
    hv = hidden(v16_ref, vpa_ref[...], vpb_ref[...], vw1a_ref[...], vw1b_ref[...])
    vct_ref[0] = _dot_nt(vw2_ref[...], hv).astype(BF)


def _compress_params(pos, w1, w2, transpose_out=False):
    eye = jnp.eye(NSA_G, dtype=F32)
    w1r = w1.astype(F32).reshape(CMP_LEN, NSA_DH, NSA_DH)
    wexp = jnp.einsum('jde,gh->jgdhe', w1r, eye).reshape(CMP_LEN, KV_W, KV_W)
    half = CMP_LEN // 2
    w1a = wexp[:half].reshape(half * KV_W, KV_W).astype(BF)
    w1b = wexp[half:].reshape(half * KV_W, KV_W).astype(BF)
    pt = jnp.broadcast_to(pos.astype(F32)[:, None, :], (CMP_LEN, NSA_G, NSA_DH))
    pa = pt[:half].reshape(1, half * KV_W)
    pb = pt[half:].reshape(1, half * KV_W)
    w2bd = jnp.einsum('de,gh->gdhe', w2.astype(F32), eye).reshape(KV_W, KV_W).astype(BF)
    return pa, pb, w1a, w1b, (w2bd.T if transpose_out else w2bd)


def _compress(kcr, vcr, kparams, vparams):
    b, l, _ = kcr.shape
    nc = l // CMP_STRIDE
    per_b = lambda r, cc: pl.BlockSpec((1, r, cc), lambda bi: (bi, 0, 0))
    params = list(kparams) + list(vparams)
    return pl.pallas_call(
        functools.partial(_compress_kernel, nc=nc),
        grid=(b,),
        in_specs=[per_b(l, KV_W), per_b(l, KV_W)] + [_const_spec(p.shape) for p in params],
        out_specs=[per_b(nc, KV_W), per_b(KV_W, nc)],
        out_shape=[jax.ShapeDtypeStruct((b, nc, KV_W), BF), jax.ShapeDtypeStruct((b, KV_W, nc), BF)],
        compiler_params=_cparams("arbitrary"),
        name="compress",
    )(kcr, vcr, *params)


def _t5_bucket(dist):
    n = jnp.maximum(dist, 0)
    max_exact = REL_BUCKETS // 2
    nf = jnp.maximum(n, max_exact).astype(jnp.float32)
    large = max_exact + (jnp.log(nf / max_exact) / math.log(REL_MAX_DIST / max_exact)
                         * (REL_BUCKETS - max_exact)).astype(jnp.int32)
    large = jnp.minimum(large, REL_BUCKETS - 1)
    return jnp.where(n < max_exact, n, large)


def _bias_tables(rel_bias, l):
    tab = rel_bias.astype(F32) * LOG2E
    by_dist = tab[_t5_bucket(jnp.arange(l))].T
    nc = l // CMP_STRIDE
    w = 2 * K_TILE

    def shifted_rows(v, n_rows, step):
        flat = jnp.tile(v, (1,) * (v.ndim - 1) + (n_rows,))[..., :n_rows * (w - step)]
        return flat.reshape(v.shape[:-1] + (n_rows, w - step))[..., :Q_TILE]

    def group_layout(t, valid):
        _, n, r, q = t.shape
        t = jnp.where(jnp.asarray(valid)[None], t, NEG_INF)
        return t.reshape(NSA_G, NSA_R, n, r, q).transpose(0, 2, 3, 1, 4).reshape(NSA_G, n, r, NSA_R * q)

    tq = np.arange(Q_TILE)[None, None, :]
    key = np.arange(K_TILE)[None, :, None]
    blocks = jnp.pad(by_dist, ((0, 0), (K_TILE, w)), mode='edge').reshape(NSA_H, -1, K_TILE)
    vec = jnp.concatenate([blocks[:, 1:N_TOEPLITZ + 1], blocks[:, 0:N_TOEPLITZ]], axis=-1)
    toep = shifted_rows(vec, K_TILE, 1)
    masked_tile = jnp.full((NSA_G, 1, K_TILE, GR_LANES), NEG_INF, F32)
    d = np.arange(N_TOEPLITZ)[:, None, None] * K_TILE + tq - key
    tz = jnp.concatenate([masked_tile, group_layout(toep, d >= 0)], axis=1)
    d = (N_WIN_TILES - 1 - np.arange(N_WIN_TILES))[:, None, None] * K_TILE + tq - key
    wz = group_layout(toep[:, N_WIN_TILES - 1::-1], (d >= 0) & (d < WIN))
    wz = jnp.concatenate([masked_tile, wz, masked_tile], axis=1)
    na = (2 * nc - SUBLANES) // SUBLANES
    c0 = CMP_STRIDE * (nc - SUBLANES) - (CMP_LEN - 1)
    front = K_TILE * (na + 1)
    off = c0 % K_TILE
    padded = jnp.pad(by_dist, ((0, 0), (front, w)), mode='edge')
    nb = (padded.shape[1] - off) // K_TILE
    blocks = padded[:, off:off + nb * K_TILE].reshape(NSA_H, nb, K_TILE)
    k0 = (front + c0 - off) // K_TILE
    vec = jnp.concatenate([blocks[:, k0 - na + 1:k0 + 1][:, ::-1], blocks[:, k0 - na:k0][:, ::-1]], axis=-1)
    cmp_rows = shifted_rows(vec, SUBLANES, CMP_STRIDE).reshape(NSA_H, 1, na * SUBLANES, Q_TILE)
    u = np.arange(na * SUBLANES)[None, :, None]
    mc = group_layout(cmp_rows, tq - CMP_STRIDE * (u - (nc - SUBLANES)) - (CMP_LEN - 1) >= 0)[:, 0]
    return tz, wz, mc


def _nsa_kernel(qt_ref, gt_ref, kc_ref, vct_ref, ksl_ref, kwn_ref, vslt_ref, vwnt_ref, mc_ref, tz_ref, wz_ref,
                ind_ref, expand_ref, o_ref, imp_scr, y_scr, *group_scr, nc, nsb):
    per_group = len(group_scr) // NSA_G
    SLC, WIN_BRANCH = 0, 1

    def branch_scr(g, branch):
        m_scr, acc_scr, s_a, s_b, max_a, max_b = group_scr[g * per_group + 3 + 6 * branch:
                                                           g * per_group + 9 + 6 * branch]
        return m_scr, acc_scr, (s_a, s_b), (max_a, max_b)

    step = pl.program_id(1)
    tiles = [QT_STEP * step + a for a in range(QT_STEP)]
    t0 = step * STEP_Q
    n_sel = min(N_SEL, nsb)
    cmp_per_q = Q_TILE // CMP_STRIDE
    cmp_per_slc = SLC_LEN // CMP_STRIDE
    pair = 2 * K_TILE
    n_last = tiles[-1] // 2
    ones_rows = (lax.broadcasted_iota(jnp.int32, (SLAB, pair), 0) == 0).astype(BF)

    def lanes(parts):
        return jnp.concatenate(parts, axis=1)

    def per_head_lanes(x):
        return lanes([x[:, a * Q_TILE:(a + 1) * Q_TILE] for a in range(QT_STEP) for _ in range(NSA_R)])

    def group_rows(g):
        return slice(g * NSA_DH, (g + 1) * NSA_DH)

    def fill(branch, kp, buf):
        k0 = pl.multiple_of(jnp.maximum(kp, 0) * pair, pair)
        for g in range(NSA_G):
            qt_scr, slab_scr = group_scr[g * per_group:g * per_group + 2]
            _, _, s_bufs, max_bufs = branch_scr(g, branch)
            if branch == SLC:
                lhs = jnp.concatenate([ksl_ref[0, pl.ds(k0, pair), :], ind_ref[...]], axis=1)
                slab = per_head_lanes(slab_scr[pl.ds(pl.multiple_of(kp * SLAB, SLAB), SLAB), :])
                rhs = jnp.concatenate([qt_scr[...], slab, jnp.zeros((KV_W - SLAB, STEP_LANES), BF)], axis=0)
                bias = [lanes([tz_ref[g, jnp.clip(qi - (2 * kp + j), -1, N_TOEPLITZ - 1) + 1] for qi in tiles])
                        for j in range(2)]
            else:
                lhs, rhs = kwn_ref[0, pl.ds(k0, pair), :], qt_scr[...]
                bias = [lanes([wz_ref[g, jnp.where(kp < 0, 0, jnp.clip(2 * kp + j - (qi - (N_WIN_TILES - 1)) + 1,
                                                                       0, N_WIN_TILES + 1))]
                               for qi in tiles]) for j in range(2)]
            s = _dot(lhs, rhs) + jnp.concatenate(bias, axis=0)
            s_bufs[buf][...] = s
            max_bufs[buf][0:1, :] = jnp.max(s, axis=0, keepdims=True)

    def drain(branch, kp, buf):
        vt_ref = vslt_ref if branch == SLC else vwnt_ref
        k0 = pl.multiple_of(jnp.maximum(kp, 0) * pair, pair)
        for g in range(NSA_G):
            m_scr, acc_scr, s_bufs, max_bufs = branch_scr(g, branch)
            m_prev = m_scr[0:1, :]
            m_new = jnp.maximum(m_prev, max_bufs[buf][0:1, :])
            p = jnp.exp2(s_bufs[buf][...] - m_new).astype(BF)
            vt = jnp.concatenate([vt_ref[0, group_rows(g), pl.ds(k0, pair)], ones_rows], axis=0)
            acc_scr[...] = jnp.exp2(m_prev - m_new) * acc_scr[...] + _dot(vt, p)
            m_scr[0:1, :] = m_new

    def result(g, branch):
        _, acc_scr, _, _ = branch_scr(g, branch)
        return acc_scr[0:NSA_DH] * (1.0 / acc_scr[NSA_DH:NSA_DH + 1])

    for g in range(NSA_G):
        qt_scr = group_scr[g * per_group]
        zeros = jnp.zeros((NSA_DH, Q_TILE), BF)
        parts = []
        for a in range(QT_STEP):
            for r in range(NSA_R):
                h = g * NSA_R + r
                qh = qt_ref[0, h * NSA_DH:(h + 1) * NSA_DH, a * Q_TILE:(a + 1) * Q_TILE]
                parts.append(jnp.concatenate([qh, zeros] if g == 0 else [zeros, qh], axis=0))
        qt_scr[...] = lanes(parts)
        for branch in (SLC, WIN_BRANCH):
            m_scr, acc_scr, _, _ = branch_scr(g, branch)
            m_scr[...] = jnp.full_like(m_scr, NEG_INF)
            acc_scr[...] = jnp.zeros_like(acc_scr)

    lane = lax.broadcasted_iota(jnp.int32, (1, STEP_LANES), 1)
    t_lane = t0 + (lane // GR_LANES) * Q_TILE + lane % Q_TILE

    def compressed(rows):
        for g in range(NSA_G):
            qt = group_scr[g * per_group][...]
            bias_c = lanes([mc_ref[g, pl.ds(pl.multiple_of((nc - SUBLANES) - cmp_per_q * qi, SUBLANES), rows), :]
                            for qi in tiles])
            s = _dot(kc_ref[0, 0:rows, :], qt) + bias_c
            m = jnp.max(s, axis=0, keepdims=True)
            e = jnp.exp2(s - m)
            lsum = jnp.sum(e, axis=0, keepdims=True)
            inv = jnp.where(t_lane >= CMP_LEN - 1, 1.0 / jnp.maximum(lsum, TINY), 0.0)
            p = e * inv
            group_scr[g * per_group + 2][...] = _dot(vct_ref[0, group_rows(g), 0:rows], p.astype(BF))
            for a in range(QT_STEP):
                first = a * GR_LANES
                imp = p[:, first:first + Q_TILE]
                for r in range(1, NSA_R):
                    imp = imp + p[:, first + r * Q_TILE:first + (r + 1) * Q_TILE]
                imp_scr[g, a, 0:SUBLANES, :] = jnp.zeros((SUBLANES, Q_TILE), F32)
                imp_scr[g, a, SUBLANES:SUBLANES + rows, :] = imp
                if rows < nc:
                    imp_scr[g, a, SUBLANES + rows:SUBLANES + nc, :] = jnp.zeros((nc - rows, Q_TILE), F32)

    n_variants = max(nc // CMP_ROWS, 1)
    visible = cmp_per_q * QT_STEP * (step + 1)
    variant = jnp.minimum((visible + CMP_ROWS - 1) // CMP_ROWS, n_variants)
    for v in range(1, n_variants + 1):
        pl.when(variant == v)(functools.partial(compressed, min(v * CMP_ROWS, nc)))

    block_scores = []
    for g in range(NSA_G):
        p_slc = []
        for a in range(QT_STEP):
            a_k = [imp_scr[g, a, pl.ds(SUBLANES - 1 + k, nsb, stride=cmp_per_slc), :]
                   for k in range(cmp_per_slc + 1)]
            acc = a_k[0] + a_k[1]
            for k in range(1, cmp_per_slc):
                acc = acc + a_k[k] + a_k[k + 1]
            p_slc.append(acc)
        block_scores.append(lanes(p_slc))

    jidx = lax.broadcasted_iota(jnp.int32, (nsb, STEP_Q), 0)
    tq = t0 + lax.broadcasted_iota(jnp.int32, (nsb, STEP_Q), 1)
    valid = jidx * SLC_LEN <= tq

    def store_masks(masks):
        for g in range(NSA_G):
            group_scr[g * per_group + 1][...] = _dot(expand_ref[...], masks[g].astype(BF)).astype(BF)

    n_forced = 3
    few_blocks_steps = (n_sel * SLC_LEN) // STEP_Q

    @pl.when(step < few_blocks_steps)
    def _():
        store_masks([jnp.where(valid, 0.0, NEG_INF)] * NSA_G)

    def pick(rows):
        j = lax.broadcasted_iota(jnp.int32, (rows, STEP_Q), 0)
        tq_r = t0 + lax.broadcasted_iota(jnp.int32, (rows, STEP_Q), 1)
        valid_r = j * SLC_LEN <= tq_r
        cur = tq_r // SLC_LEN
        forced = (j == 0) | (j == cur) | (j == cur - 1)
        masks = []
        for g in range(NSA_G):
            score = jnp.where(forced, -jnp.inf, jnp.where(valid_r, block_scores[g][0:rows], NEG_INF))
            for _ in range(n_sel - n_forced):
                mx = jnp.max(score, axis=0, keepdims=True)
                first = jnp.min(jnp.where(score == mx, j, nsb), axis=0, keepdims=True)
                score = jnp.where(j == first, -jnp.inf, score)
            mask = jnp.where(score == -jnp.inf, 0.0, NEG_INF)
            if rows < nsb:
                mask = jnp.concatenate([mask, jnp.full((nsb - rows, STEP_Q), NEG_INF, F32)], axis=0)
            masks.append(mask)
        store_masks(masks)

    n_pick_variants = max(nsb // PICK_ROWS, 1)
    blocks_started = (STEP_Q // SLC_LEN) * (step + 1)
    pick_variant = jnp.minimum((blocks_started + PICK_ROWS - 1) // PICK_ROWS, n_pick_variants)
    for v in range(1, n_pick_variants + 1):
        pl.when((step >= few_blocks_steps) & (pick_variant == v))(functools.partial(pick, min(v * PICK_ROWS, nsb)))

    n_win_pairs = (QT_STEP + N_WIN_TILES - 1 + 1) // 2
    win_first = n_last - (n_win_pairs - 1)
    fill(SLC, jnp.int32(0), 0)

    def slc_body(j, carry):
        kp = 2 * j
        fill(SLC, kp + 1, 1)
        drain(SLC, kp, 0)
        fill(SLC, kp + 2, 0)
        drain(SLC, kp + 1, 1)
        return carry

    lax.fori_loop(0, n_last // 2, slc_body, 0)

    @pl.when(n_last % 2 == 1)
    def _():
        fill(SLC, n_last, 1)
        fill(WIN_BRANCH, win_first, 0)
        drain(SLC, n_last - 1, 0)
        drain(SLC, n_last, 1)

    @pl.when(n_last % 2 == 0)
    def _():
        fill(WIN_BRANCH, win_first, 0)
        drain(SLC, n_last, 0)

    for i in range(1, n_win_pairs):
        fill(WIN_BRANCH, win_first + i, i % 2)
        drain(WIN_BRANCH, win_first + i - 1, (i - 1) % 2)
    drain(WIN_BRANCH, win_first + n_win_pairs - 1, (n_win_pairs - 1) % 2)

    for g in range(NSA_G):
        def gate(branch):
            return lanes([gt_ref[0, pl.ds((g * NSA_R + r) * 3 + branch, 1), a * Q_TILE:(a + 1) * Q_TILE]
                          for a in range(QT_STEP) for r in range(NSA_R)])

        yt = (gate(0) * group_scr[g * per_group + 2][...] + gate(1) * result(g, SLC)
              + gate(2) * result(g, WIN_BRANCH))
        for a in range(QT_STEP):
            for r in range(NSA_R):
                h = g * NSA_R + r
                first = (a * NSA_R + r) * Q_TILE
                y_scr[h * NSA_DH:(h + 1) * NSA_DH, a * Q_TILE:(a + 1) * Q_TILE] = yt[:, first:first + Q_TILE]

    o_ref[0] = y_scr[...].T


def _nsa(qt, gt, kc, vct, ksw, vt, tz, wz, mc):
    b, _, l = qt.shape
    nc = l // CMP_STRIDE
    nsb = l // SLC_LEN
    per_b = lambda shape, idx: pl.BlockSpec((1,) + shape, lambda bi, i: (bi,) + idx, pipeline_mode=pl.Buffered(1))
    assert l % STEP_Q == 0
    pair = 2 * K_TILE
    blocks_per_pair = pair // SLC_LEN
    n_pairs = l // pair
    ind = (np.arange(pair)[:, None] // SLC_LEN == np.arange(KV_W)[None, :]).astype(np.float32)
    expand = np.zeros((n_pairs * SLAB, nsb), np.float32)
    for k in range(n_pairs):
        for j in range(blocks_per_pair):
            expand[k * SLAB + j, k * blocks_per_pair + j] = 1.0
    ind, expand = jnp.asarray(ind, BF), jnp.asarray(expand, BF)
    return pl.pallas_call(
        functools.partial(_nsa_kernel, nc=nc, nsb=nsb),
        grid=(b, l // STEP_Q),
        in_specs=[pl.BlockSpec((1, NSA_W, STEP_Q), lambda bi, i: (bi, 0, i)),
                  pl.BlockSpec((1, 32, STEP_Q), lambda bi, i: (bi, 0, i)),
                  per_b((nc, KV_W), (0, 0)), per_b((KV_W, nc), (0, 0)),
                  per_b((l, KV_W), (0, 0)), per_b((l, KV_W), (0, 1)),
                  per_b((KV_W, l), (0, 0)), per_b((KV_W, l), (1, 0)),
                  _const_spec(mc.shape), _const_spec(tz.shape), _const_spec(wz.shape),
                  _const_spec(ind.shape), _const_spec(expand.shape)],
        out_specs=pl.BlockSpec((1, STEP_Q, NSA_W), lambda bi, i: (bi, i, 0)),
        out_shape=jax.ShapeDtypeStruct((b, l, NSA_W), F32),
        scratch_shapes=[pltpu.VMEM((NSA_G, QT_STEP, nc + 4 * SUBLANES, Q_TILE), F32),
                        pltpu.VMEM((NSA_W, STEP_Q), F32)]
                       + NSA_G * ([pltpu.VMEM((KV_W, STEP_LANES), BF), pltpu.VMEM((n_pairs * SLAB, STEP_Q), BF),
                                   pltpu.VMEM((NSA_DH, STEP_LANES), F32)]
                                  + 2 * [pltpu.VMEM((SUBLANES, STEP_LANES), F32),
                                         pltpu.VMEM((NSA_DH + SLAB, STEP_LANES), F32),
                                         pltpu.VMEM((pair, STEP_LANES), F32), pltpu.VMEM((pair, STEP_LANES), F32),
                                         pltpu.VMEM((SUBLANES, STEP_LANES), F32),
                                         pltpu.VMEM((SUBLANES, STEP_LANES), F32)]),
        compiler_params=_cparams("arbitrary", "arbitrary"),
        name="nsa",
    )(qt, gt, kc, vct, ksw, ksw, vt, vt, mc, tz, wz, ind, expand)


def _split_w_in(w_in):
    w_in = w_in.astype(BF)
    o_q = N_TOK_A
    o_kv = o_q + NSA_W
    o_gate = o_kv + 6 * KV_W
    kv = lambda j: w_in[:, o_kv + j * KV_W:o_kv + (j + 1) * KV_W]
    w_tok = jnp.concatenate([w_in[:, :N_TOK_A], kv(0), kv(1), kv(2), kv(4)], axis=1).astype(BF)
    gates = w_in[:, o_gate:]
    pad = jnp.zeros((w_in.shape[0], 32 - gates.shape[1]), w_in.dtype)
    w_t = jnp.concatenate([w_in[:, o_q:o_kv], kv(3), kv(5), gates, pad], axis=1).T.astype(BF)
    return w_tok, w_t


def kernel(x, ffn1_norm, ffn1_w_gate, ffn1_w_up, ffn1_w_down, mix_norm, w_in, w_out, s5_lambda_re, s5_lambda_im, s5_log_dt, s5_b_re, s5_b_im, s5_c_re, s5_c_im, s5_d, s5_w_glu, hgrn_lb_logits, hgrn_norm, nsa_cmp_pos_k, nsa_cmp_w1_k, nsa_cmp_w2_k, nsa_cmp_pos_v, nsa_cmp_w1_v, nsa_cmp_w2_v, rel_bias, ffn2_norm, ffn2_w_gate, ffn2_w_up, ffn2_w_down, final_norm):
    b, l, d = x.shape
    depth = w_in.shape[0]
    gam = jax.nn.softmax(hgrn_lb_logits.astype(F32), axis=0)
    lower_bounds = jnp.cumsum(gam, axis=0) - gam[0:1]
    tz, wz, mc = _bias_tables(rel_bias, l)
    bf = lambda w: w.astype(BF)

    x2 = x.reshape(b * l, d)
    for i in range(depth):
        x2 = _ffn(x2, ffn1_norm[i], bf(ffn1_w_gate[i]), bf(ffn1_w_up[i]), bf(ffn1_w_down[i]))
        w_tok, w_t = _split_w_in(w_in[i])
        za, kcr, vcr, ksw, qt, vt, gt = _inproj(x2.reshape(b, l, d), mix_norm[i], w_tok, w_t)
        bblk, cblk, coef = _s5_params(s5_lambda_re[i], s5_lambda_im[i], s5_log_dt[i], s5_b_re[i], s5_b_im[i],
                                      s5_c_re[i], s5_c_im[i])
        y_s5 = _s5(za, bblk, cblk, coef, s5_d[i], s5_w_glu[i])
        y_hg = _hgrn(za, lower_bounds[i], hgrn_norm[i])
        kc, vct = _compress(kcr, vcr,
                            _compress_params(nsa_cmp_pos_k[i], nsa_cmp_w1_k[i], nsa_cmp_w2_k[i]),
                            _compress_params(nsa_cmp_pos_v[i], nsa_cmp_w1_v[i], nsa_cmp_w2_v[i], transpose_out=True))
        y_nsa = _nsa(qt, gt, kc, vct, ksw, vt, tz, wz, mc)
        proj = (y_s5.reshape(b * l, S5_W), y_hg.reshape(b * l, HG_W), y_nsa.reshape(b * l, NSA_W), bf(w_out[i]))
        x2 = _ffn(x2, ffn2_norm[i], bf(ffn2_w_gate[i]), bf(ffn2_w_up[i]), bf(ffn2_w_down[i]), proj=proj,
                  final_g=final_norm if i + 1 == depth else None)
    return x2.reshape(b, l, d)
```
